```python
import jax, jax.numpy as jnp
from jax import lax
import numpy as np

D_MODEL = 1024
BATCH = 16
SEQ = 4096
DEPTH = 1

CTX_LEN = 256
GRID_W = 64
HEAD_DIM = 64
ATTN_WIDTH = D_MODEL // 2
N_Q_HEADS = ATTN_WIDTH // HEAD_DIM
N_KV_HEADS = 2
Q_PER_KV = N_Q_HEADS // N_KV_HEADS
KV_WIDTH = N_KV_HEADS * HEAD_DIM
WINDOW = 128
BLOCK = 128
ROPE_BASE = 10000.0
LRU_WIDTH = D_MODEL - ATTN_WIDTH
LRU_BLOCKS = 8
LRU_BLOCK_DIM = LRU_WIDTH // LRU_BLOCKS
CONV_WIDTH = 4
LRU_C = 8.0
MIX_WIDTH = ATTN_WIDTH + LRU_WIDTH
IN_SPLITS = (ATTN_WIDTH, ATTN_WIDTH + KV_WIDTH, ATTN_WIDTH + 2 * KV_WIDTH,
             ATTN_WIDTH + 2 * KV_WIDTH + LRU_WIDTH)
IN_WIDTH = ATTN_WIDTH + 2 * KV_WIDTH + 2 * LRU_WIDTH
N_EXPERTS = 16
CAPACITY_FACTOR = 2
D_EXPERT = D_MODEL
EPS = 1e-6
NEG_INF = -1e30

kernel_name = "hybrid_attn_rglru_ec_moe_dit_layer"


def rmsnorm(x, g):
    xf = x.astype(jnp.float32)
    xf = xf * lax.rsqrt(jnp.mean(xf * xf, axis=-1, keepdims=True) + EPS)
    return (xf * g.astype(jnp.float32)).astype(x.dtype)


def modulate(h, shift, scale):
    return h * (1 + scale) + shift


def ada_params(cond, w_ada, b_ada):
    m = jax.nn.silu(cond) @ w_ada + b_ada
    return jnp.split(m[..., None, :], 6, axis=-1)


def rope_2d_tables(rows):
    row = jnp.broadcast_to(jnp.arange(rows)[:, None], (rows, GRID_W)).reshape(-1).astype(jnp.float32)
    col = jnp.broadcast_to(jnp.arange(GRID_W)[None, :], (rows, GRID_W)).reshape(-1).astype(jnp.float32)
    n_freq = HEAD_DIM // 4
    inv_freq = ROPE_BASE ** (-jnp.arange(n_freq, dtype=jnp.float32) / n_freq)
    ang = jnp.stack([row[:, None] * inv_freq, col[:, None] * inv_freq], axis=1)
    return jnp.cos(ang), jnp.sin(ang)


def apply_rope_2d(x, cos, sin):
    xs = x.astype(jnp.float32).reshape(x.shape[:-1] + (2, 2, HEAD_DIM // 4))
    x1, x2 = xs[..., 0, :], xs[..., 1, :]
    cs, sn = cos[None, :, None], sin[None, :, None]
    out = jnp.stack([x1 * cs - x2 * sn, x2 * cs + x1 * sn], axis=-2)
    return out.reshape(x.shape).astype(x.dtype)


def windowed_attention_with_context(q, k, v, k_c, v_c, sink):
    bsz, n = q.shape[0], q.shape[1]
    n_ctx = k_c.shape[1]
    nb = n // BLOCK
    scale = HEAD_DIM ** -0.5
    qb = q.reshape(bsz, nb, BLOCK, N_KV_HEADS, Q_PER_KV, HEAD_DIM).transpose(1, 0, 3, 4, 2, 5)
    pad = ((0, 0), (BLOCK, BLOCK), (0, 0), (0, 0))
    k_pad = jnp.pad(k, pad).transpose(0, 2, 1, 3)
    v_pad = jnp.pad(v, pad).transpose(0, 2, 1, 3)
    kc = k_c.transpose(0, 2, 1, 3)
    vc = v_c.transpose(0, 2, 1, 3)
    sink_l = sink.astype(jnp.float32).reshape(N_KV_HEADS, Q_PER_KV)[None, :, :, None, None]
    offs_q = jnp.arange(BLOCK)
    offs_k = jnp.arange(3 * BLOCK) - BLOCK

    def one_block(args):
        i, q_i = args
        start = i * BLOCK
        k_i = lax.dynamic_slice_in_dim(k_pad, start, 3 * BLOCK, axis=2)
        v_i = lax.dynamic_slice_in_dim(v_pad, start, 3 * BLOCK, axis=2)
        q_pos = start + offs_q
        k_pos = start + offs_k
        valid = ((jnp.abs(q_pos[:, None] - k_pos[None, :]) <= WINDOW)
                 & (k_pos >= 0)[None, :] & (k_pos < n)[None, :])
        s_loc = jnp.einsum('bhgqd,bhkd->bhgqk', q_i, k_i).astype(jnp.float32) * scale
        s_loc = jnp.where(valid, s_loc, NEG_INF)
        s_ctx = jnp.einsum('bhgqd,bhkd->bhgqk', q_i, kc).astype(jnp.float32) * scale
        sink_b = jnp.broadcast_to(sink_l, s_ctx.shape[:-1] + (1,))
        p = jax.nn.softmax(jnp.concatenate([s_loc, s_ctx, sink_b], axis=-1), axis=-1)
        p_loc = p[..., :3 * BLOCK].astype(v.dtype)
        p_ctx = p[..., 3 * BLOCK:3 * BLOCK + n_ctx].astype(v.dtype)
        return (jnp.einsum('bhgqk,bhkd->bhgqd', p_loc, v_i)
                + jnp.einsum('bhgqk,bhkd->bhgqd', p_ctx, vc))

    out = lax.map(one_block, (jnp.arange(nb), qb))
    return out.transpose(1, 0, 4, 2, 3, 5).reshape(bsz, n, N_Q_HEADS * HEAD_DIM)


def context_attention(q_c, k_c, v_c, sink):
    bsz, n_ctx = q_c.shape[0], q_c.shape[1]
    qg = q_c.reshape(bsz, n_ctx, N_KV_HEADS, Q_PER_KV, HEAD_DIM)
    s = jnp.einsum('bqhgd,bkhd->bhgqk', qg, k_c).astype(jnp.float32) * (HEAD_DIM ** -0.5)
    sink_b = jnp.broadcast_to(sink.astype(jnp.float32).reshape(N_KV_HEADS, Q_PER_KV)[None, :, :, None, None],
                              s.shape[:-1] + (1,))
    p = jax.nn.softmax(jnp.concatenate([s, sink_b], axis=-1), axis=-1)[..., :n_ctx].astype(v_c.dtype)
    o = jnp.einsum('bhgqk,bkhd->bqhgd', p, v_c)
    return o.reshape(bsz, n_ctx, N_Q_HEADS * HEAD_DIM)


def centred_depthwise_conv(x, w, b):
    n = x.shape[1]
    left = CONV_WIDTH // 2
    right = CONV_WIDTH - 1 - left
    xp = jnp.pad(x, ((0, 0), (left, right), (0, 0)))
    y = b + w[0] * xp[:, 0:n]
    for j in range(1, CONV_WIDTH):
        y = y + w[j] * xp[:, j:j + n]
    return y


def block_diag_linear(x, w, b):
    xb = x.reshape(x.shape[:-1] + (LRU_BLOCKS, LRU_BLOCK_DIM))
    return jnp.einsum('blnd,nde->blne', xb, w).reshape(x.shape) + b


def lru_coeffs(xc, w_r, b_r, w_i, b_i, lam):
    r = jax.nn.sigmoid(block_diag_linear(xc, w_r, b_r).astype(jnp.float32))
    i = jax.nn.sigmoid(block_diag_linear(xc, w_i, b_i).astype(jnp.float32))
    log_a = -LRU_C * r * jax.nn.softplus(-lam.astype(jnp.float32))
    a = jnp.exp(log_a)
    mult = jnp.sqrt(-jnp.expm1(2.0 * log_a))
    return a, mult * i * xc.astype(jnp.float32)


def _combine(left, right):
    a_l, b_l = left
    a_r, b_r = right
    return a_l * a_r, a_r * b_l + b_r


def linear_scan(a, b, h0, reverse):
    if reverse:
        a, b = jnp.flip(a, axis=1), jnp.flip(b, axis=1)
    a_cum, h = lax.associative_scan(_combine, (a, b), axis=1)
    h = h + a_cum * h0[:, None, :]
    if reverse:
        h = jnp.flip(h, axis=1)
    return h


def bidirectional_rglru(xr_lat, xr_ctx, conv_w, conv_b, w_r, b_r, w_i, b_i, lam):
    xc_lat = centred_depthwise_conv(xr_lat, conv_w, conv_b)
    xc_ctx = centred_depthwise_conv(xr_ctx, conv_w, conv_b)
    h_lat_sum, h_ctx_sum = None, None
    for d in range(2):
        reverse = d == 1
        a_c, b_c = lru_coeffs(xc_ctx, w_r[d], b_r[d], w_i[d], b_i[d], lam[d])
        h_c = linear_scan(a_c, b_c, jnp.zeros_like(a_c[:, 0]), reverse)
        h0 = h_c[:, 0] if reverse else h_c[:, -1]
        a_l, b_l = lru_coeffs(xc_lat, w_r[d], b_r[d], w_i[d], b_i[d], lam[d])
        h_l = linear_scan(a_l, b_l, h0, reverse)
        h_lat_sum = h_l if h_lat_sum is None else h_lat_sum + h_l
        h_ctx_sum = h_c if h_ctx_sum is None else h_ctx_sum + h_c
    return h_lat_sum.astype(xr_lat.dtype), h_ctx_sum.astype(xr_ctx.dtype)


def expert_choice_moe(h, w_router, w_gate, w_up, w_down):
    bsz, n, dm = h.shape
    cap = CAPACITY_FACTOR * n // N_EXPERTS
    affinity = jax.nn.softmax((h @ w_router).astype(jnp.float32), axis=-1)
    gate_vals, idx = lax.top_k(affinity.transpose(0, 2, 1), cap)
    idx_flat = idx.reshape(bsz, N_EXPERTS * cap)
    xs = jnp.take_along_axis(h, idx_flat[..., None], axis=1).reshape(bsz, N_EXPERTS, cap, dm)
    hid = (jax.nn.silu(jnp.einsum('becd,edf->becf', xs, w_gate))
           * jnp.einsum('becd,edf->becf', xs, w_up))
    ys = jnp.einsum('becf,efd->becd', hid, w_down) * gate_vals[..., None].astype(h.dtype)
    out = jnp.zeros_like(h).at[jnp.arange(bsz)[:, None], idx_flat].add(
        ys.reshape(bsz, N_EXPERTS * cap, dm))
    return out


def setup_inputs(seed: int = 0) -> dict:
    key = jax.random.key(seed)
    ks = jax.random.split(key, 24)
    f32 = jnp.float32
    nrm = lambda k, shape, s: jax.random.normal(k, shape, f32) * s
    u = jax.random.uniform(ks[16], (DEPTH, 2, LRU_WIDTH), f32, 0.9, 0.999)
    a0 = u ** (1.0 / LRU_C)
    lru_lambda = jnp.log(a0) - jnp.log1p(-a0)
    return {
        "x": nrm(ks[0], (BATCH, SEQ, D_MODEL), 1.0),
        "c": nrm(ks[1], (BATCH, D_MODEL), 1.0),
        "ctx": nrm(ks[2], (BATCH, CTX_LEN, D_MODEL), 1.0),
        "c_ctx": nrm(ks[3], (D_MODEL,), 1.0),
        "w_ada": nrm(ks[4], (DEPTH, D_MODEL, 6 * D_MODEL), 0.5 * D_MODEL ** -0.5),
        "b_ada": nrm(ks[5], (DEPTH, 6 * D_MODEL), 0.02),
        "norm1_g": 1.0 + nrm(ks[6], (DEPTH, D_MODEL), 0.05),
        "norm2_g": 1.0 + nrm(ks[7], (DEPTH, D_MODEL), 0.05),
        "w_in": nrm(ks[8], (DEPTH, D_MODEL, IN_WIDTH), D_MODEL ** -0.5),
        "q_norm_g": 1.0 + nrm(ks[9], (DEPTH, HEAD_DIM), 0.05),
        "k_norm_g": 1.0 + nrm(ks[10], (DEPTH, HEAD_DIM), 0.05),
        "attn_sink": nrm(ks[11], (DEPTH, N_Q_HEADS), 0.5),
        "conv_w": nrm(ks[12], (DEPTH, CONV_WIDTH, LRU_WIDTH), CONV_WIDTH ** -0.5),
        "conv_b": nrm(ks[13], (DEPTH, LRU_WIDTH), 0.02),
        "lru_w_r": nrm(ks[14], (DEPTH, 2, LRU_BLOCKS, LRU_BLOCK_DIM, LRU_BLOCK_DIM), LRU_BLOCK_DIM ** -0.5),
        "lru_b_r": nrm(ks[15], (DEPTH, 2, LRU_WIDTH), 0.02),
        "lru_w_i": nrm(ks[17], (DEPTH, 2, LRU_BLOCKS, LRU_BLOCK_DIM, LRU_BLOCK_DIM), LRU_BLOCK_DIM ** -0.5),
        "lru_b_i": nrm(ks[18], (DEPTH, 2, LRU_WIDTH), 0.02),
        "lru_lambda": lru_lambda,
        "w_out": nrm(ks[19], (DEPTH, MIX_WIDTH, D_MODEL), MIX_WIDTH ** -0.5),
        "w_router": nrm(ks[20], (DEPTH, D_MODEL, N_EXPERTS), D_MODEL ** -0.5),
        "w_gate": nrm(ks[21], (DEPTH, N_EXPERTS, D_MODEL, D_EXPERT), D_MODEL ** -0.5),
        "w_up": nrm(ks[22], (DEPTH, N_EXPERTS, D_MODEL, D_EXPERT), D_MODEL ** -0.5),
        "w_down": nrm(ks[23], (DEPTH, N_EXPERTS, D_EXPERT, D_MODEL), D_EXPERT ** -0.5),
    }


def reference(x, c, ctx, c_ctx, w_ada, b_ada, norm1_g, norm2_g, w_in, q_norm_g, k_norm_g,
              attn_sink, conv_w, conv_b, lru_w_r, lru_b_r, lru_w_i, lru_b_i, lru_lambda,
              w_out, w_router, w_gate, w_up, w_down):
    bsz, n, _ = x.shape
    n_ctx = ctx.shape[1]
    rows = n // GRID_W
    cos, sin = rope_2d_tables(rows)
    for layer in range(DEPTH):
        sh1, sc1, g1, sh2, sc2, g2 = ada_params(c, w_ada[layer], b_ada[layer])
        csh1, csc1, cg1, csh2, csc2, cg2 = ada_params(c_ctx, w_ada[layer], b_ada[layer])

        h = modulate(rmsnorm(x, norm1_g[layer]), sh1, sc1)
        hc = modulate(rmsnorm(ctx, norm1_g[layer]), csh1, csc1)
        q, k, v, xr, gr = jnp.split(h @ w_in[layer], IN_SPLITS, axis=-1)
        q_c, k_c, v_c, xr_c, gr_c = jnp.split(hc @ w_in[layer], IN_SPLITS, axis=-1)

        q = apply_rope_2d(rmsnorm(q.reshape(bsz, n, N_Q_HEADS, HEAD_DIM), q_norm_g[layer]), cos, sin)
        k = apply_rope_2d(rmsnorm(k.reshape(bsz, n, N_KV_HEADS, HEAD_DIM), k_norm_g[layer]), cos, sin)
        v = v.reshape(bsz, n, N_KV_HEADS, HEAD_DIM)
        q_c = rmsnorm(q_c.reshape(bsz, n_ctx, N_Q_HEADS, HEAD_DIM), q_norm_g[layer])
        k_c = rmsnorm(k_c.reshape(bsz, n_ctx, N_KV_HEADS, HEAD_DIM), k_norm_g[layer])
        v_c = v_c.reshape(bsz, n_ctx, N_KV_HEADS, HEAD_DIM)

        attn_lat = windowed_attention_with_context(q, k, v, k_c, v_c, attn_sink[layer])
        rnn_lat, rnn_ctx = bidirectional_rglru(xr, xr_c, conv_w[layer], conv_b[layer],
                                               lru_w_r[layer], lru_b_r[layer], lru_w_i[layer],
                                               lru_b_i[layer], lru_lambda[layer])
        y = jnp.concatenate([attn_lat, rnn_lat * jax.nn.gelu(gr)], axis=-1) @ w_out[layer]
        x = x + g1 * y

        h2 = modulate(rmsnorm(x, norm2_g[layer]), sh2, sc2)
        x = x + g2 * expert_choice_moe(h2, w_router[layer], w_gate[layer], w_up[layer], w_down[layer])

        if layer + 1 < DEPTH:
            attn_ctx = context_attention(q_c, k_c, v_c, attn_sink[layer])
            yc = jnp.concatenate([attn_ctx, rnn_ctx * jax.nn.gelu(gr_c)], axis=-1) @ w_out[layer]
            ctx = ctx + cg1 * yc
            hc2 = modulate(rmsnorm(ctx, norm2_g[layer]), csh2, csc2)
            ctx = ctx + cg2 * expert_choice_moe(hc2, w_router[layer], w_gate[layer], w_up[layer], w_down[layer])
    return x
```

```python
import functools

import jax
import jax.numpy as jnp
from jax import lax
from jax.experimental import pallas as pl
from jax.experimental.pallas import tpu as pltpu

F32 = jnp.float32
BF16 = jnp.bfloat16

LANES = 128
SUBLANES = 8
VMEM_LIMIT_BYTES = 56 * 1024 * 1024

D_MODEL = 1024
HEAD_DIM = 64
N_Q_HEADS = 8
N_KV_HEADS = 2
Q_PER_KV = N_Q_HEADS // N_KV_HEADS
ATTN_WIDTH = N_Q_HEADS * HEAD_DIM
KV_WIDTH = N_KV_HEADS * HEAD_DIM
LRU_WIDTH = D_MODEL - ATTN_WIDTH
LRU_BLOCK_DIM = 64
IN_WIDTH = ATTN_WIDTH + 2 * KV_WIDTH + 2 * LRU_WIDTH
WINDOW = 128
BLOCK = 128
GRID_W = 64
ROPE_BASE = 10000.0
LRU_C = 8.0
N_EXPERTS = 16
CAPACITY_FACTOR = 2
EPS = 1e-6
NEG_INF = -1e30
N_MOD = 6
FEATURE_CHUNKS = D_MODEL // LANES


def _dot(a, b):
    return jnp.dot(a, b, preferred_element_type=F32)


def _dot_nt(a, b):
    return lax.dot_general(a, b, (((1,), (1,)), ((), ())), preferred_element_type=F32)


def _dot_tn(a, b):
    return lax.dot_general(a, b, (((0,), (0,)), ((), ())), preferred_element_type=F32)


def _split_bf16(x, parts):
    out = []
    r = x
    for _ in range(parts):
        p = r.astype(BF16)
        out.append(p)
        r = r - p.astype(F32)
    return out


def _rmsnorm_rows(x, g):
    ms = jnp.mean(x * x, axis=-1, keepdims=True)
    return x * lax.rsqrt(ms + EPS) * g


def _ada_kernel(c_ref, w_ref, b_ref, o_ref):
    s = jax.nn.silu(c_ref[...])
    s_hi, s_lo = _split_bf16(s, 2)
    w_hi, w_lo = _split_bf16(w_ref[...], 2)
    o_ref[...] = _dot(s_hi, w_hi) + _dot(s_hi, w_lo) + _dot(s_lo, w_hi) + b_ref[...]


def _ada(cc, w_ada, b_ada):
    rows = cc.shape[0]
    width = w_ada.shape[1]
    return pl.pallas_call(
        _ada_kernel,
        out_shape=jax.ShapeDtypeStruct((rows, width), F32),
        grid=(width // D_MODEL,),
        in_specs=[
            pl.BlockSpec((rows, D_MODEL), lambda j: (0, 0)),
            pl.BlockSpec((D_MODEL, D_MODEL), lambda j: (0, j)),
            pl.BlockSpec((1, D_MODEL), lambda j: (0, j)),
        ],
        out_specs=pl.BlockSpec((rows, D_MODEL), lambda j: (0, j)),
        compiler_params=pltpu.CompilerParams(
            dimension_semantics=("arbitrary",), vmem_limit_bytes=VMEM_LIMIT_BYTES),
        name="ada",
    )(cc, w_ada, b_ada)


def _head_norm(p, g):
    lane = lax.broadcasted_iota(jnp.int32, p.shape, 1)
    low = lane < HEAD_DIM
    sq = p * p
    s_low = jnp.sum(jnp.where(low, sq, 0.0), axis=-1, keepdims=True)
    s_high = jnp.sum(jnp.where(low, 0.0, sq), axis=-1, keepdims=True)
    ms = jnp.where(low, s_low, s_high) * (1.0 / HEAD_DIM)
    return p * lax.rsqrt(ms + EPS) * g


def _rope(p, cos, sin_signed):
    lane = lax.broadcasted_iota(jnp.int32, p.shape, 1)
    partner = jnp.where((lane & 16) == 0,
                        pltpu.roll(p, LANES - 16, 1), pltpu.roll(p, 16, 1))
    return p * cos + partner * sin_signed


def _inproj_latent_kernel(x_ref, mod_ref, g1_ref, w_ref, qg_ref, kg_ref, cos_ref, sin_ref,
                          q_ref, k_ref, v_ref, xr_ref, gg_ref):
    x = x_ref[...]
    h = _rmsnorm_rows(x, g1_ref[...]) * (1.0 + mod_ref[0, 1:2, :]) + mod_ref[0, 0:1, :]
    y = _dot(h.astype(BF16), w_ref[...])
    cos = cos_ref[...]
    sin = sin_ref[...]
    pieces = []
    for j in range(ATTN_WIDTH // LANES):
        p = _head_norm(y[:, j * LANES:(j + 1) * LANES], qg_ref[...])
        pieces.append(_rope(p, cos, sin))
    lane = lax.broadcasted_iota(jnp.int32, pieces[0].shape, 1)
    low = lane < HEAD_DIM
    n_blocks = x.shape[0] // BLOCK
    for g in range(Q_PER_KV):
        src0 = pieces[g // 2]
        src1 = pieces[2 + g // 2]
        if g % 2 == 0:
            out = jnp.where(low, src0, pltpu.roll(src1, HEAD_DIM, 1))
        else:
            out = jnp.where(low, pltpu.roll(src0, HEAD_DIM, 1), src1)
        out = out.astype(BF16)
        for j in range(n_blocks):
            q_ref[0, j, g * BLOCK:(g + 1) * BLOCK, :] = out[j * BLOCK:(j + 1) * BLOCK, :]
    k = _head_norm(y[:, ATTN_WIDTH:ATTN_WIDTH + KV_WIDTH], kg_ref[...])
    k_ref[...] = _rope(k, cos, sin).astype(BF16)
    o = ATTN_WIDTH + KV_WIDTH
    v_ref[...] = y[:, o:o + KV_WIDTH].astype(BF16)
    o += KV_WIDTH
    xr_ref[...] = y[:, o:o + LRU_WIDTH]
    o += LRU_WIDTH
    gg_ref[...] = jax.nn.gelu(y[:, o:o + LRU_WIDTH])


def _inproj_ctx_kernel(x_ref, mod_ref, g1_ref, w_ref, kg_ref, k_ref, v_ref, xr_ref):
    x = x_ref[...]
    h = _rmsnorm_rows(x, g1_ref[...]) * (1.0 + mod_ref[0, 1:2, :]) + mod_ref[0, 0:1, :]
    y = _dot(h.astype(BF16), w_ref[...])
    k_ref[...] = _head_norm(y[:, 0:KV_WIDTH], kg_ref[...]).astype(BF16)
    v_ref[...] = y[:, KV_WIDTH:2 * KV_WIDTH].astype(BF16)
    xr_ref[...] = y[:, 2 * KV_WIDTH:2 * KV_WIDTH + LRU_WIDTH]


def _row_tile(n):
    return min(512, n)


def _inproj_latent(x2, mod3, g1, w_in, qg, kg, cos, sin, bsz, n):
    tm = _row_tile(n)
    tps = n // tm
    rows = bsz * n
    const = lambda i: (0, 0)
    return pl.pallas_call(
        _inproj_latent_kernel,
        out_shape=(
            jax.ShapeDtypeStruct((bsz, n // BLOCK, Q_PER_KV * BLOCK, LANES), BF16),
            jax.ShapeDtypeStruct((rows, KV_WIDTH), BF16),
            jax.ShapeDtypeStruct((rows, KV_WIDTH), BF16),
            jax.ShapeDtypeStruct((rows, LRU_WIDTH), F32),
            jax.ShapeDtypeStruct((rows, LRU_WIDTH), F32),
        ),
        grid=(rows // tm,),
        in_specs=[
            pl.BlockSpec((tm, D_MODEL), lambda i: (i, 0)),
            pl.BlockSpec((1, N_MOD, D_MODEL), lambda i: (i // tps, 0, 0)),
            pl.BlockSpec((1, D_MODEL), const),
            pl.BlockSpec((D_MODEL, IN_WIDTH), const),
            pl.BlockSpec((1, LANES), const),
            pl.BlockSpec((1, LANES), const),
            pl.BlockSpec((tm, LANES), lambda i: (i % tps, 0)),
            pl.BlockSpec((tm, LANES), lambda i: (i % tps, 0)),
        ],
        out_specs=(
            pl.BlockSpec((1, tm // BLOCK, Q_PER_KV * BLOCK, LANES),
                         lambda i: (i // tps, i % tps, 0, 0)),
            pl.BlockSpec((tm, KV_WIDTH), lambda i: (i, 0)),
            pl.BlockSpec((tm, KV_WIDTH), lambda i: (i, 0)),
            pl.BlockSpec((tm, LRU_WIDTH), lambda i: (i, 0)),
            pl.BlockSpec((tm, LRU_WIDTH), lambda i: (i, 0)),
        ),
        compiler_params=pltpu.CompilerParams(
            dimension_semantics=("arbitrary",), vmem_limit_bytes=VMEM_LIMIT_BYTES),
        name="inproj_latent",
    )(x2, mod3, g1, w_in, qg, kg, cos, sin)


def _inproj_ctx(c2, mod3, g1, w_ctx, kg, bsz, n_ctx):
    tm = _row_tile(n_ctx)
    rows = bsz * n_ctx
    ctx_row = mod3.shape[0] - 1
    width = w_ctx.shape[1]
    const = lambda i: (0, 0)
    return pl.pallas_call(
        _inproj_ctx_kernel,
        out_shape=(
            jax.ShapeDtypeStruct((rows, KV_WIDTH), BF16),
            jax.ShapeDtypeStruct((rows, KV_WIDTH), BF16),
            jax.ShapeDtypeStruct((rows, LRU_WIDTH), F32),
        ),
        grid=(rows // tm,),
        in_specs=[
            pl.BlockSpec((tm, D_MODEL), lambda i: (i, 0)),
            pl.BlockSpec((1, N_MOD, D_MODEL), lambda i: (ctx_row, 0, 0)),
            pl.BlockSpec((1, D_MODEL), const),
            pl.BlockSpec((D_MODEL, width), const),
            pl.BlockSpec((1, LANES), const),
        ],
        out_specs=(
            pl.BlockSpec((tm, KV_WIDTH), lambda i: (i, 0)),
            pl.BlockSpec((tm, KV_WIDTH), lambda i: (i, 0)),
            pl.BlockSpec((tm, LRU_WIDTH), lambda i: (i, 0)),
        ),
        compiler_params=pltpu.CompilerParams(
            dimension_semantics=("arbitrary",), vmem_limit_bytes=VMEM_LIMIT_BYTES),
        name="inproj_ctx",
    )(c2, mod3, g1, w_ctx, kg)


def _attn_kernel(sink_ref, q_ref, k_ref, v_ref, kc_ref, vc_ref, o_ref, *, n):
    i = pl.program_id(1)
    span = 3 * BLOCK
    start = pl.multiple_of(jnp.clip((i - 1) * BLOCK, 0, n - span), BLOCK)
    q = q_ref[0, 0]
    kw = k_ref[0, pl.ds(start, span), :]
    vw = v_ref[0, pl.ds(start, span), :]
    kc = kc_ref[0]
    vc = vc_ref[0]
    rows = q.shape[0]
    row = lax.broadcasted_iota(jnp.int32, (rows, span), 0)
    col = lax.broadcasted_iota(jnp.int32, (rows, span), 1)
    q_pos = i * BLOCK + (row & (BLOCK - 1))
    valid = jnp.abs(q_pos - (start + col)) <= WINDOW
    group = lax.broadcasted_iota(jnp.int32, (rows, 1), 0) // BLOCK
    scale = HEAD_DIM ** -0.5
    acc = jnp.zeros((rows, LANES), F32)
    for h in range(N_KV_HEADS):
        def head_only(t):
            lane = lax.broadcasted_iota(jnp.int32, t.shape, 1)
            keep = (lane < HEAD_DIM) if h == 0 else (lane >= HEAD_DIM)
            return jnp.where(keep, t, jnp.zeros_like(t))
        s_loc = jnp.where(valid, _dot_nt(q, head_only(kw)) * scale, NEG_INF)
        s_ctx = _dot_nt(q, head_only(kc)) * scale
        sink = jnp.zeros((rows, 1), F32)
        for g in range(Q_PER_KV):
            sink = jnp.where(group == g, sink_ref[h * Q_PER_KV + g], sink)
        m = jnp.maximum(jnp.maximum(jnp.max(s_loc, axis=-1, keepdims=True),
                                    jnp.max(s_ctx, axis=-1, keepdims=True)), sink)
        p_loc = jnp.exp(s_loc - m)
        p_ctx = jnp.exp(s_ctx - m)
        denom = (jnp.sum(p_loc, axis=-1, keepdims=True)
                 + jnp.sum(p_ctx, axis=-1, keepdims=True) + jnp.exp(sink - m))
        o = _dot(p_loc.astype(BF16), head_only(vw)) + _dot(p_ctx.astype(BF16), head_only(vc))
        acc = acc + o / denom
    o_ref[0, 0] = acc.astype(BF16)


def _attn(sink, q_s, k3, v3, kc3, vc3):
    bsz, nb = q_s.shape[0], q_s.shape[1]
    n = k3.shape[1]
    n_ctx = kc3.shape[1]
    return pl.pallas_call(
        functools.partial(_attn_kernel, n=n),
        out_shape=jax.ShapeDtypeStruct(q_s.shape, BF16),
        grid=(bsz, nb),
        in_specs=[
            pl.BlockSpec(memory_space=pltpu.SMEM),
            pl.BlockSpec((1, 1, Q_PER_KV * BLOCK, LANES), lambda b, i: (b, i, 0, 0)),
            pl.BlockSpec((1, n, KV_WIDTH), lambda b, i: (b, 0, 0)),
            pl.BlockSpec((1, n, KV_WIDTH), lambda b, i: (b, 0, 0)),
            pl.BlockSpec((1, n_ctx, KV_WIDTH), lambda b, i: (b, 0, 0)),
            pl.BlockSpec((1, n_ctx, KV_WIDTH), lambda b, i: (b, 0, 0)),
        ],
        out_specs=pl.BlockSpec((1, 1, Q_PER_KV * BLOCK, LANES), lambda b, i: (b, i, 0, 0)),
        compiler_params=pltpu.CompilerParams(
            dimension_semantics=("arbitrary", "arbitrary"), vmem_limit_bytes=VMEM_LIMIT_BYTES),
        name="attn",
    )(sink, q_s, k3, v3, kc3, vc3)


def _softplus(z):
    return jnp.maximum(z, 0.0) + jnp.log1p(jnp.exp(-jnp.abs(z)))


def _lru_kernel(xr_ref, xrc_ref, gg_ref, cw_ref, cb_ref, wg_ref, bg_ref, lam_ref, o_ref,
                xc_s, xcc_s, hf_s, *, n, n_ctx, tile):
    cw = cw_ref[...]
    cb = cb_ref[...]
    n_groups = tile // SUBLANES

    def conv_tile(src_ref, dst_ref, t0, length):
        x = src_ref[0, pl.ds(t0, tile), :]
        lo = pl.multiple_of(jnp.maximum(t0 - SUBLANES, 0), SUBLANES)
        hi = pl.multiple_of(jnp.minimum(t0 + tile, length - SUBLANES), SUBLANES)
        prev = jnp.where(t0 > 0, src_ref[0, pl.ds(lo, SUBLANES), :], 0.0)
        nxt = jnp.where(t0 + tile < length, src_ref[0, pl.ds(hi, SUBLANES), :], 0.0)
        row = lax.broadcasted_iota(jnp.int32, x.shape, 0)
        xm1 = jnp.where(row == 0, prev[7:8], pltpu.roll(x, 1, 0))
        xm2 = jnp.where(row == 0, prev[6:7],
                        jnp.where(row == 1, prev[7:8], pltpu.roll(x, 2, 0)))
        xp1 = jnp.where(row == tile - 1, nxt[0:1], pltpu.roll(x, tile - 1, 0))
        y = cb + cw[0:1] * xm2
        y = y + cw[1:2] * xm1
        y = y + cw[2:3] * x
        y = y + cw[3:4] * xp1
        dst_ref[pl.ds(t0, tile), :] = y

    def coeffs(xc, d):
        w = wg_ref[0, :, 2 * LANES * d:2 * LANES * (d + 1)]
        g = _dot(xc.astype(BF16), w) + bg_ref[0, :, 2 * LANES * d:2 * LANES * (d + 1)]
        r = jax.nn.sigmoid(g[:, :LANES])
        i = jax.nn.sigmoid(g[:, LANES:])
        log_a = -LRU_C * r * _softplus(-lam_ref[d:d + 1, :])
        a = jnp.exp(log_a)
        th = jnp.tanh(log_a)
        mult = jnp.sqrt(-2.0 * th / (1.0 - th))
        return a, mult * i * xc

    def scan_tile(a, b, h, reverse):
        a3 = a.reshape(n_groups, SUBLANES, LANES)
        b3 = b.reshape(n_groups, SUBLANES, LANES)
        sub = lax.broadcasted_iota(jnp.int32, a3.shape, 1)
        for s in (1, 2, 4):
            shift = SUBLANES - s if reverse else s
            ok = (sub < SUBLANES - s) if reverse else (sub >= s)
            a_prev = jnp.where(ok, pltpu.roll(a3, shift, 1), 1.0)
            b_prev = jnp.where(ok, pltpu.roll(b3, shift, 1), 0.0)
            b3 = a3 * b_prev + b3
            a3 = a3 * a_prev
        last = 0 if reverse else SUBLANES - 1
        states = [None] * n_groups
        for g in (range(n_groups - 1, -1, -1) if reverse else range(n_groups)):
            hg = b3[g] + a3[g] * h
            states[g] = hg
            h = hg[last:last + 1, :]
        return states, h

    for j in range(n_ctx // tile):
        conv_tile(xrc_ref, xcc_s, j * tile, n_ctx)

    def conv_body(j, carry):
        conv_tile(xr_ref, xc_s, pl.multiple_of(j * tile, tile), n)
        return carry
    lax.fori_loop(0, n // tile, conv_body, 0)

    h = jnp.zeros((1, LANES), F32)
    for j in range(n_ctx // tile):
        a, b = coeffs(xcc_s[j * tile:(j + 1) * tile, :], 0)
        _, h = scan_tile(a, b, h, False)

    def fwd_body(j, h):
        t0 = pl.multiple_of(j * tile, tile)
        a, b = coeffs(xc_s[pl.ds(t0, tile), :], 0)
        states, h = scan_tile(a, b, h, False)
        for g in range(n_groups):
            hf_s[pl.ds(t0 + g * SUBLANES, SUBLANES), :] = states[g]
        return h
    lax.fori_loop(0, n // tile, fwd_body, h)

    h = jnp.zeros((1, LANES), F32)
    for j in range(n_ctx // tile - 1, -1, -1):
        a, b = coeffs(xcc_s[j * tile:(j + 1) * tile, :], 1)
        _, h = scan_tile(a, b, h, True)

    def rev_body(jj, h):
        t0 = pl.multiple_of((n // tile - 1 - jj) * tile, tile)
        a, b = coeffs(xc_s[pl.ds(t0, tile), :], 1)
        states, h = scan_tile(a, b, h, True)
        for g2 in range(n_groups // 2):
            rows = pl.ds(t0 + g2 * 2 * SUBLANES, 2 * SUBLANES)
            hb = jnp.concatenate([states[2 * g2], states[2 * g2 + 1]], axis=0)
            o_ref[0, rows, :] = ((hf_s[rows, :] + hb) * gg_ref[0, rows, :]).astype(BF16)
        return h
    lax.fori_loop(0, n // tile, rev_body, h)


def _lru(xr3, xrc3, gg3, conv_w, conv_b, wg, bg, lam):
    bsz, n, _ = xr3.shape
    n_ctx = xrc3.shape[1]
    tile = min(256, n, n_ctx)
    n_p = LRU_WIDTH // LANES
    return pl.pallas_call(
        functools.partial(_lru_kernel, n=n, n_ctx=n_ctx, tile=tile),
        out_shape=jax.ShapeDtypeStruct((bsz, n, LRU_WIDTH), BF16),
        grid=(bsz, n_p),
        in_specs=[
            pl.BlockSpec((1, n, LANES), lambda b, p: (b, 0, p)),
            pl.BlockSpec((1, n_ctx, LANES), lambda b, p: (b, 0, p)),
            pl.BlockSpec((1, n, LANES), lambda b, p: (b, 0, p)),
            pl.BlockSpec((conv_w.shape[0], LANES), lambda b, p: (0, p)),
            pl.BlockSpec((1, LANES), lambda b, p: (0, p)),
            pl.BlockSpec((1, LANES, 4 * LANES), lambda b, p: (p, 0, 0)),
            pl.BlockSpec((1, 1, 4 * LANES), lambda b, p: (p, 0, 0)),
            pl.BlockSpec((2, LANES), lambda b, p: (0, p)),
        ],
        out_specs=pl.BlockSpec((1, n, LANES), lambda b, p: (b, 0, p)),
        scratch_shapes=[
            pltpu.VMEM((n, LANES), F32),
            pltpu.VMEM((n_ctx, LANES), F32),
            pltpu.VMEM((n, LANES), F32),
        ],
        compiler_params=pltpu.CompilerParams(
            dimension_semantics=("arbitrary", "arbitrary"), vmem_limit_bytes=VMEM_LIMIT_BYTES),
        name="lru",
    )(xr3, xrc3, gg3, conv_w, conv_b, wg, bg, lam)


def _outproj_kernel(o_ref, rg_ref, x_ref, mod_ref, watt_ref, wrnn_ref, g2_ref, wr_ref,
                    x1_ref, h2t_ref, lg_ref):
    tm = x_ref.shape[0]
    n_blocks = tm // BLOCK
    att = jnp.concatenate(
        [jnp.concatenate([o_ref[0, j, g * BLOCK:(g + 1) * BLOCK, :] for g in range(Q_PER_KV)],
                         axis=1) for j in range(n_blocks)], axis=0)
    y = _dot(att, watt_ref[...]) + _dot(rg_ref[...], wrnn_ref[...])
    x1 = x_ref[...] + mod_ref[0, 2:3, :] * y
    x1_ref[...] = x1
    h2 = _rmsnorm_rows(x1, g2_ref[...]) * (1.0 + mod_ref[0, 4:5, :]) + mod_ref[0, 3:4, :]
    for s in range(FEATURE_CHUNKS):
        h2t_ref[pl.ds(s, tm, stride=FEATURE_CHUNKS), :] = h2[:, s * LANES:(s + 1) * LANES]
    lt = _dot_nt(wr_ref[...], h2.astype(BF16))
    for j in range(n_blocks):
        lg_ref[0, j] = lt[:, j * LANES:(j + 1) * LANES]


def _outproj(o_s, rg2, x2, mod3, w_att, w_rnn, g2, wr_t, bsz, n):
    tm = _row_tile(n)
    tps = n // tm
    rows = bsz * n
    const = lambda i: (0, 0)
    return pl.pallas_call(
        _outproj_kernel,
        out_shape=(
            jax.ShapeDtypeStruct((rows, D_MODEL), F32),
            jax.ShapeDtypeStruct((rows * FEATURE_CHUNKS, LANES), F32),
            jax.ShapeDtypeStruct((bsz, n // LANES, N_EXPERTS, LANES), F32),
        ),
        grid=(rows // tm,),
        in_specs=[
            pl.BlockSpec((1, tm // BLOCK, Q_PER_KV * BLOCK, LANES),
                         lambda i: (i // tps, i % tps, 0, 0)),
            pl.BlockSpec((tm, LRU_WIDTH), lambda i: (i, 0)),
            pl.BlockSpec((tm, D_MODEL), lambda i: (i, 0)),
            pl.BlockSpec((1, N_MOD, D_MODEL), lambda i: (i // tps, 0, 0)),
            pl.BlockSpec((ATTN_WIDTH, D_MODEL), const),
            pl.BlockSpec((LRU_WIDTH, D_MODEL), const),
            pl.BlockSpec((1, D_MODEL), const),
            pl.BlockSpec((N_EXPERTS, D_MODEL), const),
        ],
        out_specs=(
            pl.BlockSpec((tm, D_MODEL), lambda i: (i, 0)),
            pl.BlockSpec((tm * FEATURE_CHUNKS, LANES), lambda i: (i, 0)),
            pl.BlockSpec((1, tm // LANES, N_EXPERTS, LANES), lambda i: (i // tps, i % tps, 0, 0)),
        ),
        compiler_params=pltpu.CompilerParams(
            dimension_semantics=("arbitrary",), vmem_limit_bytes=VMEM_LIMIT_BYTES),
        name="outproj",
    )(o_s, rg2, x2, mod3, w_att, w_rnn, g2, wr_t)


def _topk_kernel(lg_ref, idx_ref, gate_ref, aff_s, cum_s, *, cap):
    lg = lg_ref[0]
    n_chunks = lg.shape[0]
    m = jnp.max(lg, axis=1, keepdims=True)
    ex = jnp.exp(lg - m)
    aff = ex / jnp.sum(ex, axis=1, keepdims=True)
    aff_s[...] = aff
    bits = pltpu.bitcast(aff, jnp.int32)

    def count(mask):
        c = jnp.sum(jnp.where(mask, 1.0, 0.0), axis=0, keepdims=True)
        return jnp.sum(c, axis=2, keepdims=True)

    def bit_body(it, thr):
        cand = thr | jnp.left_shift(jnp.int32(1), 30 - it)
        return jnp.where(count(bits >= cand) >= cap, cand, thr)
    thr = lax.fori_loop(0, 31, bit_body, jnp.zeros((1, N_EXPERTS, 1), jnp.int32))

    tri = (lax.broadcasted_iota(jnp.int32, (LANES, LANES), 0)
           <= lax.broadcasted_iota(jnp.int32, (LANES, LANES), 1))
    tri = jnp.where(tri, 1.0, 0.0).astype(BF16)

    def chunk_cumsum(mask):
        flat = jnp.where(mask, 1.0, 0.0).astype(BF16).reshape(n_chunks * N_EXPERTS, LANES)
        inc = _dot(flat, tri).reshape(n_chunks, N_EXPERTS, LANES)
        run = jnp.zeros((N_EXPERTS, 1), F32)
        before = []
        for c in range(n_chunks):
            before.append(run)
            run = run + inc[c][:, LANES - 1:LANES]
        return inc, jnp.stack(before, axis=0)

    above = bits > thr
    tied = bits == thr
    need = cap - count(above)
    tie_inc, tie_before = chunk_cumsum(tied)
    tie_rank = tie_before + tie_inc - jnp.where(tied, 1.0, 0.0)
    sel = above | (tied & (tie_rank < need))
    sel_inc, _ = chunk_cumsum(sel)
    cum_s[...] = sel_inc

    lower = (lax.broadcasted_iota(jnp.int32, (n_chunks, n_chunks), 1)
             <= lax.broadcasted_iota(jnp.int32, (n_chunks, n_chunks), 0))
    lower = jnp.where(lower, 1.0, 0.0).astype(BF16)
    slot = lax.broadcasted_iota(jnp.int32, (1, cap), 1).astype(F32)
    chunk_id = lax.broadcasted_iota(jnp.int32, (n_chunks, cap), 0).astype(F32)
    lane_id = lax.broadcasted_iota(jnp.int32, (LANES, cap), 0).astype(F32)
    for e in range(N_EXPERTS):
        cin = cum_s[:, e, :]
        tot = jnp.broadcast_to(cin[:, LANES - 1:LANES], cin.shape)
        cc_inc = _dot(lower, tot.astype(BF16))[:, 0:1]
        cc_exc = cc_inc - tot[:, 0:1]
        kc = jnp.sum(jnp.where(cc_inc <= slot, 1.0, 0.0), axis=0, keepdims=True)
        onehot = chunk_id == kc
        onehot_b = jnp.where(onehot, 1.0, 0.0).astype(BF16)
        counts_t = _dot_tn(cin.astype(BF16), onehot_b)
        local = slot - jnp.sum(jnp.where(onehot, cc_exc, 0.0), axis=0, keepdims=True)
        il = jnp.sum(jnp.where(counts_t <= local, 1.0, 0.0), axis=0, keepdims=True)
        idx_ref[0, e:e + 1, :] = (kc * LANES + il).astype(jnp.int32)
        aff_t = jnp.zeros((LANES, cap), F32)
        for part in _split_bf16(aff_s[:, e, :], 3):
            aff_t = aff_t + _dot_tn(part, onehot_b)
        gate_ref[0, e:e + 1, :] = jnp.sum(jnp.where(lane_id == il, aff_t, 0.0),
                                          axis=0, keepdims=True)


def _topk(lg4, cap):
    bsz, n_chunks = lg4.shape[0], lg4.shape[1]
    return pl.pallas_call(
        functools.partial(_topk_kernel, cap=cap),
        out_shape=(
            jax.ShapeDtypeStruct((bsz, N_EXPERTS, cap), jnp.int32),
            jax.ShapeDtypeStruct((bsz, N_EXPERTS, cap), F32),
        ),
        grid=(bsz,),
        in_specs=[pl.BlockSpec((1, n_chunks, N_EXPERTS, LANES), lambda b: (b, 0, 0, 0))],
        out_specs=(
            pl.BlockSpec((1, N_EXPERTS, cap), lambda b: (b, 0, 0)),
            pl.BlockSpec((1, N_EXPERTS, cap), lambda b: (b, 0, 0)),
        ),
        scratch_shapes=[
            pltpu.VMEM((n_chunks, N_EXPERTS, LANES), F32),
            pltpu.VMEM((n_chunks, N_EXPERTS, LANES), F32),
        ],
        compiler_params=pltpu.CompilerParams(
            dimension_semantics=("arbitrary",), vmem_limit_bytes=VMEM_LIMIT_BYTES),
        name="topk",
    )(lg4)


def _moe_kernel(idx_ref, gate_ref, h2t_hbm, wg_ref, wu_ref, wd_ref, acc_hbm,
                h2_s, acc_s, xs_s, ys_s, sem, *, n, cap, sub):
    b = pl.program_id(0)
    e = pl.program_id(1)
    tok_rows = n * FEATURE_CHUNKS

    @pl.when(e == 0)
    def _():
        cp = pltpu.make_async_copy(h2t_hbm.at[pl.ds(b * tok_rows, tok_rows), :], h2_s, sem)
        cp.start()
        acc_s[...] = jnp.zeros_like(acc_s)
        cp.wait()

    for part in range(cap // sub):
        base = part * sub

        def gather_body(r8, carry):
            for u in range(SUBLANES):
                r = r8 * SUBLANES + u
                t = idx_ref[0, 0, 0, base + r]
                src = pl.multiple_of(t * FEATURE_CHUNKS, FEATURE_CHUNKS)
                dst = pl.multiple_of(r * FEATURE_CHUNKS, FEATURE_CHUNKS)
                xs_s[pl.ds(dst, FEATURE_CHUNKS), :] = h2_s[pl.ds(src, FEATURE_CHUNKS), :]
            return carry
        lax.fori_loop(0, sub // SUBLANES, gather_body, 0)

        x = jnp.concatenate(
            [xs_s[pl.ds(s, sub, stride=FEATURE_CHUNKS), :] for s in range(FEATURE_CHUNKS)],
            axis=1).astype(BF16)
        hid = (jax.nn.silu(_dot(x, wg_ref[0])) * _dot(x, wu_ref[0])).astype(BF16)
        y = _dot(hid, wd_ref[0])
        for s in range(FEATURE_CHUNKS):
            ys_s[pl.ds(s, sub, stride=FEATURE_CHUNKS), :] = y[:, s * LANES:(s + 1) * LANES]

        def scatter_body(r8, carry):
            for u in range(SUBLANES):
                r = r8 * SUBLANES + u
                t = idx_ref[0, 0, 0, base + r]
                gv = gate_ref[0, 0, 0, base + r]
                dst = pl.ds(pl.multiple_of(t * FEATURE_CHUNKS, FEATURE_CHUNKS), FEATURE_CHUNKS)
                src = pl.ds(pl.multiple_of(r * FEATURE_CHUNKS, FEATURE_CHUNKS), FEATURE_CHUNKS)
                acc_s[dst, :] = acc_s[dst, :] + ys_s[src, :] * gv
            return carry
        lax.fori_loop(0, sub // SUBLANES, scatter_body, 0)

    @pl.when(e == pl.num_programs(1) - 1)
    def _():
        cp = pltpu.make_async_copy(acc_s, acc_hbm.at[pl.ds(b * tok_rows, tok_rows), :], sem)
        cp.start()
        cp.wait()


def _moe(idx, gate, h2t, wg, wu, wd, bsz, n):
    cap = idx.shape[-1]
    sub = min(256, cap)
    tok_rows = n * FEATURE_CHUNKS
    weight_spec = pl.BlockSpec((1, D_MODEL, D_MODEL), lambda b, e: (e, 0, 0))
    smem_spec = pl.BlockSpec((1, 1, 1, cap), lambda b, e: (b, e, 0, 0),
                             memory_space=pltpu.SMEM)
    return pl.pallas_call(
        functools.partial(_moe_kernel, n=n, cap=cap, sub=sub),
        out_shape=jax.ShapeDtypeStruct((bsz * tok_rows, LANES), F32),
        grid=(bsz, N_EXPERTS),
        in_specs=[
            smem_spec,
            smem_spec,
            pl.BlockSpec(memory_space=pl.ANY),
            weight_spec,
            weight_spec,
            weight_spec,
        ],
        out_specs=pl.BlockSpec(memory_space=pl.ANY),
        scratch_shapes=[
            pltpu.VMEM((tok_rows, LANES), F32),
            pltpu.VMEM((tok_rows, LANES), F32),
            pltpu.VMEM((sub * FEATURE_CHUNKS, LANES), F32),
            pltpu.VMEM((sub * FEATURE_CHUNKS, LANES), F32),
            pltpu.SemaphoreType.DMA,
        ],
        compiler_params=pltpu.CompilerParams(
            dimension_semantics=("arbitrary", "arbitrary"), vmem_limit_bytes=VMEM_LIMIT_BYTES),
        name="moe",
    )(idx[:, :, None, :], gate[:, :, None, :], h2t, wg, wu, wd)


def _final_kernel(x1_ref, acc_ref, mod_ref, o_ref):
    tm = x1_ref.shape[0]
    g2 = mod_ref[0, 5:6, :]
    for s in range(FEATURE_CHUNKS):
        cols = slice(s * LANES, (s + 1) * LANES)
        piece = acc_ref[pl.ds(s, tm, stride=FEATURE_CHUNKS), :]
        o_ref[:, cols] = x1_ref[:, cols] + g2[:, cols] * piece


def _final(x1, acc_t, mod3, bsz, n):
    tm = _row_tile(n)
    tps = n // tm
    rows = bsz * n
    return pl.pallas_call(
        _final_kernel,
        out_shape=jax.ShapeDtypeStruct((rows, D_MODEL), F32),
        grid=(rows // tm,),
        in_specs=[
            pl.BlockSpec((tm, D_MODEL), lambda i: (i, 0)),
            pl.BlockSpec((tm * FEATURE_CHUNKS, LANES), lambda i: (i, 0)),
            pl.BlockSpec((1, N_MOD, D_MODEL), lambda i: (i // tps, 0, 0)),
        ],
        out_specs=pl.BlockSpec((tm, D_MODEL), lambda i: (i, 0)),
        compiler_params=pltpu.CompilerParams(
            dimension_semantics=("arbitrary",), vmem_limit_bytes=VMEM_LIMIT_BYTES),
        name="final",
    )(x1, acc_t, mod3)


def _rope_tables(n):
    t = jnp.arange(n)
    row = (t // GRID_W).astype(F32)
    col = (t % GRID_W).astype(F32)
    n_freq = HEAD_DIM // 4
    inv_freq = ROPE_BASE ** (-jnp.arange(n_freq, dtype=F32) / n_freq)
    ang_r = row[:, None] * inv_freq
    ang_c = col[:, None] * inv_freq
    cos = jnp.concatenate([jnp.cos(ang_r)] * 2 + [jnp.cos(ang_c)] * 2, axis=1)
    sin = jnp.concatenate([-jnp.sin(ang_r), jnp.sin(ang_r), -jnp.sin(ang_c), jnp.sin(ang_c)],
                          axis=1)
    return jnp.tile(cos, (1, 2)), jnp.tile(sin, (1, 2))


def _pack_gate_weights(w_r, b_r, w_i, b_i):
    n_p = LRU_WIDTH // LANES
    zeros = jnp.zeros((LRU_BLOCK_DIM, LRU_BLOCK_DIM), F32)

    def pair(w, p):
        top = jnp.concatenate([w[2 * p], zeros], axis=1)
        bot = jnp.concatenate([zeros, w[2 * p + 1]], axis=1)
        return jnp.concatenate([top, bot], axis=0)

    ws, bs = [], []
    for p in range(n_p):
        ws.append(jnp.concatenate(
            [pair(w_r[0], p), pair(w_i[0], p), pair(w_r[1], p), pair(w_i[1], p)], axis=1))
        sl = slice(p * LANES, (p + 1) * LANES)
        bs.append(jnp.concatenate([b_r[0, sl], b_i[0, sl], b_r[1, sl], b_i[1, sl]])[None, :])
    return jnp.stack(ws).astype(BF16), jnp.stack(bs)


def _layer(x, ctx, mod3, norm1_g, norm2_g, w_in, q_norm_g, k_norm_g, attn_sink, conv_w, conv_b,
           lru_w_r, lru_b_r, lru_w_i, lru_b_i, lru_lambda, w_out, w_router, w_gate, w_up, w_down):
    bsz, n, _ = x.shape
    n_ctx = ctx.shape[1]
    x2 = x.reshape(bsz * n, D_MODEL)
    c2 = ctx.reshape(bsz * n_ctx, D_MODEL)
    g1 = norm1_g[None, :]
    g2 = norm2_g[None, :]
    qg = jnp.tile(q_norm_g, 2)[None, :]
    kg = jnp.tile(k_norm_g, 2)[None, :]
    cos, sin = _rope_tables(n)
    w_in_b = w_in.astype(BF16)
    w_ctx_b = w_in_b[:, ATTN_WIDTH:ATTN_WIDTH + 2 * KV_WIDTH + LRU_WIDTH]

    q_s, k2, v2, xr2, gg2 = _inproj_latent(x2, mod3, g1, w_in_b, qg, kg, cos, sin, bsz, n)
    kc2, vc2, xrc2 = _inproj_ctx(c2, mod3, g1, w_ctx_b, kg, bsz, n_ctx)

    o_s = _attn(attn_sink, q_s, k2.reshape(bsz, n, KV_WIDTH), v2.reshape(bsz, n, KV_WIDTH),
                kc2.reshape(bsz, n_ctx, KV_WIDTH), vc2.reshape(bsz, n_ctx, KV_WIDTH))

    wg_lru, bg_lru = _pack_gate_weights(lru_w_r, lru_b_r, lru_w_i, lru_b_i)
    rg3 = _lru(xr2.reshape(bsz, n, LRU_WIDTH), xrc2.reshape(bsz, n_ctx, LRU_WIDTH),
               gg2.reshape(bsz, n, LRU_WIDTH), conv_w, conv_b[None, :], wg_lru, bg_lru,
               lru_lambda)

    w_att = (w_out[:ATTN_WIDTH].reshape(N_KV_HEADS, Q_PER_KV, HEAD_DIM, D_MODEL)
             .transpose(1, 0, 2, 3).reshape(ATTN_WIDTH, D_MODEL).astype(BF16))
    w_rnn = w_out[ATTN_WIDTH:].astype(BF16)
    x1, h2t, lg4 = _outproj(o_s, rg3.reshape(bsz * n, LRU_WIDTH), x2, mod3, w_att, w_rnn, g2,
                            w_router.T.astype(BF16), bsz, n)

    cap = CAPACITY_FACTOR * n // N_EXPERTS
    idx, gate = _topk(lg4, cap)
    acc_t = _moe(idx, gate, h2t, w_gate.astype(BF16), w_up.astype(BF16), w_down.astype(BF16),
                 bsz, n)
    return _final(x1, acc_t, mod3, bsz, n).reshape(bsz, n, D_MODEL)


def kernel(x, c, ctx, c_ctx, w_ada, b_ada, norm1_g, norm2_g, w_in, q_norm_g, k_norm_g, attn_sink, conv_w, conv_b, lru_w_r, lru_b_r, lru_w_i, lru_b_i, lru_lambda, w_out, w_router, w_gate, w_up, w_down):
    bsz = x.shape[0]
    depth = w_ada.shape[0]
    assert depth == 1, "context-stream update between layers is not implemented"
    pad_rows = -(bsz + 1) % SUBLANES
    cc = jnp.concatenate([c, jnp.zeros((pad_rows, D_MODEL), F32), c_ctx[None, :]], axis=0)
    for layer in range(depth):
        mod = _ada(cc, w_ada[layer], b_ada[layer][None, :])
        mod3 = mod.reshape(cc.shape[0], N_MOD, D_MODEL)
        x = _layer(x, ctx, mod3, norm1_g[layer], norm2_g[layer], w_in[layer], q_norm_g[layer],
                   k_norm_g[layer], attn_sink[layer], conv_w[layer], conv_b[layer],
                   lru_w_r[layer], lru_b_r[layer], lru_w_i[layer], lru_b_i[layer],
                   lru_lambda[layer], w_out[layer], w_router[layer], w_gate[layer],
                   w_up[layer], w_down[layer])
    return x
```

```python
import functools

import jax
import jax.numpy as jnp
from jax import lax
from jax.experimental import pallas as pl
from jax.experimental.pallas import tpu as pltpu

F32 = jnp.float32
BF16 = jnp.bfloat16

LANES = 128
SUBLANES = 8
VMEM_LIMIT_BYTES = 56 * 1024 * 1024

D_MODEL = 1024
HEAD_DIM = 64
N_Q_HEADS = 8
N_KV_HEADS = 2
Q_PER_KV = N_Q_HEADS // N_KV_HEADS
ATTN_WIDTH = N_Q_HEADS * HEAD_DIM
KV_WIDTH = N_KV_HEADS * HEAD_DIM
LRU_WIDTH = D_MODEL - ATTN_WIDTH
LRU_BLOCK_DIM = 64
IN_WIDTH = ATTN_WIDTH + 2 * KV_WIDTH + 2 * LRU_WIDTH
WINDOW = 128
BLOCK = 128
GRID_W = 64
ROPE_BASE = 10000.0
LRU_C = 8.0
N_EXPERTS = 16
CAPACITY_FACTOR = 2
EPS = 1e-6
NEG_INF = -1e30
N_MOD = 6
FEATURE_CHUNKS = D_MODEL // LANES
SCATTER_UNROLL = 16


def _dot(a, b):
    return jnp.dot(a, b, preferred_element_type=F32)


def _dot_nt(a, b):
    return lax.dot_general(a, b, (((1,), (1,)), ((), ())), preferred_element_type=F32)


def _dot_tn(a, b):
    return lax.dot_general(a, b, (((0,), (0,)), ((), ())), preferred_element_type=F32)


def _split_bf16(x, parts):
    out = []
    r = x
    for _ in range(parts):
        p = r.astype(BF16)
        out.append(p)
        r = r - p.astype(F32)
    return out


def _rmsnorm_rows(x, g):
    ms = jnp.mean(x * x, axis=-1, keepdims=True)
    return x * lax.rsqrt(ms + EPS) * g


def _ada_kernel(c_ref, w_ref, b_ref, o_ref):
    s = jax.nn.silu(c_ref[...])
    s_hi, s_lo = _split_bf16(s, 2)
    w_hi, w_lo = _split_bf16(w_ref[...], 2)
    o_ref[...] = _dot(s_hi, w_hi) + _dot(s_hi, w_lo) + _dot(s_lo, w_hi) + b_ref[...]


def _ada(cc, w_ada, b_ada):
    rows = cc.shape[0]
    width = w_ada.shape[1]
    return pl.pallas_call(
        _ada_kernel,
        out_shape=jax.ShapeDtypeStruct((rows, width), F32),
        grid=(width // D_MODEL,),
        in_specs=[
            pl.BlockSpec((rows, D_MODEL), lambda j: (0, 0)),
            pl.BlockSpec((D_MODEL, D_MODEL), lambda j: (0, j)),
            pl.BlockSpec((1, D_MODEL), lambda j: (0, j)),
        ],
        out_specs=pl.BlockSpec((rows, D_MODEL), lambda j: (0, j)),
        compiler_params=pltpu.CompilerParams(
            dimension_semantics=("arbitrary",), vmem_limit_bytes=VMEM_LIMIT_BYTES),
        name="ada",
    )(cc, w_ada, b_ada)


def _head_norm(p, g):
    lane = lax.broadcasted_iota(jnp.int32, p.shape, 1)
    low = lane < HEAD_DIM
    sq = p * p
    s_low = jnp.sum(jnp.where(low, sq, 0.0), axis=-1, keepdims=True)
    s_high = jnp.sum(jnp.where(low, 0.0, sq), axis=-1, keepdims=True)
    ms = jnp.where(low, s_low, s_high) * (1.0 / HEAD_DIM)
    return p * lax.rsqrt(ms + EPS) * g


def _rope(p, cos, sin_signed):
    lane = lax.broadcasted_iota(jnp.int32, p.shape, 1)
    partner = jnp.where((lane & 16) == 0,
                        pltpu.roll(p, LANES - 16, 1), pltpu.roll(p, 16, 1))
    return p * cos + partner * sin_signed


def _inproj_latent_kernel(x_ref, mod_ref, g1_ref, w_ref, qg_ref, kg_ref, cos_ref, sin_ref,
                          q_ref, k_ref, v_ref, xr_ref, gg_ref):
    x = x_ref[...]
    h = _rmsnorm_rows(x, g1_ref[...]) * (1.0 + mod_ref[0, 1:2, :]) + mod_ref[0, 0:1, :]
    y = _dot(h.astype(BF16), w_ref[...])
    cos = cos_ref[...]
    sin = sin_ref[...]
    pieces = []
    for j in range(ATTN_WIDTH // LANES):
        p = _head_norm(y[:, j * LANES:(j + 1) * LANES], qg_ref[...])
        pieces.append(_rope(p, cos, sin))
    lane = lax.broadcasted_iota(jnp.int32, pieces[0].shape, 1)
    low = lane < HEAD_DIM
    n_blocks = x.shape[0] // BLOCK
    for g in range(Q_PER_KV):
        src0 = pieces[g // 2]
        src1 = pieces[2 + g // 2]
        if g % 2 == 0:
            out = jnp.where(low, src0, pltpu.roll(src1, HEAD_DIM, 1))
        else:
            out = jnp.where(low, pltpu.roll(src0, HEAD_DIM, 1), src1)
        out = out.astype(BF16)
        for j in range(n_blocks):
            q_ref[0, j, g * BLOCK:(g + 1) * BLOCK, :] = out[j * BLOCK:(j + 1) * BLOCK, :]
    k = _head_norm(y[:, ATTN_WIDTH:ATTN_WIDTH + KV_WIDTH], kg_ref[...])
    k_ref[...] = _rope(k, cos, sin).astype(BF16)
    o = ATTN_WIDTH + KV_WIDTH
    v_ref[...] = y[:, o:o + KV_WIDTH].astype(BF16)
    o += KV_WIDTH
    xr_ref[...] = y[:, o:o + LRU_WIDTH]
    o += LRU_WIDTH
    gg_ref[...] = jax.nn.gelu(y[:, o:o + LRU_WIDTH])


def _inproj_ctx_kernel(x_ref, mod_ref, g1_ref, w_ref, kg_ref, k_ref, v_ref, xr_ref):
    x = x_ref[...]
    h = _rmsnorm_rows(x, g1_ref[...]) * (1.0 + mod_ref[0, 1:2, :]) + mod_ref[0, 0:1, :]
    y = _dot(h.astype(BF16), w_ref[...])
    k_ref[...] = _head_norm(y[:, 0:KV_WIDTH], kg_ref[...]).astype(BF16)
    v_ref[...] = y[:, KV_WIDTH:2 * KV_WIDTH].astype(BF16)
    xr_ref[...] = y[:, 2 * KV_WIDTH:2 * KV_WIDTH + LRU_WIDTH]


def _row_tile(n):
    return min(512, n)


def _inproj_latent(x2, mod3, g1, w_in, qg, kg, cos, sin, bsz, n):
    tm = _row_tile(n)
    tps = n // tm
    rows = bsz * n
    const = lambda i: (0, 0)
    return pl.pallas_call(
        _inproj_latent_kernel,
        out_shape=(
            jax.ShapeDtypeStruct((bsz, n // BLOCK, Q_PER_KV * BLOCK, LANES), BF16),
            jax.ShapeDtypeStruct((rows, KV_WIDTH), BF16),
            jax.ShapeDtypeStruct((rows, KV_WIDTH), BF16),
            jax.ShapeDtypeStruct((rows, LRU_WIDTH), F32),
            jax.ShapeDtypeStruct((rows, LRU_WIDTH), F32),
        ),
        grid=(rows // tm,),
        in_specs=[
            pl.BlockSpec((tm, D_MODEL), lambda i: (i, 0)),
            pl.BlockSpec((1, N_MOD, D_MODEL), lambda i: (i // tps, 0, 0)),
            pl.BlockSpec((1, D_MODEL), const),
            pl.BlockSpec((D_MODEL, IN_WIDTH), const),
            pl.BlockSpec((1, LANES), const),
            pl.BlockSpec((1, LANES), const),
            pl.BlockSpec((tm, LANES), lambda i: (i % tps, 0)),
            pl.BlockSpec((tm, LANES), lambda i: (i % tps, 0)),
        ],
        out_specs=(
            pl.BlockSpec((1, tm // BLOCK, Q_PER_KV * BLOCK, LANES),
                         lambda i: (i // tps, i % tps, 0, 0)),
            pl.BlockSpec((tm, KV_WIDTH), lambda i: (i, 0)),
            pl.BlockSpec((tm, KV_WIDTH), lambda i: (i, 0)),
            pl.BlockSpec((tm, LRU_WIDTH), lambda i: (i, 0)),
            pl.BlockSpec((tm, LRU_WIDTH), lambda i: (i, 0)),
        ),
        compiler_params=pltpu.CompilerParams(
            dimension_semantics=("arbitrary",), vmem_limit_bytes=VMEM_LIMIT_BYTES),
        name="inproj_latent",
    )(x2, mod3, g1, w_in, qg, kg, cos, sin)


def _inproj_ctx(c2, mod3, g1, w_ctx, kg, bsz, n_ctx):
    tm = _row_tile(n_ctx)
    rows = bsz * n_ctx
    ctx_row = mod3.shape[0] - 1
    width = w_ctx.shape[1]
    const = lambda i: (0, 0)
    return pl.pallas_call(
        _inproj_ctx_kernel,
        out_shape=(
            jax.ShapeDtypeStruct((rows, KV_WIDTH), BF16),
            jax.ShapeDtypeStruct((rows, KV_WIDTH), BF16),
            jax.ShapeDtypeStruct((rows, LRU_WIDTH), F32),
        ),
        grid=(rows // tm,),
        in_specs=[
            pl.BlockSpec((tm, D_MODEL), lambda i: (i, 0)),
            pl.BlockSpec((1, N_MOD, D_MODEL), lambda i: (ctx_row, 0, 0)),
            pl.BlockSpec((1, D_MODEL), const),
            pl.BlockSpec((D_MODEL, width), const),
            pl.BlockSpec((1, LANES), const),
        ],
        out_specs=(
            pl.BlockSpec((tm, KV_WIDTH), lambda i: (i, 0)),
            pl.BlockSpec((tm, KV_WIDTH), lambda i: (i, 0)),
            pl.BlockSpec((tm, LRU_WIDTH), lambda i: (i, 0)),
        ),
        compiler_params=pltpu.CompilerParams(
            dimension_semantics=("arbitrary",), vmem_limit_bytes=VMEM_LIMIT_BYTES),
        name="inproj_ctx",
    )(c2, mod3, g1, w_ctx, kg)


def _attn_kernel(sink_ref, q_ref, k_ref, v_ref, kc_ref, vc_ref, o_ref, *, n):
    i = pl.program_id(1)
    span = 3 * BLOCK
    start = pl.multiple_of(jnp.clip((i - 1) * BLOCK, 0, n - span), BLOCK)
    q = q_ref[0, 0]
    kw = k_ref[0, pl.ds(start, span), :]
    vw = v_ref[0, pl.ds(start, span), :]
    kc = kc_ref[0]
    vc = vc_ref[0]
    rows = q.shape[0]
    row = lax.broadcasted_iota(jnp.int32, (rows, span), 0)
    col = lax.broadcasted_iota(jnp.int32, (rows, span), 1)
    q_pos = i * BLOCK + (row & (BLOCK - 1))
    valid = jnp.abs(q_pos - (start + col)) <= WINDOW
    group = lax.broadcasted_iota(jnp.int32, (rows, 1), 0) // BLOCK
    scale = HEAD_DIM ** -0.5
    acc = jnp.zeros((rows, LANES), F32)
    for h in range(N_KV_HEADS):
        def head_only(t):
            lane = lax.broadcasted_iota(jnp.int32, t.shape, 1)
            keep = (lane < HEAD_DIM) if h == 0 else (lane >= HEAD_DIM)
            return jnp.where(keep, t, jnp.zeros_like(t))
        s_loc = jnp.where(valid, _dot_nt(q, head_only(kw)) * scale, NEG_INF)
        s_ctx = _dot_nt(q, head_only(kc)) * scale
        sink = jnp.zeros((rows, 1), F32)
        for g in range(Q_PER_KV):
            sink = jnp.where(group == g, sink_ref[h * Q_PER_KV + g], sink)
        m = jnp.maximum(jnp.maximum(jnp.max(s_loc, axis=-1, keepdims=True),
                                    jnp.max(s_ctx, axis=-1, keepdims=True)), sink)
        p_loc = jnp.exp(s_loc - m)
        p_ctx = jnp.exp(s_ctx - m)
        denom = (jnp.sum(p_loc, axis=-1, keepdims=True)
                 + jnp.sum(p_ctx, axis=-1, keepdims=True) + jnp.exp(sink - m))
        o = _dot(p_loc.astype(BF16), head_only(vw)) + _dot(p_ctx.astype(BF16), head_only(vc))
        acc = acc + o / denom
    o_ref[0, 0] = acc.astype(BF16)


def _attn(sink, q_s, k3, v3, kc3, vc3):
    bsz, nb = q_s.shape[0], q_s.shape[1]
    n = k3.shape[1]
    n_ctx = kc3.shape[1]
    return pl.pallas_call(
        functools.partial(_attn_kernel, n=n),
        out_shape=jax.ShapeDtypeStruct(q_s.shape, BF16),
        grid=(bsz, nb),
        in_specs=[
            pl.BlockSpec(memory_space=pltpu.SMEM),
            pl.BlockSpec((1, 1, Q_PER_KV * BLOCK, LANES), lambda b, i: (b, i, 0, 0)),
            pl.BlockSpec((1, n, KV_WIDTH), lambda b, i: (b, 0, 0)),
            pl.BlockSpec((1, n, KV_WIDTH), lambda b, i: (b, 0, 0)),
            pl.BlockSpec((1, n_ctx, KV_WIDTH), lambda b, i: (b, 0, 0)),
            pl.BlockSpec((1, n_ctx, KV_WIDTH), lambda b, i: (b, 0, 0)),
        ],
        out_specs=pl.BlockSpec((1, 1, Q_PER_KV * BLOCK, LANES), lambda b, i: (b, i, 0, 0)),
        compiler_params=pltpu.CompilerParams(
            dimension_semantics=("arbitrary", "arbitrary"), vmem_limit_bytes=VMEM_LIMIT_BYTES),
        name="attn",
    )(sink, q_s, k3, v3, kc3, vc3)


def _softplus(z):
    return jnp.maximum(z, 0.0) + jnp.log1p(jnp.exp(-jnp.abs(z)))


def _lru_kernel(xr_ref, xrc_ref, gg_ref, cw_ref, cb_ref, wg_ref, bg_ref, lam_ref, o_ref,
                xc_s, xcc_s, hf_s, *, n, n_ctx, tile):
    cw = cw_ref[...]
    cb = cb_ref[...]
    n_groups = tile // SUBLANES

    def conv_tile(src_ref, dst_ref, t0, length):
        x = src_ref[0, pl.ds(t0, tile), :]
        lo = pl.multiple_of(jnp.maximum(t0 - SUBLANES, 0), SUBLANES)
        hi = pl.multiple_of(jnp.minimum(t0 + tile, length - SUBLANES), SUBLANES)
        prev = jnp.where(t0 > 0, src_ref[0, pl.ds(lo, SUBLANES), :], 0.0)
        nxt = jnp.where(t0 + tile < length, src_ref[0, pl.ds(hi, SUBLANES), :], 0.0)
        row = lax.broadcasted_iota(jnp.int32, x.shape, 0)
        xm1 = jnp.where(row == 0, prev[7:8], pltpu.roll(x, 1, 0))
        xm2 = jnp.where(row == 0, prev[6:7],
                        jnp.where(row == 1, prev[7:8], pltpu.roll(x, 2, 0)))
        xp1 = jnp.where(row == tile - 1, nxt[0:1], pltpu.roll(x, tile - 1, 0))
        y = cb + cw[0:1] * xm2
        y = y + cw[1:2] * xm1
        y = y + cw[2:3] * x
        y = y + cw[3:4] * xp1
        dst_ref[pl.ds(t0, tile), :] = y

    def coeffs(xc, d):
        w = wg_ref[0, :, 2 * LANES * d:2 * LANES * (d + 1)]
        g = _dot(xc.astype(BF16), w) + bg_ref[0, :, 2 * LANES * d:2 * LANES * (d + 1)]
        r = jax.nn.sigmoid(g[:, :LANES])
        i = jax.nn.sigmoid(g[:, LANES:])
        log_a = -LRU_C * r * _softplus(-lam_ref[d:d + 1, :])
        a = jnp.exp(log_a)
        th = jnp.tanh(log_a)
        mult = jnp.sqrt(-2.0 * th / (1.0 - th))
        return a, mult * i * xc

    def scan_tile(a, b, h, reverse):
        a3 = a.reshape(n_groups, SUBLANES, LANES)
        b3 = b.reshape(n_groups, SUBLANES, LANES)
        sub = lax.broadcasted_iota(jnp.int32, a3.shape, 1)
        for s in (1, 2, 4):
            shift = SUBLANES - s if reverse else s
            ok = (sub < SUBLANES - s) if reverse else (sub >= s)
            a_prev = jnp.where(ok, pltpu.roll(a3, shift, 1), 1.0)
            b_prev = jnp.where(ok, pltpu.roll(b3, shift, 1), 0.0)
            b3 = a3 * b_prev + b3
            a3 = a3 * a_prev
        last = 0 if reverse else SUBLANES - 1
        states = [None] * n_groups
        for g in (range(n_groups - 1, -1, -1) if reverse else range(n_groups)):
            hg = b3[g] + a3[g] * h
            states[g] = hg
            h = hg[last:last + 1, :]
        return states, h

    for j in range(n_ctx // tile):
        conv_tile(xrc_ref, xcc_s, j * tile, n_ctx)

    def conv_body(j, carry):
        conv_tile(xr_ref, xc_s, pl.multiple_of(j * tile, tile), n)
        return carry
    lax.fori_loop(0, n // tile, conv_body, 0)

    h = jnp.zeros((1, LANES), F32)
    for j in range(n_ctx // tile):
        a, b = coeffs(xcc_s[j * tile:(j + 1) * tile, :], 0)
        _, h = scan_tile(a, b, h, False)

    def fwd_body(j, h):
        t0 = pl.multiple_of(j * tile, tile)
        a, b = coeffs(xc_s[pl.ds(t0, tile), :], 0)
        states, h = scan_tile(a, b, h, False)
        for g in range(n_groups):
            hf_s[pl.ds(t0 + g * SUBLANES, SUBLANES), :] = states[g]
        return h
    lax.fori_loop(0, n // tile, fwd_body, h)

    h = jnp.zeros((1, LANES), F32)
    for j in range(n_ctx // tile - 1, -1, -1):
        a, b = coeffs(xcc_s[j * tile:(j + 1) * tile, :], 1)
        _, h = scan_tile(a, b, h, True)

    def rev_body(jj, h):
        t0 = pl.multiple_of((n // tile - 1 - jj) * tile, tile)
        a, b = coeffs(xc_s[pl.ds(t0, tile), :], 1)
        states, h = scan_tile(a, b, h, True)
        for g2 in range(n_groups // 2):
            rows = pl.ds(t0 + g2 * 2 * SUBLANES, 2 * SUBLANES)
            hb = jnp.concatenate([states[2 * g2], states[2 * g2 + 1]], axis=0)
            o_ref[0, rows, :] = ((hf_s[rows, :] + hb) * gg_ref[0, rows, :]).astype(BF16)
        return h
    lax.fori_loop(0, n // tile, rev_body, h)


def _lru(xr3, xrc3, gg3, conv_w, conv_b, wg, bg, lam):
    bsz, n, _ = xr3.shape
    n_ctx = xrc3.shape[1]
    tile = min(256, n, n_ctx)
    n_p = LRU_WIDTH // LANES
    return pl.pallas_call(
        functools.partial(_lru_kernel, n=n, n_ctx=n_ctx, tile=tile),
        out_shape=jax.ShapeDtypeStruct((bsz, n, LRU_WIDTH), BF16),
        grid=(bsz, n_p),
        in_specs=[
            pl.BlockSpec((1, n, LANES), lambda b, p: (b, 0, p)),
            pl.BlockSpec((1, n_ctx, LANES), lambda b, p: (b, 0, p)),
            pl.BlockSpec((1, n, LANES), lambda b, p: (b, 0, p)),
            pl.BlockSpec((conv_w.shape[0], LANES), lambda b, p: (0, p)),
            pl.BlockSpec((1, LANES), lambda b, p: (0, p)),
            pl.BlockSpec((1, LANES, 4 * LANES), lambda b, p: (p, 0, 0)),
            pl.BlockSpec((1, 1, 4 * LANES), lambda b, p: (p, 0, 0)),
            pl.BlockSpec((2, LANES), lambda b, p: (0, p)),
        ],
        out_specs=pl.BlockSpec((1, n, LANES), lambda b, p: (b, 0, p)),
        scratch_shapes=[
            pltpu.VMEM((n, LANES), F32),
            pltpu.VMEM((n_ctx, LANES), F32),
            pltpu.VMEM((n, LANES), F32),
        ],
        compiler_params=pltpu.CompilerParams(
            dimension_semantics=("arbitrary", "arbitrary"), vmem_limit_bytes=VMEM_LIMIT_BYTES),
        name="lru",
    )(xr3, xrc3, gg3, conv_w, conv_b, wg, bg, lam)


def _outproj_kernel(o_ref, rg_ref, x_ref, mod_ref, watt_ref, wrnn_ref, g2_ref, wr_ref,
                    x1t_ref, h2t_ref, lg_ref):
    tm = x_ref.shape[0]
    n_blocks = tm // BLOCK
    att = jnp.concatenate(
        [jnp.concatenate([o_ref[0, j, g * BLOCK:(g + 1) * BLOCK, :] for g in range(Q_PER_KV)],
                         axis=1) for j in range(n_blocks)], axis=0)
    y = _dot(att, watt_ref[...]) + _dot(rg_ref[...], wrnn_ref[...])
    x1 = x_ref[...] + mod_ref[0, 2:3, :] * y
    h2 = _rmsnorm_rows(x1, g2_ref[...]) * (1.0 + mod_ref[0, 4:5, :]) + mod_ref[0, 3:4, :]
    for s in range(FEATURE_CHUNKS):
        rows = pl.ds(s, tm, stride=FEATURE_CHUNKS)
        x1t_ref[rows, :] = x1[:, s * LANES:(s + 1) * LANES]
        h2t_ref[rows, :] = h2[:, s * LANES:(s + 1) * LANES]
    lt = _dot_nt(wr_ref[...], h2.astype(BF16))
    for j in range(n_blocks):
        lg_ref[0, j] = lt[:, j * LANES:(j + 1) * LANES]


def _outproj(o_s, rg2, x2, mod3, w_att, w_rnn, g2, wr_t, bsz, n):
    tm = _row_tile(n)
    tps = n // tm
    rows = bsz * n
    const = lambda i: (0, 0)
    return pl.pallas_call(
        _outproj_kernel,
        out_shape=(
            jax.ShapeDtypeStruct((rows * FEATURE_CHUNKS, LANES), F32),
            jax.ShapeDtypeStruct((rows * FEATURE_CHUNKS, LANES), F32),
            jax.ShapeDtypeStruct((bsz, n // LANES, N_EXPERTS, LANES), F32),
        ),
        grid=(rows // tm,),
        in_specs=[
            pl.BlockSpec((1, tm // BLOCK, Q_PER_KV * BLOCK, LANES),
                         lambda i: (i // tps, i % tps, 0, 0)),
            pl.BlockSpec((tm, LRU_WIDTH), lambda i: (i, 0)),
            pl.BlockSpec((tm, D_MODEL), lambda i: (i, 0)),
            pl.BlockSpec((1, N_MOD, D_MODEL), lambda i: (i // tps, 0, 0)),
            pl.BlockSpec((ATTN_WIDTH, D_MODEL), const),
            pl.BlockSpec((LRU_WIDTH, D_MODEL), const),
            pl.BlockSpec((1, D_MODEL), const),
            pl.BlockSpec((N_EXPERTS, D_MODEL), const),
        ],
        out_specs=(
            pl.BlockSpec((tm * FEATURE_CHUNKS, LANES), lambda i: (i, 0)),
            pl.BlockSpec((tm * FEATURE_CHUNKS, LANES), lambda i: (i, 0)),
            pl.BlockSpec((1, tm // LANES, N_EXPERTS, LANES), lambda i: (i // tps, i % tps, 0, 0)),
        ),
        compiler_params=pltpu.CompilerParams(
            dimension_semantics=("arbitrary",), vmem_limit_bytes=VMEM_LIMIT_BYTES),
        name="outproj",
    )(o_s, rg2, x2, mod3, w_att, w_rnn, g2, wr_t)


def _topk_kernel(lg_ref, idx_ref, gate_ref, aff_s, cum_s, *, cap):
    lg = lg_ref[0]
    n_chunks = lg.shape[0]
    m = jnp.max(lg, axis=1, keepdims=True)
    ex = jnp.exp(lg - m)
    aff = ex / jnp.sum(ex, axis=1, keepdims=True)
    aff_s[...] = aff
    bits = pltpu.bitcast(aff, jnp.int32)

    def count(mask):
        c = jnp.sum(jnp.where(mask, 1.0, 0.0), axis=0, keepdims=True)
        return jnp.sum(c, axis=2, keepdims=True)

    def bit_body(it, thr):
        cand = thr | jnp.left_shift(jnp.int32(1), 30 - it)
        return jnp.where(count(bits >= cand) >= cap, cand, thr)
    thr = lax.fori_loop(0, 31, bit_body, jnp.zeros((1, N_EXPERTS, 1), jnp.int32))

    tri = (lax.broadcasted_iota(jnp.int32, (LANES, LANES), 0)
           <= lax.broadcasted_iota(jnp.int32, (LANES, LANES), 1))
    tri = jnp.where(tri, 1.0, 0.0).astype(BF16)

    def chunk_cumsum(mask):
        flat = jnp.where(mask, 1.0, 0.0).astype(BF16).reshape(n_chunks * N_EXPERTS, LANES)
        inc = _dot(flat, tri).reshape(n_chunks, N_EXPERTS, LANES)
        run = jnp.zeros((N_EXPERTS, 1), F32)
        before = []
        for c in range(n_chunks):
            before.append(run)
            run = run + inc[c][:, LANES - 1:LANES]
        return inc, jnp.stack(before, axis=0)

    above = bits > thr
    tied = bits == thr
    need = cap - count(above)
    tie_inc, tie_before = chunk_cumsum(tied)
    tie_rank = tie_before + tie_inc - jnp.where(tied, 1.0, 0.0)
    sel = above | (tied & (tie_rank < need))
    sel_inc, _ = chunk_cumsum(sel)
    cum_s[...] = sel_inc

    lower = (lax.broadcasted_iota(jnp.int32, (n_chunks, n_chunks), 1)
             <= lax.broadcasted_iota(jnp.int32, (n_chunks, n_chunks), 0))
    lower = jnp.where(lower, 1.0, 0.0).astype(BF16)
    slot = lax.broadcasted_iota(jnp.int32, (1, cap), 1).astype(F32)
    chunk_id = lax.broadcasted_iota(jnp.int32, (n_chunks, cap), 0).astype(F32)
    lane_id = lax.broadcasted_iota(jnp.int32, (LANES, cap), 0).astype(F32)
    for e in range(N_EXPERTS):
        cin = cum_s[:, e, :]
        tot = jnp.broadcast_to(cin[:, LANES - 1:LANES], cin.shape)
        cc_inc = _dot(lower, tot.astype(BF16))[:, 0:1]
        cc_exc = cc_inc - tot[:, 0:1]
        kc = jnp.sum(jnp.where(cc_inc <= slot, 1.0, 0.0), axis=0, keepdims=True)
        onehot = chunk_id == kc
        onehot_b = jnp.where(onehot, 1.0, 0.0).astype(BF16)
        counts_t = _dot_tn(cin.astype(BF16), onehot_b)
        local = slot - jnp.sum(jnp.where(onehot, cc_exc, 0.0), axis=0, keepdims=True)
        il = jnp.sum(jnp.where(counts_t <= local, 1.0, 0.0), axis=0, keepdims=True)
        idx_ref[0, e:e + 1, :] = (kc * LANES + il).astype(jnp.int32)
        aff_t = jnp.zeros((LANES, cap), F32)
        for part in _split_bf16(aff_s[:, e, :], 3):
            aff_t = aff_t + _dot_tn(part, onehot_b)
        gate_ref[0, e:e + 1, :] = jnp.sum(jnp.where(lane_id == il, aff_t, 0.0),
                                          axis=0, keepdims=True)


def _topk(lg4, cap):
    bsz, n_chunks = lg4.shape[0], lg4.shape[1]
    return pl.pallas_call(
        functools.partial(_topk_kernel, cap=cap),
        out_shape=(
            jax.ShapeDtypeStruct((bsz, N_EXPERTS, cap), jnp.int32),
            jax.ShapeDtypeStruct((bsz, N_EXPERTS, cap), F32),
        ),
        grid=(bsz,),
        in_specs=[pl.BlockSpec((1, n_chunks, N_EXPERTS, LANES), lambda b: (b, 0, 0, 0))],
        out_specs=(
            pl.BlockSpec((1, N_EXPERTS, cap), lambda b: (b, 0, 0)),
            pl.BlockSpec((1, N_EXPERTS, cap), lambda b: (b, 0, 0)),
        ),
        scratch_shapes=[
            pltpu.VMEM((n_chunks, N_EXPERTS, LANES), F32),
            pltpu.VMEM((n_chunks, N_EXPERTS, LANES), F32),
        ],
        compiler_params=pltpu.CompilerParams(
            dimension_semantics=("arbitrary",), vmem_limit_bytes=VMEM_LIMIT_BYTES),
        name="topk",
    )(lg4)


def _moe_kernel(idx_ref, gate_ref, h2t_hbm, x1t_hbm, g2t_ref, wg_ref, wu_ref, wd_ref, out_hbm,
                h2_s, acc_s, xs_s, ys_s, stage_s, sems, *, n, cap, sub):
    b = pl.program_id(0)
    e = pl.program_id(1)
    tok_rows = n * FEATURE_CHUNKS
    sample_rows = pl.ds(b * tok_rows, tok_rows)
    load_h2 = pltpu.make_async_copy(h2t_hbm.at[sample_rows, :], h2_s, sems.at[0])
    load_x1 = pltpu.make_async_copy(x1t_hbm.at[sample_rows, :], acc_s, sems.at[1])

    @pl.when(e == 0)
    def _():
        load_h2.start()
        load_x1.start()
        load_h2.wait()

    g2t = g2t_ref[0]
    for part in range(cap // sub):
        base = part * sub

        def gather_body(r8, carry):
            for u in range(SUBLANES):
                r = r8 * SUBLANES + u
                t = idx_ref[0, 0, 0, base + r]
                src = pl.multiple_of(t * FEATURE_CHUNKS, FEATURE_CHUNKS)
                dst = pl.multiple_of(r * FEATURE_CHUNKS, FEATURE_CHUNKS)
                xs_s[pl.ds(dst, FEATURE_CHUNKS), :] = h2_s[pl.ds(src, FEATURE_CHUNKS), :]
            return carry
        lax.fori_loop(0, sub // SUBLANES, gather_body, 0)

        x = jnp.concatenate(
            [xs_s[pl.ds(s, sub, stride=FEATURE_CHUNKS), :] for s in range(FEATURE_CHUNKS)],
            axis=1).astype(BF16)
        hid = (jax.nn.silu(_dot(x, wg_ref[0])) * _dot(x, wu_ref[0])).astype(BF16)
        y = _dot(hid, wd_ref[0])
        for s in range(FEATURE_CHUNKS):
            ys_s[pl.ds(s, sub, stride=FEATURE_CHUNKS), :] = y[:, s * LANES:(s + 1) * LANES]

        if part == 0:
            @pl.when(e == 0)
            def _():
                load_x1.wait()

        def scatter_body(r8, carry):
            updates = []
            for u in range(SCATTER_UNROLL):
                r = r8 * SCATTER_UNROLL + u
                t = idx_ref[0, 0, 0, base + r]
                gv = gate_ref[0, 0, 0, base + r]
                dst = pl.ds(pl.multiple_of(t * FEATURE_CHUNKS, FEATURE_CHUNKS), FEATURE_CHUNKS)
                src = pl.ds(pl.multiple_of(r * FEATURE_CHUNKS, FEATURE_CHUNKS), FEATURE_CHUNKS)
                updates.append((dst, acc_s[dst, :] + g2t * (ys_s[src, :] * gv)))
            for dst, val in updates:
                acc_s[dst, :] = val
            return carry
        lax.fori_loop(0, sub // SCATTER_UNROLL, scatter_body, 0)

    @pl.when(e == pl.num_programs(1) - 1)
    def _():
        n_out = n // sub

        def store(c, slot):
            return pltpu.make_async_copy(
                stage_s.at[slot], out_hbm.at[pl.ds(b * n + c * sub, sub), :], sems.at[2 + slot])

        def out_body(c, carry):
            slot = c % 2

            @pl.when(c >= 2)
            def _():
                store(c - 2, slot).wait()
            row0 = pl.multiple_of(c * (sub * FEATURE_CHUNKS), sub * FEATURE_CHUNKS)
            for s in range(FEATURE_CHUNKS):
                stage_s[slot, :, s * LANES:(s + 1) * LANES] = (
                    acc_s[pl.ds(row0 + s, sub, stride=FEATURE_CHUNKS), :])
            store(c, slot).start()
            return carry
        lax.fori_loop(0, n_out, out_body, 0)
        for c in range(n_out - 2, n_out):
            store(c, c % 2).wait()


def _moe(idx, gate, h2t, x1t, g2t, wg, wu, wd, bsz, n):
    cap = idx.shape[-1]
    sub = min(256, cap)
    assert n // sub >= 2 and cap % sub == 0 and sub % SCATTER_UNROLL == 0
    tok_rows = n * FEATURE_CHUNKS
    weight_spec = pl.BlockSpec((1, D_MODEL, D_MODEL), lambda b, e: (e, 0, 0))
    smem_spec = pl.BlockSpec((1, 1, 1, cap), lambda b, e: (b, e, 0, 0),
                             memory_space=pltpu.SMEM)
    return pl.pallas_call(
        functools.partial(_moe_kernel, n=n, cap=cap, sub=sub),
        out_shape=jax.ShapeDtypeStruct((bsz * n, D_MODEL), F32),
        grid=(bsz, N_EXPERTS),
        in_specs=[
            smem_spec,
            smem_spec,
            pl.BlockSpec(memory_space=pl.ANY),
            pl.BlockSpec(memory_space=pl.ANY),
            pl.BlockSpec((1, FEATURE_CHUNKS, LANES), lambda b, e: (b, 0, 0)),
            weight_spec,
            weight_spec,
            weight_spec,
        ],
        out_specs=pl.BlockSpec(memory_space=pl.ANY),
        scratch_shapes=[
            pltpu.VMEM((tok_rows, LANES), F32),
            pltpu.VMEM((tok_rows, LANES), F32),
            pltpu.VMEM((sub * FEATURE_CHUNKS, LANES), F32),
            pltpu.VMEM((sub * FEATURE_CHUNKS, LANES), F32),
            pltpu.VMEM((2, sub, D_MODEL), F32),
            pltpu.SemaphoreType.DMA((4,)),
        ],
        compiler_params=pltpu.CompilerParams(
            dimension_semantics=("arbitrary", "arbitrary"), vmem_limit_bytes=VMEM_LIMIT_BYTES),
        name="moe",
    )(idx[:, :, None, :], gate[:, :, None, :], h2t, x1t, g2t, wg, wu, wd)


def _rope_tables(n):
    t = jnp.arange(n)
    row = (t // GRID_W).astype(F32)
    col = (t % GRID_W).astype(F32)
    n_freq = HEAD_DIM // 4
    inv_freq = ROPE_BASE ** (-jnp.arange(n_freq, dtype=F32) / n_freq)
    ang_r = row[:, None] * inv_freq
    ang_c = col[:, None] * inv_freq
    cos = jnp.concatenate([jnp.cos(ang_r)] * 2 + [jnp.cos(ang_c)] * 2, axis=1)
    sin = jnp.concatenate([-jnp.sin(ang_r), jnp.sin(ang_r), -jnp.sin(ang_c), jnp.sin(ang_c)],
                          axis=1)
    return jnp.tile(cos, (1, 2)), jnp.tile(sin, (1, 2))


def _pack_gate_weights(w_r, b_r, w_i, b_i):
    n_p = LRU_WIDTH // LANES
    zeros = jnp.zeros((LRU_BLOCK_DIM, LRU_BLOCK_DIM), F32)

    def pair(w, p):
        top = jnp.concatenate([w[2 * p], zeros], axis=1)
        bot = jnp.concatenate([zeros, w[2 * p + 1]], axis=1)
        return jnp.concatenate([top, bot], axis=0)

    ws, bs = [], []
    for p in range(n_p):
        ws.append(jnp.concatenate(
            [pair(w_r[0], p), pair(w_i[0], p), pair(w_r[1], p), pair(w_i[1], p)], axis=1))
        sl = slice(p * LANES, (p + 1) * LANES)
        bs.append(jnp.concatenate([b_r[0, sl], b_i[0, sl], b_r[1, sl], b_i[1, sl]])[None, :])
    return jnp.stack(ws).astype(BF16), jnp.stack(bs)


def _layer(x, ctx, mod3, norm1_g, norm2_g, w_in, q_norm_g, k_norm_g, attn_sink, conv_w, conv_b,
           lru_w_r, lru_b_r, lru_w_i, lru_b_i, lru_lambda, w_out, w_router, w_gate, w_up, w_down):
    bsz, n, _ = x.shape
    n_ctx = ctx.shape[1]
    x2 = x.reshape(bsz * n, D_MODEL)
    c2 = ctx.reshape(bsz * n_ctx, D_MODEL)
    g1 = norm1_g[None, :]
    g2 = norm2_g[None, :]
    qg = jnp.tile(q_norm_g, 2)[None, :]
    kg = jnp.tile(k_norm_g, 2)[None, :]
    cos, sin = _rope_tables(n)
    w_in_b = w_in.astype(BF16)
    w_ctx_b = w_in_b[:, ATTN_WIDTH:ATTN_WIDTH + 2 * KV_WIDTH + LRU_WIDTH]

    q_s, k2, v2, xr2, gg2 = _inproj_latent(x2, mod3, g1, w_in_b, qg, kg, cos, sin, bsz, n)
    kc2, vc2, xrc2 = _inproj_ctx(c2, mod3, g1, w_ctx_b, kg, bsz, n_ctx)

    o_s = _attn(attn_sink, q_s, k2.reshape(bsz, n, KV_WIDTH), v2.reshape(bsz, n, KV_WIDTH),
                kc2.reshape(bsz, n_ctx, KV_WIDTH), vc2.reshape(bsz, n_ctx, KV_WIDTH))

    wg_lru, bg_lru = _pack_gate_weights(lru_w_r, lru_b_r, lru_w_i, lru_b_i)
    rg3 = _lru(xr2.reshape(bsz, n, LRU_WIDTH), xrc2.reshape(bsz, n_ctx, LRU_WIDTH),
               gg2.reshape(bsz, n, LRU_WIDTH), conv_w, conv_b[None, :], wg_lru, bg_lru,
               lru_lambda)

    w_att = (w_out[:ATTN_WIDTH].reshape(N_KV_HEADS, Q_PER_KV, HEAD_DIM, D_MODEL)
             .transpose(1, 0, 2, 3).reshape(ATTN_WIDTH, D_MODEL).astype(BF16))
    w_rnn = w_out[ATTN_WIDTH:].astype(BF16)
    x1t, h2t, lg4 = _outproj(o_s, rg3.reshape(bsz * n, LRU_WIDTH), x2, mod3, w_att, w_rnn, g2,
                             w_router.T.astype(BF16), bsz, n)

    cap = CAPACITY_FACTOR * n // N_EXPERTS
    idx, gate = _topk(lg4, cap)
    g2t = mod3[:, N_MOD - 1, :].reshape(mod3.shape[0], FEATURE_CHUNKS, LANES)
    out = _moe(idx, gate, h2t, x1t, g2t, w_gate.astype(BF16), w_up.astype(BF16),
               w_down.astype(BF16), bsz, n)
    return out.reshape(bsz, n, D_MODEL)


def kernel(x, c, ctx, c_ctx, w_ada, b_ada, norm1_g, norm2_g, w_in, q_norm_g, k_norm_g, attn_sink, conv_w, conv_b, lru_w_r, lru_b_r, lru_w_i, lru_b_i, lru_lambda, w_out, w_router, w_gate, w_up, w_down):
    bsz = x.shape[0]
    depth = w_ada.shape[0]
    assert depth == 1, "context-stream update between layers is not implemented"
    pad_rows = -(bsz + 1) % SUBLANES
    cc = jnp.concatenate([c, jnp.zeros((pad_rows, D_MODEL), F32), c_ctx[None, :]], axis=0)
    for layer in range(depth):
        mod = _ada(cc, w_ada[layer], b_ada[layer][None, :])
        mod3 = mod.reshape(cc.shape[0], N_MOD, D_MODEL)
        x = _layer(x, ctx, mod3, norm1_g[layer], norm2_g[layer], w_in[layer], q_norm_g[layer],
                   k_norm_g[layer], attn_sink[layer], conv_w[layer], conv_b[layer],
                   lru_w_r[layer], lru_b_r[layer], lru_w_i[layer], lru_b_i[layer],
                   lru_lambda[layer], w_out[layer], w_router[layer], w_gate[layer],
                   w_up[layer], w_down[layer])
    return x
```

```python
import functools

import jax
import jax.numpy as jnp
from jax import lax
from jax.experimental import pallas as pl
from jax.experimental.pallas import tpu as pltpu

F32 = jnp.float32
BF16 = jnp.bfloat16

LANES = 128
SUBLANES = 8
VMEM_LIMIT_BYTES = 56 * 1024 * 1024

D_MODEL = 1024
HEAD_DIM = 64
N_Q_HEADS = 8
N_KV_HEADS = 2
Q_PER_KV = N_Q_HEADS // N_KV_HEADS
ATTN_WIDTH = N_Q_HEADS * HEAD_DIM
KV_WIDTH = N_KV_HEADS * HEAD_DIM
LRU_WIDTH = D_MODEL - ATTN_WIDTH
LRU_BLOCK_DIM = 64
IN_WIDTH = ATTN_WIDTH + 2 * KV_WIDTH + 2 * LRU_WIDTH
WINDOW = 128
BLOCK = 128
GRID_W = 64
ROPE_BASE = 10000.0
LRU_C = 8.0
N_EXPERTS = 16
CAPACITY_FACTOR = 2
EPS = 1e-6
NEG_INF = -1e30
N_MOD = 6
FEATURE_CHUNKS = D_MODEL // LANES
SCATTER_UNROLL = 16


def _dot(a, b):
    return jnp.dot(a, b, preferred_element_type=F32)


def _dot_nt(a, b):
    return lax.dot_general(a, b, (((1,), (1,)), ((), ())), preferred_element_type=F32)


def _dot_tn(a, b):
    return lax.dot_general(a, b, (((0,), (0,)), ((), ())), preferred_element_type=F32)


def _split_bf16(x, parts):
    out = []
    r = x
    for _ in range(parts):
        p = r.astype(BF16)
        out.append(p)
        r = r - p.astype(F32)
    return out


def _rmsnorm_rows(x, g):
    ms = jnp.mean(x * x, axis=-1, keepdims=True)
    return x * lax.rsqrt(ms + EPS) * g


def _ada_kernel(c_ref, w_ref, b_ref, o_ref):
    s = jax.nn.silu(c_ref[...])
    s_hi, s_lo = _split_bf16(s, 2)
    w_hi, w_lo = _split_bf16(w_ref[...], 2)
    o_ref[...] = _dot(s_hi, w_hi) + _dot(s_hi, w_lo) + _dot(s_lo, w_hi) + b_ref[...]


def _ada(cc, w_ada, b_ada):
    rows = cc.shape[0]
    width = w_ada.shape[1]
    return pl.pallas_call(
        _ada_kernel,
        out_shape=jax.ShapeDtypeStruct((rows, width), F32),
        grid=(width // D_MODEL,),
        in_specs=[
            pl.BlockSpec((rows, D_MODEL), lambda j: (0, 0)),
            pl.BlockSpec((D_MODEL, D_MODEL), lambda j: (0, j)),
            pl.BlockSpec((1, D_MODEL), lambda j: (0, j)),
        ],
        out_specs=pl.BlockSpec((rows, D_MODEL), lambda j: (0, j)),
        compiler_params=pltpu.CompilerParams(
            dimension_semantics=("arbitrary",), vmem_limit_bytes=VMEM_LIMIT_BYTES),
        name="ada",
    )(cc, w_ada, b_ada)


def _head_norm(p, g):
    lane = lax.broadcasted_iota(jnp.int32, p.shape, 1)
    low = lane < HEAD_DIM
    sq = p * p
    s_low = jnp.sum(jnp.where(low, sq, 0.0), axis=-1, keepdims=True)
    s_high = jnp.sum(jnp.where(low, 0.0, sq), axis=-1, keepdims=True)
    ms = jnp.where(low, s_low, s_high) * (1.0 / HEAD_DIM)
    return p * lax.rsqrt(ms + EPS) * g


def _rope(p, cos, sin_signed):
    lane = lax.broadcasted_iota(jnp.int32, p.shape, 1)
    partner = jnp.where((lane & 16) == 0,
                        pltpu.roll(p, LANES - 16, 1), pltpu.roll(p, 16, 1))
    return p * cos + partner * sin_signed


def _inproj_latent_kernel(x_ref, mod_ref, g1_ref, w_ref, qg_ref, kg_ref, cos_ref, sin_ref,
                          q_ref, k_ref, v_ref, xr_ref, gg_ref):
    x = x_ref[...]
    h = _rmsnorm_rows(x, g1_ref[...]) * (1.0 + mod_ref[0, 1:2, :]) + mod_ref[0, 0:1, :]
    y = _dot(h.astype(BF16), w_ref[...])
    cos = cos_ref[...]
    sin = sin_ref[...]
    pieces = []
    for j in range(ATTN_WIDTH // LANES):
        p = _head_norm(y[:, j * LANES:(j + 1) * LANES], qg_ref[...])
        pieces.append(_rope(p, cos, sin))
    lane = lax.broadcasted_iota(jnp.int32, pieces[0].shape, 1)
    low = lane < HEAD_DIM
    n_blocks = x.shape[0] // BLOCK
    for g in range(Q_PER_KV):
        src0 = pieces[g // 2]
        src1 = pieces[2 + g // 2]
        if g % 2 == 0:
            out = jnp.where(low, src0, pltpu.roll(src1, HEAD_DIM, 1))
        else:
            out = jnp.where(low, pltpu.roll(src0, HEAD_DIM, 1), src1)
        out = (out * HEAD_DIM ** -0.5).astype(BF16)
        for j in range(n_blocks):
            q_ref[0, j, g * BLOCK:(g + 1) * BLOCK, :] = out[j * BLOCK:(j + 1) * BLOCK, :]
    k = _head_norm(y[:, ATTN_WIDTH:ATTN_WIDTH + KV_WIDTH], kg_ref[...])
    k_ref[...] = _rope(k, cos, sin).astype(BF16)
    o = ATTN_WIDTH + KV_WIDTH
    v_ref[...] = y[:, o:o + KV_WIDTH].astype(BF16)
    o += KV_WIDTH
    xr_ref[...] = y[:, o:o + LRU_WIDTH]
    o += LRU_WIDTH
    gg_ref[...] = jax.nn.gelu(y[:, o:o + LRU_WIDTH])


def _inproj_ctx_kernel(x_ref, mod_ref, g1_ref, w_ref, kg_ref, k_ref, v_ref, xr_ref):
    x = x_ref[...]
    h = _rmsnorm_rows(x, g1_ref[...]) * (1.0 + mod_ref[0, 1:2, :]) + mod_ref[0, 0:1, :]
    y = _dot(h.astype(BF16), w_ref[...])
    k_ref[...] = _head_norm(y[:, 0:KV_WIDTH], kg_ref[...]).astype(BF16)
    v_ref[...] = y[:, KV_WIDTH:2 * KV_WIDTH].astype(BF16)
    xr_ref[...] = y[:, 2 * KV_WIDTH:2 * KV_WIDTH + LRU_WIDTH]


def _row_tile(n):
    return min(512, n)


def _inproj_latent(x2, mod3, g1, w_in, qg, kg, cos, sin, bsz, n):
    tm = _row_tile(n)
    tps = n // tm
    rows = bsz * n
    const = lambda i: (0, 0)
    return pl.pallas_call(
        _inproj_latent_kernel,
        out_shape=(
            jax.ShapeDtypeStruct((bsz, n // BLOCK, Q_PER_KV * BLOCK, LANES), BF16),
            jax.ShapeDtypeStruct((rows, KV_WIDTH), BF16),
            jax.ShapeDtypeStruct((rows, KV_WIDTH), BF16),
            jax.ShapeDtypeStruct((rows, LRU_WIDTH), F32),
            jax.ShapeDtypeStruct((rows, LRU_WIDTH), F32),
        ),
        grid=(rows // tm,),
        in_specs=[
            pl.BlockSpec((tm, D_MODEL), lambda i: (i, 0)),
            pl.BlockSpec((1, N_MOD, D_MODEL), lambda i: (i // tps, 0, 0)),
            pl.BlockSpec((1, D_MODEL), const),
            pl.BlockSpec((D_MODEL, IN_WIDTH), const),
            pl.BlockSpec((1, LANES), const),
            pl.BlockSpec((1, LANES), const),
            pl.BlockSpec((tm, LANES), lambda i: (i % tps, 0)),
            pl.BlockSpec((tm, LANES), lambda i: (i % tps, 0)),
        ],
        out_specs=(
            pl.BlockSpec((1, tm // BLOCK, Q_PER_KV * BLOCK, LANES),
                         lambda i: (i // tps, i % tps, 0, 0)),
            pl.BlockSpec((tm, KV_WIDTH), lambda i: (i, 0)),
            pl.BlockSpec((tm, KV_WIDTH), lambda i: (i, 0)),
            pl.BlockSpec((tm, LRU_WIDTH), lambda i: (i, 0)),
            pl.BlockSpec((tm, LRU_WIDTH), lambda i: (i, 0)),
        ),
        compiler_params=pltpu.CompilerParams(
            dimension_semantics=("arbitrary",), vmem_limit_bytes=VMEM_LIMIT_BYTES),
        name="inproj_latent",
    )(x2, mod3, g1, w_in, qg, kg, cos, sin)


def _inproj_ctx(c2, mod3, g1, w_ctx, kg, bsz, n_ctx):
    tm = _row_tile(n_ctx)
    rows = bsz * n_ctx
    ctx_row = mod3.shape[0] - 1
    width = w_ctx.shape[1]
    const = lambda i: (0, 0)
    return pl.pallas_call(
        _inproj_ctx_kernel,
        out_shape=(
            jax.ShapeDtypeStruct((rows, KV_WIDTH), BF16),
            jax.ShapeDtypeStruct((rows, KV_WIDTH), BF16),
            jax.ShapeDtypeStruct((rows, LRU_WIDTH), F32),
        ),
        grid=(rows // tm,),
        in_specs=[
            pl.BlockSpec((tm, D_MODEL), lambda i: (i, 0)),
            pl.BlockSpec((1, N_MOD, D_MODEL), lambda i: (ctx_row, 0, 0)),
            pl.BlockSpec((1, D_MODEL), const),
            pl.BlockSpec((D_MODEL, width), const),
            pl.BlockSpec((1, LANES), const),
        ],
        out_specs=(
            pl.BlockSpec((tm, KV_WIDTH), lambda i: (i, 0)),
            pl.BlockSpec((tm, KV_WIDTH), lambda i: (i, 0)),
            pl.BlockSpec((tm, LRU_WIDTH), lambda i: (i, 0)),
        ),
        compiler_params=pltpu.CompilerParams(
            dimension_semantics=("arbitrary",), vmem_limit_bytes=VMEM_LIMIT_BYTES),
        name="inproj_ctx",
    )(c2, mod3, g1, w_ctx, kg)


def _attn_kernel(sink_ref, q_ref, k_ref, v_ref, kc_ref, vc_ref, o_ref, *, n):
    i = pl.program_id(1)
    span = 3 * BLOCK
    start = pl.multiple_of(jnp.clip((i - 1) * BLOCK, 0, n - span), BLOCK)
    q = q_ref[0, 0]
    kw = k_ref[0, pl.ds(start, span), :]
    vw = v_ref[0, pl.ds(start, span), :]
    kc = kc_ref[0]
    vc = vc_ref[0]
    rows = q.shape[0]
    row = lax.broadcasted_iota(jnp.int32, (rows, span), 0)
    col = lax.broadcasted_iota(jnp.int32, (rows, span), 1)
    q_pos = i * BLOCK + (row & (BLOCK - 1))
    valid = jnp.abs(q_pos - (start + col)) <= WINDOW
    group = lax.broadcasted_iota(jnp.int32, (rows, 1), 0) // BLOCK
    acc = jnp.zeros((rows, LANES), F32)
    for h in range(N_KV_HEADS):
        def head_only(t):
            lane = lax.broadcasted_iota(jnp.int32, t.shape, 1)
            keep = (lane < HEAD_DIM) if h == 0 else (lane >= HEAD_DIM)
            return jnp.where(keep, t, jnp.zeros_like(t))
        s_loc = jnp.where(valid, _dot_nt(q, head_only(kw)), NEG_INF)
        s_ctx = _dot_nt(q, head_only(kc))
        sink = jnp.zeros((rows, 1), F32)
        for g in range(Q_PER_KV):
            sink = jnp.where(group == g, sink_ref[h * Q_PER_KV + g], sink)
        m = jnp.maximum(jnp.maximum(jnp.max(s_loc, axis=-1, keepdims=True),
                                    jnp.max(s_ctx, axis=-1, keepdims=True)), sink)
        p_loc = jnp.exp(s_loc - m)
        p_ctx = jnp.exp(s_ctx - m)
        denom = (jnp.sum(p_loc, axis=-1, keepdims=True)
                 + jnp.sum(p_ctx, axis=-1, keepdims=True) + jnp.exp(sink - m))
        o = _dot(p_loc.astype(BF16), head_only(vw)) + _dot(p_ctx.astype(BF16), head_only(vc))
        acc = acc + o / denom
    o_ref[0, 0] = acc.astype(BF16)


def _attn(sink, q_s, k3, v3, kc3, vc3):
    bsz, nb = q_s.shape[0], q_s.shape[1]
    n = k3.shape[1]
    n_ctx = kc3.shape[1]
    return pl.pallas_call(
        functools.partial(_attn_kernel, n=n),
        out_shape=jax.ShapeDtypeStruct(q_s.shape, BF16),
        grid=(bsz, nb),
        in_specs=[
            pl.BlockSpec(memory_space=pltpu.SMEM),
            pl.BlockSpec((1, 1, Q_PER_KV * BLOCK, LANES), lambda b, i: (b, i, 0, 0)),
            pl.BlockSpec((1, n, KV_WIDTH), lambda b, i: (b, 0, 0)),
            pl.BlockSpec((1, n, KV_WIDTH), lambda b, i: (b, 0, 0)),
            pl.BlockSpec((1, n_ctx, KV_WIDTH), lambda b, i: (b, 0, 0)),
            pl.BlockSpec((1, n_ctx, KV_WIDTH), lambda b, i: (b, 0, 0)),
        ],
        out_specs=pl.BlockSpec((1, 1, Q_PER_KV * BLOCK, LANES), lambda b, i: (b, i, 0, 0)),
        compiler_params=pltpu.CompilerParams(
            dimension_semantics=("arbitrary", "arbitrary"), vmem_limit_bytes=VMEM_LIMIT_BYTES),
        name="attn",
    )(sink, q_s, k3, v3, kc3, vc3)


def _softplus(z):
    return jnp.maximum(z, 0.0) + jnp.log1p(jnp.exp(-jnp.abs(z)))


def _sigmoid(z):
    return 0.5 * jnp.tanh(0.5 * z) + 0.5


LRU_MAIN_SEG = 36


def _lru_plan(length):
    main, rest = divmod(length // SUBLANES, LRU_MAIN_SEG)
    if rest == 0:
        tails = ()
    elif rest % SUBLANES == 0 and rest > SUBLANES:
        tails = (rest - 5, 5)
    else:
        tails = (rest,)
    assert length % SUBLANES == 0 and all(t >= 2 for t in tails)
    return main, tails


def _lru_kernel(xr_ref, xrc_ref, gg_ref, cw_ref, cb_ref, wg_ref, bg_ref, lam_ref, o_ref,
                xc_s, xcc_s, hf_s, hb_s, *, n, n_ctx):
    cw = cw_ref[...]
    cb = cb_ref[...]
    sub = lax.broadcasted_iota(jnp.int32, (SUBLANES, LANES), 0)
    main_rows = SUBLANES * LRU_MAIN_SEG
    n_main, tails = _lru_plan(n)
    c_main, c_tails = _lru_plan(n_ctx)

    def static_tiles(n_main_tiles, tail_segs, with_main):
        tiles = [(k * main_rows, LRU_MAIN_SEG) for k in range(n_main_tiles)] if with_main else []
        t0 = n_main_tiles * main_rows
        for seg in tail_segs:
            tiles.append((t0, seg))
            t0 += SUBLANES * seg
        return tiles

    def aligned(v):
        return v if isinstance(v, int) else pl.multiple_of(v, SUBLANES)

    def conv_tile(src_ref, dst_ref, t0, seg, length):
        x = [src_ref[0, pl.ds(t0 + j, SUBLANES, stride=seg), :] for j in range(seg)]
        lo = aligned(jnp.maximum(t0 - SUBLANES, 0))
        hi = aligned(jnp.minimum(t0 + SUBLANES * seg, length - SUBLANES))
        prev = jnp.where(t0 > 0, src_ref[0, pl.ds(lo, SUBLANES), :], 0.0)
        nxt = jnp.where(t0 + SUBLANES * seg < length, src_ref[0, pl.ds(hi, SUBLANES), :], 0.0)

        def from_prev_segment(v, row):
            return jnp.where(sub == 0, row, pltpu.roll(v, 1, 0))

        def from_next_segment(v, row):
            return jnp.where(sub == SUBLANES - 1, row, pltpu.roll(v, SUBLANES - 1, 0))

        m1 = [from_prev_segment(x[seg - 1], prev[7:8])] + x[:seg - 1]
        m2 = [from_prev_segment(x[seg - 2], prev[6:7])] + m1[:seg - 1]
        p1 = x[1:] + [from_next_segment(x[0], nxt[0:1])]
        for j in range(seg):
            y = cb + cw[0:1] * m2[j]
            y = y + cw[1:2] * m1[j]
            y = y + cw[2:3] * x[j]
            y = y + cw[3:4] * p1[j]
            dst_ref[pl.ds(t0 + j * SUBLANES, SUBLANES), :] = y

    decay = [_softplus(-lam_ref[d:d + 1, :]) for d in range(2)]

    def coeffs(xc, d):
        w = wg_ref[0, :, 2 * LANES * d:2 * LANES * (d + 1)]
        g = _dot(xc.astype(BF16), w) + bg_ref[0, :, 2 * LANES * d:2 * LANES * (d + 1)]
        r = _sigmoid(g[:, :LANES])
        i = _sigmoid(g[:, LANES:])
        log_a = -LRU_C * r * decay[d]
        a = jnp.exp(log_a)
        mult = jnp.sqrt(-jnp.tanh(log_a) * (a * a + 1.0))
        return a, mult * i * xc

    def scan_tile(a, b, h_in, seg, reverse):
        order = list(range(seg - 1, -1, -1) if reverse else range(seg))
        local = [None] * seg
        prod = [None] * seg
        h = p = None
        for j in order:
            aj = a[j * SUBLANES:(j + 1) * SUBLANES, :]
            bj = b[j * SUBLANES:(j + 1) * SUBLANES, :]
            h = bj if h is None else aj * h + bj
            p = aj if p is None else aj * p
            local[j], prod[j] = h, p
        end = order[-1]
        carry = jnp.zeros((SUBLANES, LANES), F32)
        c = h_in
        for s in (range(SUBLANES - 1, -1, -1) if reverse else range(SUBLANES)):
            carry = jnp.where(sub == s, c, carry)
            c = prod[end][s:s + 1, :] * c + local[end][s:s + 1, :]
        return [local[j] + prod[j] * carry for j in range(seg)], c

    def run_tile(xc_ref, t0, seg, d, h, emit=None):
        a, b = coeffs(xc_ref[pl.ds(t0, SUBLANES * seg), :], d)
        states, h = scan_tile(a, b, h, seg, d == 1)
        if emit is not None:
            emit(t0, seg, states)
        return h

    def keep(dst_ref):
        def emit(t0, seg, states):
            for j in range(seg):
                dst_ref[pl.ds(t0 + j * SUBLANES, SUBLANES), :] = states[j]
        return emit

    def write_output(other_ref):
        def emit(t0, seg, states):
            for j in range(seg):
                rows = pl.ds(t0 + j, SUBLANES, stride=seg)
                other = other_ref[pl.ds(t0 + j * SUBLANES, SUBLANES), :]
                o_ref[0, rows, :] = (other + states[j]) * gg_ref[0, rows, :]
        return emit

    def main_t0(k):
        return pl.multiple_of(k * main_rows, SUBLANES)

    for t0, seg in static_tiles(c_main, c_tails, True):
        conv_tile(xrc_ref, xcc_s, t0, seg, n_ctx)

    def conv_body(k, carry):
        conv_tile(xr_ref, xc_s, main_t0(k), LRU_MAIN_SEG, n)
        return carry
    lax.fori_loop(0, n_main, conv_body, 0)
    for t0, seg in static_tiles(n_main, tails, False):
        conv_tile(xr_ref, xc_s, t0, seg, n)

    hf = jnp.zeros((1, LANES), F32)
    hb = jnp.zeros((1, LANES), F32)
    for t0, seg in static_tiles(c_main, c_tails, True):
        hf = run_tile(xcc_s, t0, seg, 0, hf)
    for t0, seg in reversed(static_tiles(c_main, c_tails, True)):
        hb = run_tile(xcc_s, t0, seg, 1, hb)
    for t0, seg in reversed(static_tiles(n_main, tails, False)):
        hb = run_tile(xc_s, t0, seg, 1, hb, keep(hb_s))

    def pair_body(first_visit):
        def body(k, carry):
            hf, hb = carry
            fwd_emit = keep(hf_s) if first_visit else write_output(hb_s)
            rev_emit = keep(hb_s) if first_visit else write_output(hf_s)
            hf = run_tile(xc_s, main_t0(k), LRU_MAIN_SEG, 0, hf, fwd_emit)
            hb = run_tile(xc_s, main_t0(n_main - 1 - k), LRU_MAIN_SEG, 1, hb, rev_emit)
            return hf, hb
        return body

    half = n_main // 2
    hf, hb = lax.fori_loop(0, half, pair_body(True), (hf, hb))
    if n_main % 2:
        hf = run_tile(xc_s, half * main_rows, LRU_MAIN_SEG, 0, hf, keep(hf_s))
        hb = run_tile(xc_s, half * main_rows, LRU_MAIN_SEG, 1, hb, write_output(hf_s))
    hf, hb = lax.fori_loop(n_main - half, n_main, pair_body(False), (hf, hb))
    for t0, seg in static_tiles(n_main, tails, False):
        hf = run_tile(xc_s, t0, seg, 0, hf, write_output(hb_s))


def _lru(xr3, xrc3, gg3, conv_w, conv_b, wg, bg, lam):
    bsz, n, _ = xr3.shape
    n_ctx = xrc3.shape[1]
    n_p = LRU_WIDTH // LANES
    return pl.pallas_call(
        functools.partial(_lru_kernel, n=n, n_ctx=n_ctx),
        out_shape=jax.ShapeDtypeStruct((bsz, n, LRU_WIDTH), F32),
        grid=(bsz, n_p),
        in_specs=[
            pl.BlockSpec((1, n, LANES), lambda b, p: (b, 0, p)),
            pl.BlockSpec((1, n_ctx, LANES), lambda b, p: (b, 0, p)),
            pl.BlockSpec((1, n, LANES), lambda b, p: (b, 0, p)),
            pl.BlockSpec((conv_w.shape[0], LANES), lambda b, p: (0, p)),
            pl.BlockSpec((1, LANES), lambda b, p: (0, p)),
            pl.BlockSpec((1, LANES, 4 * LANES), lambda b, p: (p, 0, 0)),
            pl.BlockSpec((1, 1, 4 * LANES), lambda b, p: (p, 0, 0)),
            pl.BlockSpec((2, LANES), lambda b, p: (0, p)),
        ],
        out_specs=pl.BlockSpec((1, n, LANES), lambda b, p: (b, 0, p)),
        scratch_shapes=[
            pltpu.VMEM((n, LANES), F32),
            pltpu.VMEM((n_ctx, LANES), F32),
            pltpu.VMEM((n, LANES), F32),
            pltpu.VMEM((n, LANES), F32),
        ],
        compiler_params=pltpu.CompilerParams(
            dimension_semantics=("arbitrary", "arbitrary"), vmem_limit_bytes=VMEM_LIMIT_BYTES),
        name="lru",
    )(xr3, xrc3, gg3, conv_w, conv_b, wg, bg, lam)


def _outproj_kernel(o_ref, rg_ref, x_ref, mod_ref, watt_ref, wrnn_ref, g2_ref, wr_ref,
                    x1t_ref, h2t_ref, lg_ref):
    tm = x_ref.shape[0]
    n_blocks = tm // BLOCK
    att = jnp.concatenate(
        [jnp.concatenate([o_ref[0, j, g * BLOCK:(g + 1) * BLOCK, :] for g in range(Q_PER_KV)],
                         axis=1) for j in range(n_blocks)], axis=0)
    y = _dot(att, watt_ref[...]) + _dot(rg_ref[...].astype(BF16), wrnn_ref[...])
    x1 = x_ref[...] + mod_ref[0, 2:3, :] * y
    h2 = _rmsnorm_rows(x1, g2_ref[...]) * (1.0 + mod_ref[0, 4:5, :]) + mod_ref[0, 3:4, :]
    for s in range(FEATURE_CHUNKS):
        rows = pl.ds(s, tm, stride=FEATURE_CHUNKS)
        x1t_ref[rows, :] = x1[:, s * LANES:(s + 1) * LANES]
        h2t_ref[rows, :] = h2[:, s * LANES:(s + 1) * LANES]
    lt = _dot_nt(wr_ref[...], h2.astype(BF16))
    for j in range(n_blocks):
        lg_ref[0, j] = lt[:, j * LANES:(j + 1) * LANES]


def _outproj(o_s, rg2, x2, mod3, w_att, w_rnn, g2, wr_t, bsz, n):
    tm = _row_tile(n)
    tps = n // tm
    rows = bsz * n
    const = lambda i: (0, 0)
    return pl.pallas_call(
        _outproj_kernel,
        out_shape=(
            jax.ShapeDtypeStruct((rows * FEATURE_CHUNKS, LANES), F32),
            jax.ShapeDtypeStruct((rows * FEATURE_CHUNKS, LANES), F32),
            jax.ShapeDtypeStruct((bsz, n // LANES, N_EXPERTS, LANES), F32),
        ),
        grid=(rows // tm,),
        in_specs=[
            pl.BlockSpec((1, tm // BLOCK, Q_PER_KV * BLOCK, LANES),
                         lambda i: (i // tps, i % tps, 0, 0)),
            pl.BlockSpec((tm, LRU_WIDTH), lambda i: (i, 0)),
            pl.BlockSpec((tm, D_MODEL), lambda i: (i, 0)),
            pl.BlockSpec((1, N_MOD, D_MODEL), lambda i: (i // tps, 0, 0)),
            pl.BlockSpec((ATTN_WIDTH, D_MODEL), const),
            pl.BlockSpec((LRU_WIDTH, D_MODEL), const),
            pl.BlockSpec((1, D_MODEL), const),
            pl.BlockSpec((N_EXPERTS, D_MODEL), const),
        ],
        out_specs=(
            pl.BlockSpec((tm * FEATURE_CHUNKS, LANES), lambda i: (i, 0)),
            pl.BlockSpec((tm * FEATURE_CHUNKS, LANES), lambda i: (i, 0)),
            pl.BlockSpec((1, tm // LANES, N_EXPERTS, LANES), lambda i: (i // tps, i % tps, 0, 0)),
        ),
        compiler_params=pltpu.CompilerParams(
            dimension_semantics=("arbitrary",), vmem_limit_bytes=VMEM_LIMIT_BYTES),
        name="outproj",
    )(o_s, rg2, x2, mod3, w_att, w_rnn, g2, wr_t)


def _topk_kernel(lg_ref, idx_ref, gate_ref, aff_s, cum_s, *, cap):
    lg = lg_ref[0]
    n_chunks = lg.shape[0]
    m = jnp.max(lg, axis=1, keepdims=True)
    ex = jnp.exp(lg - m)
    aff = ex / jnp.sum(ex, axis=1, keepdims=True)
    aff_s[...] = aff

    def count(mask):
        c = jnp.sum(jnp.where(mask, 1.0, 0.0), axis=0, keepdims=True)
        return jnp.sum(c, axis=2, keepdims=True)

    def bit_body(it, thr):
        cand = thr | jnp.left_shift(jnp.int32(1), 30 - it)
        return jnp.where(count(aff >= pltpu.bitcast(cand, F32)) >= cap, cand, thr)
    thr = lax.fori_loop(0, 31, bit_body, jnp.zeros((1, N_EXPERTS, 1), jnp.int32))
    thr = pltpu.bitcast(thr, F32)

    tri = (lax.broadcasted_iota(jnp.int32, (LANES, LANES), 0)
           <= lax.broadcasted_iota(jnp.int32, (LANES, LANES), 1))
    tri = jnp.where(tri, 1.0, 0.0).astype(BF16)

    def chunk_cumsum(mask):
        flat = jnp.where(mask, 1.0, 0.0).astype(BF16).reshape(n_chunks * N_EXPERTS, LANES)
        inc = _dot(flat, tri).reshape(n_chunks, N_EXPERTS, LANES)
        run = jnp.zeros((N_EXPERTS, 1), F32)
        before = []
        for c in range(n_chunks):
            before.append(run)
            run = run + inc[c][:, LANES - 1:LANES]
        return inc, jnp.stack(before, axis=0)

    above = aff > thr
    tied = aff == thr
    need = cap - count(above)
    tie_inc, tie_before = chunk_cumsum(tied)
    tie_rank = tie_before + tie_inc - jnp.where(tied, 1.0, 0.0)
    sel = above | (tied & (tie_rank < need))
    sel_inc, _ = chunk_cumsum(sel)
    cum_s[...] = sel_inc

    lower = (lax.broadcasted_iota(jnp.int32, (n_chunks, n_chunks), 1)
             <= lax.broadcasted_iota(jnp.int32, (n_chunks, n_chunks), 0))
    lower = jnp.where(lower, 1.0, 0.0).astype(BF16)
    slot = lax.broadcasted_iota(jnp.int32, (1, cap), 1).astype(F32)
    chunk_id = lax.broadcasted_iota(jnp.int32, (n_chunks, cap), 0).astype(F32)
    lane_id = lax.broadcasted_iota(jnp.int32, (LANES, cap), 0).astype(F32)
    for e in range(N_EXPERTS):
        cin = cum_s[:, e, :]
        tot = jnp.broadcast_to(cin[:, LANES - 1:LANES], cin.shape)
        cc_inc = _dot(lower, tot.astype(BF16))[:, 0:1]
        cc_exc = cc_inc - tot[:, 0:1]
        kc = jnp.sum(jnp.where(cc_inc <= slot, 1.0, 0.0), axis=0, keepdims=True)
        onehot = chunk_id == kc
        onehot_b = jnp.where(onehot, 1.0, 0.0).astype(BF16)
        counts_t = _dot_tn(cin.astype(BF16), onehot_b)
        local = slot - jnp.sum(jnp.where(onehot, cc_exc, 0.0), axis=0, keepdims=True)
        il = jnp.sum(jnp.where(counts_t <= local, 1.0, 0.0), axis=0, keepdims=True)
        idx_ref[0, e:e + 1, :] = (kc * LANES + il).astype(jnp.int32)
        aff_t = jnp.zeros((LANES, cap), F32)
        for part in _split_bf16(aff_s[:, e, :], 3):
            aff_t = aff_t + _dot_tn(part, onehot_b)
        gate_ref[0, e:e + 1, :] = jnp.sum(jnp.where(lane_id == il, aff_t, 0.0),
                                          axis=0, keepdims=True)


def _topk(lg4, cap):
    bsz, n_chunks = lg4.shape[0], lg4.shape[1]
    return pl.pallas_call(
        functools.partial(_topk_kernel, cap=cap),
        out_shape=(
            jax.ShapeDtypeStruct((bsz, N_EXPERTS, cap), jnp.int32),
            jax.ShapeDtypeStruct((bsz, N_EXPERTS, cap), F32),
        ),
        grid=(bsz,),
        in_specs=[pl.BlockSpec((1, n_chunks, N_EXPERTS, LANES), lambda b: (b, 0, 0, 0))],
        out_specs=(
            pl.BlockSpec((1, N_EXPERTS, cap), lambda b: (b, 0, 0)),
            pl.BlockSpec((1, N_EXPERTS, cap), lambda b: (b, 0, 0)),
        ),
        scratch_shapes=[
            pltpu.VMEM((n_chunks, N_EXPERTS, LANES), F32),
            pltpu.VMEM((n_chunks, N_EXPERTS, LANES), F32),
        ],
        compiler_params=pltpu.CompilerParams(
            dimension_semantics=("arbitrary",), vmem_limit_bytes=VMEM_LIMIT_BYTES),
        name="topk",
    )(lg4)


def _moe_kernel(idx_ref, gate_ref, h2t_hbm, x1t_hbm, g2t_ref, wg_ref, wu_ref, wd_ref, out_hbm,
                h2_s, acc_s, xs_s, ys_s, stage_s, sems, *, n, cap, sub):
    b = pl.program_id(0)
    e = pl.program_id(1)
    n_b = pl.num_programs(0)
    n_e = pl.num_programs(1)
    tok_rows = n * FEATURE_CHUNKS
    n_parts = cap // sub

    def load_h2(sample):
        return pltpu.make_async_copy(
            h2t_hbm.at[pl.ds(sample * tok_rows, tok_rows), :], h2_s, sems.at[0])

    def load_x1(sample):
        return pltpu.make_async_copy(
            x1t_hbm.at[pl.ds(sample * tok_rows, tok_rows), :], acc_s, sems.at[1])

    g2t = g2t_ref[0]

    def token_tile(row):
        return pl.ds(pl.multiple_of(row * FEATURE_CHUNKS, FEATURE_CHUNKS), FEATURE_CHUNKS)

    def gather(part):
        for r in range(sub):
            t = idx_ref[0, 0, 0, part * sub + r]
            xs_s[part, r * FEATURE_CHUNKS:(r + 1) * FEATURE_CHUNKS, :] = h2_s[token_tile(t), :]

    def expert(part):
        x = jnp.concatenate(
            [xs_s[part, pl.ds(s, sub, stride=FEATURE_CHUNKS), :] for s in range(FEATURE_CHUNKS)],
            axis=1).astype(BF16)
        hid = (jax.nn.silu(_dot(x, wg_ref[0])) * _dot(x, wu_ref[0])).astype(BF16)
        y = _dot(hid, wd_ref[0])
        for s in range(FEATURE_CHUNKS):
            ys_s[part, pl.ds(s, sub, stride=FEATURE_CHUNKS), :] = y[:, s * LANES:(s + 1) * LANES]

    def scatter(part):
        for r0 in range(0, sub, SCATTER_UNROLL):
            updates = []
            for r in range(r0, r0 + SCATTER_UNROLL):
                dst = token_tile(idx_ref[0, 0, 0, part * sub + r])
                gv = gate_ref[0, 0, 0, part * sub + r]
                row = ys_s[part, r * FEATURE_CHUNKS:(r + 1) * FEATURE_CHUNKS, :]
                updates.append((dst, acc_s[dst, :] + g2t * (row * gv)))
            for dst, val in updates:
                acc_s[dst, :] = val

    @pl.when((e == 0) & (b == 0))
    def _():
        load_h2(b).start()
        load_x1(b).start()

    @pl.when(e == 0)
    def _():
        load_h2(b).wait()

    for part in range(n_parts):
        gather(part)
    expert(0)

    @pl.when(e == 0)
    def _():
        load_x1(b).wait()

    for part in range(1, n_parts):
        expert(part)
    for part in range(n_parts):
        scatter(part)

    @pl.when(e == n_e - 1)
    def _():
        n_out = n // sub

        @pl.when(b + 1 < n_b)
        def _():
            load_h2(b + 1).start()

        def store(c, slot):
            return pltpu.make_async_copy(
                stage_s.at[slot], out_hbm.at[pl.ds(b * n + c * sub, sub), :], sems.at[2 + slot])

        def out_body(c, carry):
            slot = c % 2

            @pl.when(c >= 2)
            def _():
                store(c - 2, slot).wait()
            row0 = pl.multiple_of(c * (sub * FEATURE_CHUNKS), sub * FEATURE_CHUNKS)
            for s in range(FEATURE_CHUNKS):
                stage_s[slot, :, s * LANES:(s + 1) * LANES] = (
                    acc_s[pl.ds(row0 + s, sub, stride=FEATURE_CHUNKS), :])
            store(c, slot).start()
            return carry
        lax.fori_loop(0, n_out, out_body, 0)

        @pl.when(b + 1 < n_b)
        def _():
            load_x1(b + 1).start()
        for c in range(n_out - 2, n_out):
            store(c, c % 2).wait()


def _moe(idx, gate, h2t, x1t, g2t, wg, wu, wd, bsz, n):
    cap = idx.shape[-1]
    sub = min(256, cap)
    assert n // sub >= 2 and cap % sub == 0 and sub % SCATTER_UNROLL == 0
    tok_rows = n * FEATURE_CHUNKS
    weight_spec = pl.BlockSpec((1, D_MODEL, D_MODEL), lambda b, e: (e, 0, 0))
    smem_spec = pl.BlockSpec((1, 1, 1, cap), lambda b, e: (b, e, 0, 0),
                             memory_space=pltpu.SMEM)
    return pl.pallas_call(
        functools.partial(_moe_kernel, n=n, cap=cap, sub=sub),
        out_shape=jax.ShapeDtypeStruct((bsz * n, D_MODEL), F32),
        grid=(bsz, N_EXPERTS),
        in_specs=[
            smem_spec,
            smem_spec,
            pl.BlockSpec(memory_space=pl.ANY),
            pl.BlockSpec(memory_space=pl.ANY),
            pl.BlockSpec((1, FEATURE_CHUNKS, LANES), lambda b, e: (b, 0, 0)),
            weight_spec,
            weight_spec,
            weight_spec,
        ],
        out_specs=pl.BlockSpec(memory_space=pl.ANY),
        scratch_shapes=[
            pltpu.VMEM((tok_rows, LANES), F32),
            pltpu.VMEM((tok_rows, LANES), F32),
            pltpu.VMEM((cap // sub, sub * FEATURE_CHUNKS, LANES), F32),
            pltpu.VMEM((cap // sub, sub * FEATURE_CHUNKS, LANES), F32),
            pltpu.VMEM((2, sub, D_MODEL), F32),
            pltpu.SemaphoreType.DMA((4,)),
        ],
        compiler_params=pltpu.CompilerParams(
            dimension_semantics=("arbitrary", "arbitrary"), vmem_limit_bytes=VMEM_LIMIT_BYTES),
        name="moe",
    )(idx[:, :, None, :], gate[:, :, None, :], h2t, x1t, g2t, wg, wu, wd)


def _rope_tables(n):
    t = jnp.arange(n)
    row = (t // GRID_W).astype(F32)
    col = (t % GRID_W).astype(F32)
    n_freq = HEAD_DIM // 4
    inv_freq = ROPE_BASE ** (-jnp.arange(n_freq, dtype=F32) / n_freq)
    ang_r = row[:, None] * inv_freq
    ang_c = col[:, None] * inv_freq
    cos = jnp.concatenate([jnp.cos(ang_r)] * 2 + [jnp.cos(ang_c)] * 2, axis=1)
    sin = jnp.concatenate([-jnp.sin(ang_r), jnp.sin(ang_r), -jnp.sin(ang_c), jnp.sin(ang_c)],
                          axis=1)
    return jnp.tile(cos, (1, 2)), jnp.tile(sin, (1, 2))


def _pack_gate_weights(w_r, b_r, w_i, b_i):
    n_p = LRU_WIDTH // LANES
    zeros = jnp.zeros((LRU_BLOCK_DIM, LRU_BLOCK_DIM), F32)

    def pair(w, p):
        top = jnp.concatenate([w[2 * p], zeros], axis=1)
        bot = jnp.concatenate([zeros, w[2 * p + 1]], axis=1)
        return jnp.concatenate([top, bot], axis=0)

    ws, bs = [], []
    for p in range(n_p):
        ws.append(jnp.concatenate(
            [pair(w_r[0], p), pair(w_i[0], p), pair(w_r[1], p), pair(w_i[1], p)], axis=1))
        sl = slice(p * LANES, (p + 1) * LANES)
        bs.append(jnp.concatenate([b_r[0, sl], b_i[0, sl], b_r[1, sl], b_i[1, sl]])[None, :])
    return jnp.stack(ws).astype(BF16), jnp.stack(bs)


def _layer(x, ctx, mod3, norm1_g, norm2_g, w_in, q_norm_g, k_norm_g, attn_sink, conv_w, conv_b,
           lru_w_r, lru_b_r, lru_w_i, lru_b_i, lru_lambda, w_out, w_router, w_gate, w_up, w_down):
    bsz, n, _ = x.shape
    n_ctx = ctx.shape[1]
    x2 = x.reshape(bsz * n, D_MODEL)
    c2 = ctx.reshape(bsz * n_ctx, D_MODEL)
    g1 = norm1_g[None, :]
    g2 = norm2_g[None, :]
    qg = jnp.tile(q_norm_g, 2)[None, :]
    kg = jnp.tile(k_norm_g, 2)[None, :]
    cos, sin = _rope_tables(n)
    w_in_b = w_in.astype(BF16)
    w_ctx_b = w_in_b[:, ATTN_WIDTH:ATTN_WIDTH + 2 * KV_WIDTH + LRU_WIDTH]

    q_s, k2, v2, xr2, gg2 = _inproj_latent(x2, mod3, g1, w_in_b, qg, kg, cos, sin, bsz, n)
    kc2, vc2, xrc2 = _inproj_ctx(c2, mod3, g1, w_ctx_b, kg, bsz, n_ctx)

    o_s = _attn(attn_sink, q_s, k2.reshape(bsz, n, KV_WIDTH), v2.reshape(bsz, n, KV_WIDTH),
                kc2.reshape(bsz, n_ctx, KV_WIDTH), vc2.reshape(bsz, n_ctx, KV_WIDTH))

    wg_lru, bg_lru = _pack_gate_weights(lru_w_r, lru_b_r, lru_w_i, lru_b_i)
    rg3 = _lru(xr2.reshape(bsz, n, LRU_WIDTH), xrc2.reshape(bsz, n_ctx, LRU_WIDTH),
               gg2.reshape(bsz, n, LRU_WIDTH), conv_w, conv_b[None, :], wg_lru, bg_lru,
               lru_lambda)

    w_att = (w_out[:ATTN_WIDTH].reshape(N_KV_HEADS, Q_PER_KV, HEAD_DIM, D_MODEL)
             .transpose(1, 0, 2, 3).reshape(ATTN_WIDTH, D_MODEL).astype(BF16))
    w_rnn = w_out[ATTN_WIDTH:].astype(BF16)
    x1t, h2t, lg4 = _outproj(o_s, rg3.reshape(bsz * n, LRU_WIDTH), x2, mod3, w_att, w_rnn, g2,
                             w_router.T.astype(BF16), bsz, n)

    cap = CAPACITY_FACTOR * n // N_EXPERTS
    idx, gate = _topk(lg4, cap)
    g2t = mod3[:, N_MOD - 1, :].reshape(mod3.shape[0], FEATURE_CHUNKS, LANES)
    out = _moe(idx, gate, h2t, x1t, g2t, w_gate.astype(BF16), w_up.astype(BF16),
               w_down.astype(BF16), bsz, n)
    return out.reshape(bsz, n, D_MODEL)


def kernel(x, c, ctx, c_ctx, w_ada, b_ada, norm1_g, norm2_g, w_in, q_norm_g, k_norm_g, attn_sink, conv_w, conv_b, lru_w_r, lru_b_r, lru_w_i, lru_b_i, lru_lambda, w_out, w_router, w_gate, w_up, w_down):
    bsz = x.shape[0]
    depth = w_ada.shape[0]
    assert depth == 1, "context-stream update between layers is not implemented"
    pad_rows = -(bsz + 1) % SUBLANES
    cc = jnp.concatenate([c, jnp.zeros((pad_rows, D_MODEL), F32), c_ctx[None, :]], axis=0)
    for layer in range(depth):
        mod = _ada(cc, w_ada[layer], b_ada[layer][None, :])
        mod3 = mod.reshape(cc.shape[0], N_MOD, D_MODEL)
        x = _layer(x, ctx, mod3, norm1_g[layer], norm2_g[layer], w_in[layer], q_norm_g[layer],
                   k_norm_g[layer], attn_sink[layer], conv_w[layer], conv_b[layer],
                   lru_w_r[layer], lru_b_r[layer], lru_w_i[layer], lru_b_i[layer],
                   lru_lambda[layer], w_out[layer], w_router[layer], w_gate[layer],
                   w_up[layer], w_down[layer])
    return x
```

```python
import functools

import jax
import jax.numpy as jnp
from jax import lax
from jax.experimental import pallas as pl
from jax.experimental.pallas import tpu as pltpu

F32 = jnp.float32
BF16 = jnp.bfloat16

LANES = 128
SUBLANES = 8
VMEM_LIMIT_BYTES = 56 * 1024 * 1024

D_MODEL = 1024
HEAD_DIM = 64
N_Q_HEADS = 8
N_KV_HEADS = 2
Q_PER_KV = N_Q_HEADS // N_KV_HEADS
ATTN_WIDTH = N_Q_HEADS * HEAD_DIM
KV_WIDTH = N_KV_HEADS * HEAD_DIM
LRU_WIDTH = D_MODEL - ATTN_WIDTH
LRU_BLOCK_DIM = 64
IN_WIDTH = ATTN_WIDTH + 2 * KV_WIDTH + 2 * LRU_WIDTH
WINDOW = 128
BLOCK = 128
GRID_W = 64
ROPE_BASE = 10000.0
LRU_C = 8.0
N_EXPERTS = 16
CAPACITY_FACTOR = 2
EPS = 1e-6
NEG_INF = -1e30
LOG2_E = 1.4426950408889634
SCORE_SCALE = HEAD_DIM ** -0.5 * LOG2_E
N_MOD = 6
FEATURE_CHUNKS = D_MODEL // LANES
SCATTER_UNROLL = 16


def _dot(a, b):
    return jnp.dot(a, b, preferred_element_type=F32)


def _dot_nt(a, b):
    return lax.dot_general(a, b, (((1,), (1,)), ((), ())), preferred_element_type=F32)


def _dot_tn(a, b):
    return lax.dot_general(a, b, (((0,), (0,)), ((), ())), preferred_element_type=F32)


def _split_bf16(x, parts):
    out = []
    r = x
    for _ in range(parts):
        p = r.astype(BF16)
        out.append(p)
        r = r - p.astype(F32)
    return out


def _rmsnorm_rows(x, g):
    ms = jnp.mean(x * x, axis=-1, keepdims=True)
    return x * lax.rsqrt(ms + EPS) * g


def _ada_kernel(c_ref, w_ref, b_ref, o_ref):
    s = jax.nn.silu(c_ref[...])
    s_hi, s_lo = _split_bf16(s, 2)
    w_hi, w_lo = _split_bf16(w_ref[...], 2)
    o_ref[...] = _dot(s_hi, w_hi) + _dot(s_hi, w_lo) + _dot(s_lo, w_hi) + b_ref[...]


def _ada(cc, w_ada, b_ada):
    rows = cc.shape[0]
    width = w_ada.shape[1]
    return pl.pallas_call(
        _ada_kernel,
        out_shape=jax.ShapeDtypeStruct((rows, width), F32),
        grid=(width // D_MODEL,),
        in_specs=[
            pl.BlockSpec((rows, D_MODEL), lambda j: (0, 0)),
            pl.BlockSpec((D_MODEL, D_MODEL), lambda j: (0, j)),
            pl.BlockSpec((1, D_MODEL), lambda j: (0, j)),
        ],
        out_specs=pl.BlockSpec((rows, D_MODEL), lambda j: (0, j)),
        compiler_params=pltpu.CompilerParams(
            dimension_semantics=("arbitrary",), vmem_limit_bytes=VMEM_LIMIT_BYTES),
        name="ada",
    )(cc, w_ada, b_ada)


def _head_norm(p, g):
    lane = lax.broadcasted_iota(jnp.int32, p.shape, 1)
    low = lane < HEAD_DIM
    sq = p * p
    s_low = jnp.sum(jnp.where(low, sq, 0.0), axis=-1, keepdims=True)
    s_high = jnp.sum(jnp.where(low, 0.0, sq), axis=-1, keepdims=True)
    ms = jnp.where(low, s_low, s_high) * (1.0 / HEAD_DIM)
    return p * lax.rsqrt(ms + EPS) * g


def _rope(p, cos, sin_signed):
    lane = lax.broadcasted_iota(jnp.int32, p.shape, 1)
    partner = jnp.where((lane & 16) == 0,
                        pltpu.roll(p, LANES - 16, 1), pltpu.roll(p, 16, 1))
    return p * cos + partner * sin_signed


def _inproj_latent_kernel(x_ref, mod_ref, g1_ref, w_ref, qg_ref, kg_ref, cos_ref, sin_ref,
                          q_ref, k_ref, v_ref, xr_ref, gg_ref):
    x = x_ref[...]
    h = _rmsnorm_rows(x, g1_ref[...]) * (1.0 + mod_ref[0, 1:2, :]) + mod_ref[0, 0:1, :]
    y = _dot(h.astype(BF16), w_ref[...])
    cos = cos_ref[...]
    sin = sin_ref[...]
    pieces = []
    for j in range(ATTN_WIDTH // LANES):
        p = _head_norm(y[:, j * LANES:(j + 1) * LANES], qg_ref[...])
        pieces.append(_rope(p, cos, sin))
    lane = lax.broadcasted_iota(jnp.int32, pieces[0].shape, 1)
    low = lane < HEAD_DIM
    n_blocks = x.shape[0] // BLOCK
    for g in range(Q_PER_KV):
        src0 = pieces[g // 2]
        src1 = pieces[2 + g // 2]
        if g % 2 == 0:
            out = jnp.where(low, src0, pltpu.roll(src1, HEAD_DIM, 1))
        else:
            out = jnp.where(low, pltpu.roll(src0, HEAD_DIM, 1), src1)
        out = (out * SCORE_SCALE).astype(BF16)
        for j in range(n_blocks):
            q_ref[0, j, g * BLOCK:(g + 1) * BLOCK, :] = out[j * BLOCK:(j + 1) * BLOCK, :]
    k = _head_norm(y[:, ATTN_WIDTH:ATTN_WIDTH + KV_WIDTH], kg_ref[...])
    k_ref[...] = _rope(k, cos, sin).astype(BF16)
    o = ATTN_WIDTH + KV_WIDTH
    v_ref[...] = y[:, o:o + KV_WIDTH].astype(BF16)
    o += KV_WIDTH
    xr_ref[...] = y[:, o:o + LRU_WIDTH]
    o += LRU_WIDTH
    gg_ref[...] = jax.nn.gelu(y[:, o:o + LRU_WIDTH])


def _inproj_ctx_kernel(x_ref, mod_ref, g1_ref, w_ref, kg_ref, k_ref, v_ref, xr_ref):
    x = x_ref[...]
    h = _rmsnorm_rows(x, g1_ref[...]) * (1.0 + mod_ref[0, 1:2, :]) + mod_ref[0, 0:1, :]
    y = _dot(h.astype(BF16), w_ref[...])
    k_ref[...] = _head_norm(y[:, 0:KV_WIDTH], kg_ref[...]).astype(BF16)
    v_ref[...] = y[:, KV_WIDTH:2 * KV_WIDTH].astype(BF16)
    xr_ref[...] = y[:, 2 * KV_WIDTH:2 * KV_WIDTH + LRU_WIDTH]


def _row_tile(n):
    return min(512, n)


def _inproj_latent(x2, mod3, g1, w_in, qg, kg, cos, sin, bsz, n):
    tm = _row_tile(n)
    tps = n // tm
    rows = bsz * n
    const = lambda i: (0, 0)
    return pl.pallas_call(
        _inproj_latent_kernel,
        out_shape=(
            jax.ShapeDtypeStruct((bsz, n // BLOCK, Q_PER_KV * BLOCK, LANES), BF16),
            jax.ShapeDtypeStruct((rows, KV_WIDTH), BF16),
            jax.ShapeDtypeStruct((rows, KV_WIDTH), BF16),
            jax.ShapeDtypeStruct((rows, LRU_WIDTH), F32),
            jax.ShapeDtypeStruct((rows, LRU_WIDTH), F32),
        ),
        grid=(rows // tm,),
        in_specs=[
            pl.BlockSpec((tm, D_MODEL), lambda i: (i, 0)),
            pl.BlockSpec((1, N_MOD, D_MODEL), lambda i: (i // tps, 0, 0)),
            pl.BlockSpec((1, D_MODEL), const),
            pl.BlockSpec((D_MODEL, IN_WIDTH), const),
            pl.BlockSpec((1, LANES), const),
            pl.BlockSpec((1, LANES), const),
            pl.BlockSpec((tm, LANES), lambda i: (i % tps, 0)),
            pl.BlockSpec((tm, LANES), lambda i: (i % tps, 0)),
        ],
        out_specs=(
            pl.BlockSpec((1, tm // BLOCK, Q_PER_KV * BLOCK, LANES),
                         lambda i: (i // tps, i % tps, 0, 0)),
            pl.BlockSpec((tm, KV_WIDTH), lambda i: (i, 0)),
            pl.BlockSpec((tm, KV_WIDTH), lambda i: (i, 0)),
            pl.BlockSpec((tm, LRU_WIDTH), lambda i: (i, 0)),
            pl.BlockSpec((tm, LRU_WIDTH), lambda i: (i, 0)),
        ),
        compiler_params=pltpu.CompilerParams(
            dimension_semantics=("arbitrary",), vmem_limit_bytes=VMEM_LIMIT_BYTES),
        name="inproj_latent",
    )(x2, mod3, g1, w_in, qg, kg, cos, sin)


def _inproj_ctx(c2, mod3, g1, w_ctx, kg, bsz, n_ctx):
    tm = _row_tile(n_ctx)
    rows = bsz * n_ctx
    ctx_row = mod3.shape[0] - 1
    width = w_ctx.shape[1]
    const = lambda i: (0, 0)
    return pl.pallas_call(
        _inproj_ctx_kernel,
        out_shape=(
            jax.ShapeDtypeStruct((rows, KV_WIDTH), BF16),
            jax.ShapeDtypeStruct((rows, KV_WIDTH), BF16),
            jax.ShapeDtypeStruct((rows, LRU_WIDTH), F32),
        ),
        grid=(rows // tm,),
        in_specs=[
            pl.BlockSpec((tm, D_MODEL), lambda i: (i, 0)),
            pl.BlockSpec((1, N_MOD, D_MODEL), lambda i: (ctx_row, 0, 0)),
            pl.BlockSpec((1, D_MODEL), const),
            pl.BlockSpec((D_MODEL, width), const),
            pl.BlockSpec((1, LANES), const),
        ],
        out_specs=(
            pl.BlockSpec((tm, KV_WIDTH), lambda i: (i, 0)),
            pl.BlockSpec((tm, KV_WIDTH), lambda i: (i, 0)),
            pl.BlockSpec((tm, LRU_WIDTH), lambda i: (i, 0)),
        ),
        compiler_params=pltpu.CompilerParams(
            dimension_semantics=("arbitrary",), vmem_limit_bytes=VMEM_LIMIT_BYTES),
        name="inproj_ctx",
    )(c2, mod3, g1, w_ctx, kg)


def _attn_kernel(sink_ref, q_ref, k_ref, v_ref, kc_ref, vc_ref, bias_ref, o_ref, s0_s, s1_s,
                 *, n, nb):
    j = pl.program_id(1)
    span = 3 * BLOCK

    def window_start(i):
        return pl.multiple_of(jnp.clip((i - 1) * BLOCK, 0, n - span), BLOCK)

    def head_only(t, h):
        lane = lax.broadcasted_iota(jnp.int32, t.shape, 1)
        keep = (lane < HEAD_DIM) if h == 0 else (lane >= HEAD_DIM)
        return jnp.where(keep, t, jnp.zeros_like(t))

    stages = functools.partial(_attn_stages, sink_ref, q_ref, k_ref, v_ref, kc_ref, vc_ref,
                               bias_ref, o_ref, j=j, nb=nb, window_start=window_start,
                               head_only=head_only)

    @pl.when((pl.program_id(0) == 0) & (j == 0))
    def _():
        s1_s[...] = jnp.zeros_like(s1_s)

    @pl.when(j % 2 == 0)
    def _():
        stages(s0_s, s1_s)

    @pl.when(j % 2 == 1)
    def _():
        stages(s1_s, s0_s)


def _attn_stages(sink_ref, q_ref, k_ref, v_ref, kc_ref, vc_ref, bias_ref, o_ref, s_new, s_old,
                 j, nb, window_start, head_only):
    span = 3 * BLOCK
    q = q_ref[0, 0]
    kw = k_ref[0, pl.ds(window_start(jnp.minimum(j, nb - 1)), span), :]
    kc = kc_ref[0]
    bias = bias_ref[0]
    for h in range(N_KV_HEADS):
        s_new[h, :, 0:span] = _dot_nt(q, head_only(kw, h)) + bias
        s_new[h, :, span:] = _dot_nt(q, head_only(kc, h))

    vw = v_ref[0, pl.ds(window_start(jnp.maximum(j - 1, 0)), span), :]
    vc = vc_ref[0]
    rows = q.shape[0]
    group = lax.broadcasted_iota(jnp.int32, (rows, 1), 0) // BLOCK
    acc = jnp.zeros((rows, LANES), F32)
    for h in range(N_KV_HEADS):
        s_loc = s_old[h, :, 0:span]
        s_ctx = s_old[h, :, span:]
        sink = jnp.zeros((rows, 1), F32)
        for g in range(Q_PER_KV):
            sink = jnp.where(group == g, sink_ref[h * Q_PER_KV + g] * LOG2_E, sink)
        m = jnp.maximum(jnp.maximum(jnp.max(s_loc, axis=-1, keepdims=True),
                                    jnp.max(s_ctx, axis=-1, keepdims=True)), sink)
        p_loc = jnp.exp2(s_loc - m)
        p_ctx = jnp.exp2(s_ctx - m)
        denom = (jnp.sum(p_loc, axis=-1, keepdims=True)
                 + jnp.sum(p_ctx, axis=-1, keepdims=True) + jnp.exp2(sink - m))
        o = (_dot(p_loc.astype(BF16), head_only(vw, h))
             + _dot(p_ctx.astype(BF16), head_only(vc, h)))
        acc = acc + o / denom
    o_ref[0, 0] = acc.astype(BF16)


def _band_bias(n):
    span = 3 * BLOCK
    r = jnp.arange(Q_PER_KV * BLOCK)[:, None] % BLOCK
    c = jnp.arange(span)[None, :]
    offsets = jnp.arange(span // BLOCK)[:, None, None] * BLOCK
    valid = jnp.abs(r - c + offsets) <= WINDOW
    return jnp.where(valid, 0.0, NEG_INF).astype(F32)


def _attn(sink, q_s, k3, v3, kc3, vc3):
    bsz, nb = q_s.shape[0], q_s.shape[1]
    n = k3.shape[1]
    n_ctx = kc3.shape[1]
    assert nb >= 3
    bias = _band_bias(n)

    def scored(j):
        return jnp.minimum(j, nb - 1)

    def finished(j):
        return jnp.maximum(j - 1, 0)

    def placement(b, j):
        i = scored(j)
        return (jnp.minimum(i, 1) + (i == nb - 1).astype(jnp.int32), 0, 0)

    return pl.pallas_call(
        functools.partial(_attn_kernel, n=n, nb=nb),
        out_shape=jax.ShapeDtypeStruct(q_s.shape, BF16),
        grid=(bsz, nb + 1),
        in_specs=[
            pl.BlockSpec(memory_space=pltpu.SMEM),
            pl.BlockSpec((1, 1, Q_PER_KV * BLOCK, LANES), lambda b, j: (b, scored(j), 0, 0)),
            pl.BlockSpec((1, n, KV_WIDTH), lambda b, i: (b, 0, 0)),
            pl.BlockSpec((1, n, KV_WIDTH), lambda b, i: (b, 0, 0)),
            pl.BlockSpec((1, n_ctx, KV_WIDTH), lambda b, i: (b, 0, 0)),
            pl.BlockSpec((1, n_ctx, KV_WIDTH), lambda b, i: (b, 0, 0)),
            pl.BlockSpec((1, Q_PER_KV * BLOCK, 3 * BLOCK), placement),
        ],
        out_specs=pl.BlockSpec((1, 1, Q_PER_KV * BLOCK, LANES),
                               lambda b, j: (b, finished(j), 0, 0)),
        scratch_shapes=[
            pltpu.VMEM((N_KV_HEADS, Q_PER_KV * BLOCK, 3 * BLOCK + n_ctx), F32),
            pltpu.VMEM((N_KV_HEADS, Q_PER_KV * BLOCK, 3 * BLOCK + n_ctx), F32),
        ],
        compiler_params=pltpu.CompilerParams(
            dimension_semantics=("arbitrary", "arbitrary"), vmem_limit_bytes=VMEM_LIMIT_BYTES),
        name="attn",
    )(sink, q_s, k3, v3, kc3, vc3, bias)


def _softplus(z):
    return jnp.maximum(z, 0.0) + jnp.log1p(jnp.exp(-jnp.abs(z)))


def _sigmoid(z):
    return 0.5 * jnp.tanh(0.5 * z) + 0.5


LRU_MAIN_SEG = 36


def _lru_plan(length):
    main, rest = divmod(length // SUBLANES, LRU_MAIN_SEG)
    if rest == 0:
        tails = ()
    elif rest % SUBLANES == 0 and rest > SUBLANES:
        tails = (rest - 5, 5)
    else:
        tails = (rest,)
    assert length % SUBLANES == 0 and all(t >= 2 for t in tails)
    return main, tails


def _lru_kernel(xr_ref, xrc_ref, gg_ref, cw_ref, cb_ref, wg_ref, bg_ref, lam_ref, o_ref,
                xc_s, xcc_s, hf_s, hb_s, *, n, n_ctx):
    cw = cw_ref[...]
    cb = cb_ref[...]
    sub = lax.broadcasted_iota(jnp.int32, (SUBLANES, LANES), 0)
    main_rows = SUBLANES * LRU_MAIN_SEG
    n_main, tails = _lru_plan(n)
    c_main, c_tails = _lru_plan(n_ctx)

    def static_tiles(n_main_tiles, tail_segs, with_main):
        tiles = [(k * main_rows, LRU_MAIN_SEG) for k in range(n_main_tiles)] if with_main else []
        t0 = n_main_tiles * main_rows
        for seg in tail_segs:
            tiles.append((t0, seg))
            t0 += SUBLANES * seg
        return tiles

    def aligned(v):
        return v if isinstance(v, int) else pl.multiple_of(v, SUBLANES)

    def conv_tile(src_ref, dst_ref, t0, seg, length):
        x = [src_ref[0, pl.ds(t0 + j, SUBLANES, stride=seg), :] for j in range(seg)]
        lo = aligned(jnp.maximum(t0 - SUBLANES, 0))
        hi = aligned(jnp.minimum(t0 + SUBLANES * seg, length - SUBLANES))
        prev = jnp.where(t0 > 0, src_ref[0, pl.ds(lo, SUBLANES), :], 0.0)
        nxt = jnp.where(t0 + SUBLANES * seg < length, src_ref[0, pl.ds(hi, SUBLANES), :], 0.0)

        def from_prev_segment(v, row):
            return jnp.where(sub == 0, row, pltpu.roll(v, 1, 0))

        def from_next_segment(v, row):
            return jnp.where(sub == SUBLANES - 1, row, pltpu.roll(v, SUBLANES - 1, 0))

        m1 = [from_prev_segment(x[seg - 1], prev[7:8])] + x[:seg - 1]
        m2 = [from_prev_segment(x[seg - 2], prev[6:7])] + m1[:seg - 1]
        p1 = x[1:] + [from_next_segment(x[0], nxt[0:1])]
        for j in range(seg):
            y = cb + cw[0:1] * m2[j]
            y = y + cw[1:2] * m1[j]
            y = y + cw[2:3] * x[j]
            y = y + cw[3:4] * p1[j]
            dst_ref[pl.ds(t0 + j * SUBLANES, SUBLANES), :] = y

    decay = [_softplus(-lam_ref[d:d + 1, :]) for d in range(2)]

    def coeffs(xc, d):
        w = wg_ref[0, :, 2 * LANES * d:2 * LANES * (d + 1)]
        g = _dot(xc.astype(BF16), w) + bg_ref[0, :, 2 * LANES * d:2 * LANES * (d + 1)]
        r = _sigmoid(g[:, :LANES])
        i = _sigmoid(g[:, LANES:])
        log_a = -LRU_C * r * decay[d]
        a = jnp.exp(log_a)
        mult = jnp.sqrt(-jnp.tanh(log_a) * (a * a + 1.0))
        return a, mult * i * xc

    def scan_tile(a, b, h_in, seg, reverse):
        order = list(range(seg - 1, -1, -1) if reverse else range(seg))
        local = [None] * seg
        prod = [None] * seg
        h = p = None
        for j in order:
            aj = a[j * SUBLANES:(j + 1) * SUBLANES, :]
            bj = b[j * SUBLANES:(j + 1) * SUBLANES, :]
            h = bj if h is None else aj * h + bj
            p = aj if p is None else aj * p
            local[j], prod[j] = h, p
        end = order[-1]
        carry = jnp.zeros((SUBLANES, LANES), F32)
        c = h_in
        for s in (range(SUBLANES - 1, -1, -1) if reverse else range(SUBLANES)):
            carry = jnp.where(sub == s, c, carry)
            c = prod[end][s:s + 1, :] * c + local[end][s:s + 1, :]
        return [local[j] + prod[j] * carry for j in range(seg)], c

    def run_tile(xc_ref, t0, seg, d, h, emit=None):
        a, b = coeffs(xc_ref[pl.ds(t0, SUBLANES * seg), :], d)
        states, h = scan_tile(a, b, h, seg, d == 1)
        if emit is not None:
            emit(t0, seg, states)
        return h

    def keep(dst_ref):
        def emit(t0, seg, states):
            for j in range(seg):
                dst_ref[pl.ds(t0 + j * SUBLANES, SUBLANES), :] = states[j]
        return emit

    def write_output(other_ref):
        def emit(t0, seg, states):
            for j in range(seg):
                rows = pl.ds(t0 + j, SUBLANES, stride=seg)
                other = other_ref[pl.ds(t0 + j * SUBLANES, SUBLANES), :]
                o_ref[0, rows, :] = (other + states[j]) * gg_ref[0, rows, :]
        return emit

    def main_t0(k):
        return pl.multiple_of(k * main_rows, SUBLANES)

    for t0, seg in static_tiles(c_main, c_tails, True):
        conv_tile(xrc_ref, xcc_s, t0, seg, n_ctx)

    def conv_body(k, carry):
        conv_tile(xr_ref, xc_s, main_t0(k), LRU_MAIN_SEG, n)
        return carry
    lax.fori_loop(0, n_main, conv_body, 0)
    for t0, seg in static_tiles(n_main, tails, False):
        conv_tile(xr_ref, xc_s, t0, seg, n)

    hf = jnp.zeros((1, LANES), F32)
    hb = jnp.zeros((1, LANES), F32)
    for t0, seg in static_tiles(c_main, c_tails, True):
        hf = run_tile(xcc_s, t0, seg, 0, hf)
    for t0, seg in reversed(static_tiles(c_main, c_tails, True)):
        hb = run_tile(xcc_s, t0, seg, 1, hb)
    for t0, seg in reversed(static_tiles(n_main, tails, False)):
        hb = run_tile(xc_s, t0, seg, 1, hb, keep(hb_s))

    def pair_body(first_visit):
        def body(k, carry):
            hf, hb = carry
            fwd_emit = keep(hf_s) if first_visit else write_output(hb_s)
            rev_emit = keep(hb_s) if first_visit else write_output(hf_s)
            hf = run_tile(xc_s, main_t0(k), LRU_MAIN_SEG, 0, hf, fwd_emit)
            hb = run_tile(xc_s, main_t0(n_main - 1 - k), LRU_MAIN_SEG, 1, hb, rev_emit)
            return hf, hb
        return body

    half = n_main // 2
    hf, hb = lax.fori_loop(0, half, pair_body(True), (hf, hb))
    if n_main % 2:
        hf = run_tile(xc_s, half * main_rows, LRU_MAIN_SEG, 0, hf, keep(hf_s))
        hb = run_tile(xc_s, half * main_rows, LRU_MAIN_SEG, 1, hb, write_output(hf_s))
    hf, hb = lax.fori_loop(n_main - half, n_main, pair_body(False), (hf, hb))
    for t0, seg in static_tiles(n_main, tails, False):
        hf = run_tile(xc_s, t0, seg, 0, hf, write_output(hb_s))


def _lru(xr3, xrc3, gg3, conv_w, conv_b, wg, bg, lam):
    bsz, n, _ = xr3.shape
    n_ctx = xrc3.shape[1]
    n_p = LRU_WIDTH // LANES
    return pl.pallas_call(
        functools.partial(_lru_kernel, n=n, n_ctx=n_ctx),
        out_shape=jax.ShapeDtypeStruct((bsz, n, LRU_WIDTH), F32),
        grid=(bsz, n_p),
        in_specs=[
            pl.BlockSpec((1, n, LANES), lambda b, p: (b, 0, p)),
            pl.BlockSpec((1, n_ctx, LANES), lambda b, p: (b, 0, p)),
            pl.BlockSpec((1, n, LANES), lambda b, p: (b, 0, p)),
            pl.BlockSpec((conv_w.shape[0], LANES), lambda b, p: (0, p)),
            pl.BlockSpec((1, LANES), lambda b, p: (0, p)),
            pl.BlockSpec((1, LANES, 4 * LANES), lambda b, p: (p, 0, 0)),
            pl.BlockSpec((1, 1, 4 * LANES), lambda b, p: (p, 0, 0)),
            pl.BlockSpec((2, LANES), lambda b, p: (0, p)),
        ],
        out_specs=pl.BlockSpec((1, n, LANES), lambda b, p: (b, 0, p)),
        scratch_shapes=[
            pltpu.VMEM((n, LANES), F32),
            pltpu.VMEM((n_ctx, LANES), F32),
            pltpu.VMEM((n, LANES), F32),
            pltpu.VMEM((n, LANES), F32),
        ],
        compiler_params=pltpu.CompilerParams(
            dimension_semantics=("arbitrary", "arbitrary"), vmem_limit_bytes=VMEM_LIMIT_BYTES),
        name="lru",
    )(xr3, xrc3, gg3, conv_w, conv_b, wg, bg, lam)


def _outproj_kernel(o_ref, rg_ref, x_ref, mod_ref, watt_ref, wrnn_ref, g2_ref, wr_ref,
                    x1t_ref, h2t_ref, lg_ref):
    tm = x_ref.shape[0]
    n_blocks = tm // BLOCK
    att = jnp.concatenate(
        [jnp.concatenate([o_ref[0, j, g * BLOCK:(g + 1) * BLOCK, :] for g in range(Q_PER_KV)],
                         axis=1) for j in range(n_blocks)], axis=0)
    y = _dot(att, watt_ref[...]) + _dot(rg_ref[...].astype(BF16), wrnn_ref[...])
    x1 = x_ref[...] + mod_ref[0, 2:3, :] * y
    h2 = _rmsnorm_rows(x1, g2_ref[...]) * (1.0 + mod_ref[0, 4:5, :]) + mod_ref[0, 3:4, :]
    for s in range(FEATURE_CHUNKS):
        rows = pl.ds(s, tm, stride=FEATURE_CHUNKS)
        x1t_ref[rows, :] = x1[:, s * LANES:(s + 1) * LANES]
        h2t_ref[rows, :] = h2[:, s * LANES:(s + 1) * LANES]
    lt = _dot_nt(wr_ref[...], h2.astype(BF16))
    for j in range(n_blocks):
        lg_ref[0, j] = lt[:, j * LANES:(j + 1) * LANES]


def _outproj(o_s, rg2, x2, mod3, w_att, w_rnn, g2, wr_t, bsz, n):
    tm = _row_tile(n)
    tps = n // tm
    rows = bsz * n
    const = lambda i: (0, 0)
    return pl.pallas_call(
        _outproj_kernel,
        out_shape=(
            jax.ShapeDtypeStruct((rows * FEATURE_CHUNKS, LANES), F32),
            jax.ShapeDtypeStruct((rows * FEATURE_CHUNKS, LANES), F32),
            jax.ShapeDtypeStruct((bsz, n // LANES, N_EXPERTS, LANES), F32),
        ),
        grid=(rows // tm,),
        in_specs=[
            pl.BlockSpec((1, tm // BLOCK, Q_PER_KV * BLOCK, LANES),
                         lambda i: (i // tps, i % tps, 0, 0)),
            pl.BlockSpec((tm, LRU_WIDTH), lambda i: (i, 0)),
            pl.BlockSpec((tm, D_MODEL), lambda i: (i, 0)),
            pl.BlockSpec((1, N_MOD, D_MODEL), lambda i: (i // tps, 0, 0)),
            pl.BlockSpec((ATTN_WIDTH, D_MODEL), const),
            pl.BlockSpec((LRU_WIDTH, D_MODEL), const),
            pl.BlockSpec((1, D_MODEL), const),
            pl.BlockSpec((N_EXPERTS, D_MODEL), const),
        ],
        out_specs=(
            pl.BlockSpec((tm * FEATURE_CHUNKS, LANES), lambda i: (i, 0)),
            pl.BlockSpec((tm * FEATURE_CHUNKS, LANES), lambda i: (i, 0)),
            pl.BlockSpec((1, tm // LANES, N_EXPERTS, LANES), lambda i: (i // tps, i % tps, 0, 0)),
        ),
        compiler_params=pltpu.CompilerParams(
            dimension_semantics=("arbitrary",), vmem_limit_bytes=VMEM_LIMIT_BYTES),
        name="outproj",
    )(o_s, rg2, x2, mod3, w_att, w_rnn, g2, wr_t)


def _topk_kernel(lg_ref, idx_ref, gate_ref, aff_s, cum_s, *, cap):
    lg = lg_ref[0]
    n_chunks = lg.shape[0]
    m = jnp.max(lg, axis=1, keepdims=True)
    ex = jnp.exp(lg - m)
    aff = ex / jnp.sum(ex, axis=1, keepdims=True)
    aff_s[...] = aff

    def count(mask):
        c = jnp.sum(jnp.where(mask, 1.0, 0.0), axis=0, keepdims=True)
        return jnp.sum(c, axis=2, keepdims=True)

    def bit_body(it, thr):
        cand = thr | jnp.left_shift(jnp.int32(1), 30 - it)
        return jnp.where(count(aff >= pltpu.bitcast(cand, F32)) >= cap, cand, thr)
    thr = lax.fori_loop(0, 31, bit_body, jnp.zeros((1, N_EXPERTS, 1), jnp.int32))
    thr = pltpu.bitcast(thr, F32)

    tri = (lax.broadcasted_iota(jnp.int32, (LANES, LANES), 0)
           <= lax.broadcasted_iota(jnp.int32, (LANES, LANES), 1))
    tri = jnp.where(tri, 1.0, 0.0).astype(BF16)

    def chunk_cumsum(mask):
        flat = jnp.where(mask, 1.0, 0.0).astype(BF16).reshape(n_chunks * N_EXPERTS, LANES)
        inc = _dot(flat, tri).reshape(n_chunks, N_EXPERTS, LANES)
        run = jnp.zeros((N_EXPERTS, 1), F32)
        before = []
        for c in range(n_chunks):
            before.append(run)
            run = run + inc[c][:, LANES - 1:LANES]
        return inc, jnp.stack(before, axis=0)

    above = aff > thr
    tied = aff == thr
    need = cap - count(above)
    tie_inc, tie_before = chunk_cumsum(tied)
    tie_rank = tie_before + tie_inc - jnp.where(tied, 1.0, 0.0)
    sel = above | (tied & (tie_rank < need))
    sel_inc, _ = chunk_cumsum(sel)
    cum_s[...] = sel_inc

    lower = (lax.broadcasted_iota(jnp.int32, (n_chunks, n_chunks), 1)
             <= lax.broadcasted_iota(jnp.int32, (n_chunks, n_chunks), 0))
    lower = jnp.where(lower, 1.0, 0.0).astype(BF16)
    slot = lax.broadcasted_iota(jnp.int32, (1, cap), 1).astype(F32)
    chunk_id = lax.broadcasted_iota(jnp.int32, (n_chunks, cap), 0).astype(F32)
    lane_id = lax.broadcasted_iota(jnp.int32, (LANES, cap), 0).astype(F32)
    for e in range(N_EXPERTS):
        cin = cum_s[:, e, :]
        tot = jnp.broadcast_to(cin[:, LANES - 1:LANES], cin.shape)
        cc_inc = _dot(lower, tot.astype(BF16))[:, 0:1]
        cc_exc = cc_inc - tot[:, 0:1]
        kc = jnp.sum(jnp.where(cc_inc <= slot, 1.0, 0.0), axis=0, keepdims=True)
        onehot = chunk_id == kc
        onehot_b = jnp.where(onehot, 1.0, 0.0).astype(BF16)
        counts_t = _dot_tn(cin.astype(BF16), onehot_b)
        local = slot - jnp.sum(jnp.where(onehot, cc_exc, 0.0), axis=0, keepdims=True)
        il = jnp.sum(jnp.where(counts_t <= local, 1.0, 0.0), axis=0, keepdims=True)
        idx_ref[0, e:e + 1, :] = (kc * LANES + il).astype(jnp.int32)
        aff_t = jnp.zeros((LANES, cap), F32)
        for part in _split_bf16(aff_s[:, e, :], 3):
            aff_t = aff_t + _dot_tn(part, onehot_b)
        gate_ref[0, e:e + 1, :] = jnp.sum(jnp.where(lane_id == il, aff_t, 0.0),
                                          axis=0, keepdims=True)


def _topk(lg4, cap):
    bsz, n_chunks = lg4.shape[0], lg4.shape[1]
    return pl.pallas_call(
        functools.partial(_topk_kernel, cap=cap),
        out_shape=(
            jax.ShapeDtypeStruct((bsz, N_EXPERTS, cap), jnp.int32),
            jax.ShapeDtypeStruct((bsz, N_EXPERTS, cap), F32),
        ),
        grid=(bsz,),
        in_specs=[pl.BlockSpec((1, n_chunks, N_EXPERTS, LANES), lambda b: (b, 0, 0, 0))],
        out_specs=(
            pl.BlockSpec((1, N_EXPERTS, cap), lambda b: (b, 0, 0)),
            pl.BlockSpec((1, N_EXPERTS, cap), lambda b: (b, 0, 0)),
        ),
        scratch_shapes=[
            pltpu.VMEM((n_chunks, N_EXPERTS, LANES), F32),
            pltpu.VMEM((n_chunks, N_EXPERTS, LANES), F32),
        ],
        compiler_params=pltpu.CompilerParams(
            dimension_semantics=("arbitrary",), vmem_limit_bytes=VMEM_LIMIT_BYTES),
        name="topk",
    )(lg4)


def _moe_kernel(idx_ref, gate_ref, h2t_hbm, x1t_hbm, mod_ref, wg_ref, wu_ref, wd_ref, out_hbm,
                h2_s, acc_s, xs_s, ys_s, stage_s, sems, *, n, cap, sub):
    b = pl.program_id(0)
    e = pl.program_id(1)
    n_b = pl.num_programs(0)
    n_e = pl.num_programs(1)
    tok_rows = n * FEATURE_CHUNKS
    n_parts = cap // sub

    def load_h2(sample):
        return pltpu.make_async_copy(
            h2t_hbm.at[pl.ds(sample * tok_rows, tok_rows), :], h2_s, sems.at[0])

    def load_x1(sample):
        return pltpu.make_async_copy(
            x1t_hbm.at[pl.ds(sample * tok_rows, tok_rows), :], acc_s, sems.at[1])

    gates = gate_ref[0, 0]
    gl = gates.shape[1]
    gates = jnp.pad(gates, ((0, SUBLANES - gates.shape[0]), (0, LANES - gl)))
    gate_cols = gates.T
    gate2 = mod_ref[0, N_MOD - 1:N_MOD, :]

    def token_tile(row):
        return pl.ds(pl.multiple_of(row * FEATURE_CHUNKS, FEATURE_CHUNKS), FEATURE_CHUNKS)

    def gather(part):
        for r in range(sub):
            t = idx_ref[0, 0, 0, part * sub + r]
            xs_s[part, r * FEATURE_CHUNKS:(r + 1) * FEATURE_CHUNKS, :] = h2_s[token_tile(t), :]

    def expert(part):
        x = jnp.concatenate(
            [xs_s[part, pl.ds(s, sub, stride=FEATURE_CHUNKS), :] for s in range(FEATURE_CHUNKS)],
            axis=1).astype(BF16)
        hid = (jax.nn.silu(_dot(x, wg_ref[0])) * _dot(x, wu_ref[0])).astype(BF16)
        y = _dot(hid, wd_ref[0]) * gate2
        first_col = part * sub // gl
        y = jnp.concatenate(
            [y[k * gl:(k + 1) * gl, :] * gate_cols[:gl, first_col + k:first_col + k + 1]
             for k in range(sub // gl)], axis=0)
        for s in range(FEATURE_CHUNKS):
            ys_s[part, pl.ds(s, sub, stride=FEATURE_CHUNKS), :] = y[:, s * LANES:(s + 1) * LANES]

    def scatter(part):
        for r0 in range(0, sub, SCATTER_UNROLL):
            updates = []
            for r in range(r0, r0 + SCATTER_UNROLL):
                dst = token_tile(idx_ref[0, 0, 0, part * sub + r])
                row = ys_s[part, r * FEATURE_CHUNKS:(r + 1) * FEATURE_CHUNKS, :]
                updates.append((dst, acc_s[dst, :] + row))
            for dst, val in updates:
                acc_s[dst, :] = val

    @pl.when((e == 0) & (b == 0))
    def _():
        load_h2(b).start()
        load_x1(b).start()

    @pl.when(e == 0)
    def _():
        load_h2(b).wait()

    for part in range(n_parts):
        gather(part)

    @pl.when((e == n_e - 1) & (b + 1 < n_b))
    def _():
        load_h2(b + 1).start()

    expert(0)

    @pl.when(e == 0)
    def _():
        load_x1(b).wait()

    for part in range(1, n_parts):
        expert(part)
    for part in range(n_parts):
        scatter(part)

    @pl.when(e == n_e - 1)
    def _():
        n_out = n // sub

        def store(c, slot):
            return pltpu.make_async_copy(
                stage_s.at[slot], out_hbm.at[pl.ds(b * n + c * sub, sub), :], sems.at[2 + slot])

        def out_body(c, carry):
            slot = c % 2

            @pl.when(c >= 2)
            def _():
                store(c - 2, slot).wait()
            row0 = pl.multiple_of(c * (sub * FEATURE_CHUNKS), sub * FEATURE_CHUNKS)
            for s in range(FEATURE_CHUNKS):
                stage_s[slot, :, s * LANES:(s + 1) * LANES] = (
                    acc_s[pl.ds(row0 + s, sub, stride=FEATURE_CHUNKS), :])
            store(c, slot).start()
            return carry
        lax.fori_loop(0, n_out, out_body, 0)

        @pl.when(b + 1 < n_b)
        def _():
            load_x1(b + 1).start()
        for c in range(n_out - 2, n_out):
            store(c, c % 2).wait()


def _moe(idx, gate, h2t, x1t, mod3, wg, wu, wd, bsz, n):
    cap = idx.shape[-1]
    sub = min(256, cap)
    gl = min(LANES, cap)
    assert n // sub >= 2 and cap % sub == 0 and sub % SCATTER_UNROLL == 0 and sub % gl == 0
    assert cap // gl <= SUBLANES
    tok_rows = n * FEATURE_CHUNKS
    weight_spec = pl.BlockSpec((1, D_MODEL, D_MODEL), lambda b, e: (e, 0, 0))
    smem_spec = pl.BlockSpec((1, 1, 1, cap), lambda b, e: (b, e, 0, 0),
                             memory_space=pltpu.SMEM)
    return pl.pallas_call(
        functools.partial(_moe_kernel, n=n, cap=cap, sub=sub),
        out_shape=jax.ShapeDtypeStruct((bsz * n, D_MODEL), F32),
        grid=(bsz, N_EXPERTS),
        in_specs=[
            smem_spec,
            pl.BlockSpec((1, 1, cap // gl, gl), lambda b, e: (b, e, 0, 0)),
            pl.BlockSpec(memory_space=pl.ANY),
            pl.BlockSpec(memory_space=pl.ANY),
            pl.BlockSpec((1, N_MOD, D_MODEL), lambda b, e: (b, 0, 0)),
            weight_spec,
            weight_spec,
            weight_spec,
        ],
        out_specs=pl.BlockSpec(memory_space=pl.ANY),
        scratch_shapes=[
            pltpu.VMEM((tok_rows, LANES), F32),
            pltpu.VMEM((tok_rows, LANES), F32),
            pltpu.VMEM((cap // sub, sub * FEATURE_CHUNKS, LANES), F32),
            pltpu.VMEM((cap // sub, sub * FEATURE_CHUNKS, LANES), F32),
            pltpu.VMEM((2, sub, D_MODEL), F32),
            pltpu.SemaphoreType.DMA((4,)),
        ],
        compiler_params=pltpu.CompilerParams(
            dimension_semantics=("arbitrary", "arbitrary"), vmem_limit_bytes=VMEM_LIMIT_BYTES),
        name="moe",
    )(idx[:, :, None, :], gate.reshape(bsz, N_EXPERTS, cap // gl, gl), h2t, x1t, mod3, wg, wu, wd)


def _rope_tables(n):
    t = jnp.arange(n)
    row = (t // GRID_W).astype(F32)
    col = (t % GRID_W).astype(F32)
    n_freq = HEAD_DIM // 4
    inv_freq = ROPE_BASE ** (-jnp.arange(n_freq, dtype=F32) / n_freq)
    ang_r = row[:, None] * inv_freq
    ang_c = col[:, None] * inv_freq
    cos = jnp.concatenate([jnp.cos(ang_r)] * 2 + [jnp.cos(ang_c)] * 2, axis=1)
    sin = jnp.concatenate([-jnp.sin(ang_r), jnp.sin(ang_r), -jnp.sin(ang_c), jnp.sin(ang_c)],
                          axis=1)
    return jnp.tile(cos, (1, 2)), jnp.tile(sin, (1, 2))


def _pack_gate_weights(w_r, b_r, w_i, b_i):
    n_p = LRU_WIDTH // LANES
    zeros = jnp.zeros((LRU_BLOCK_DIM, LRU_BLOCK_DIM), F32)

    def pair(w, p):
        top = jnp.concatenate([w[2 * p], zeros], axis=1)
        bot = jnp.concatenate([zeros, w[2 * p + 1]], axis=1)
        return jnp.concatenate([top, bot], axis=0)

    ws, bs = [], []
    for p in range(n_p):
        ws.append(jnp.concatenate(
            [pair(w_r[0], p), pair(w_i[0], p), pair(w_r[1], p), pair(w_i[1], p)], axis=1))
        sl = slice(p * LANES, (p + 1) * LANES)
        bs.append(jnp.concatenate([b_r[0, sl], b_i[0, sl], b_r[1, sl], b_i[1, sl]])[None, :])
    return jnp.stack(ws).astype(BF16), jnp.stack(bs)


def _layer(x, ctx, mod3, norm1_g, norm2_g, w_in, q_norm_g, k_norm_g, attn_sink, conv_w, conv_b,
           lru_w_r, lru_b_r, lru_w_i, lru_b_i, lru_lambda, w_out, w_router, w_gate, w_up, w_down):
    bsz, n, _ = x.shape
    n_ctx = ctx.shape[1]
    x2 = x.reshape(bsz * n, D_MODEL)
    c2 = ctx.reshape(bsz * n_ctx, D_MODEL)
    g1 = norm1_g[None, :]
    g2 = norm2_g[None, :]
    qg = jnp.tile(q_norm_g, 2)[None, :]
    kg = jnp.tile(k_norm_g, 2)[None, :]
    cos, sin = _rope_tables(n)
    w_in_b = w_in.astype(BF16)
    w_ctx_b = w_in_b[:, ATTN_WIDTH:ATTN_WIDTH + 2 * KV_WIDTH + LRU_WIDTH]

    q_s, k2, v2, xr2, gg2 = _inproj_latent(x2, mod3, g1, w_in_b, qg, kg, cos, sin, bsz, n)
    kc2, vc2, xrc2 = _inproj_ctx(c2, mod3, g1, w_ctx_b, kg, bsz, n_ctx)

    o_s = _attn(attn_sink, q_s, k2.reshape(bsz, n, KV_WIDTH), v2.reshape(bsz, n, KV_WIDTH),
                kc2.reshape(bsz, n_ctx, KV_WIDTH), vc2.reshape(bsz, n_ctx, KV_WIDTH))

    wg_lru, bg_lru = _pack_gate_weights(lru_w_r, lru_b_r, lru_w_i, lru_b_i)
    rg3 = _lru(xr2.reshape(bsz, n, LRU_WIDTH), xrc2.reshape(bsz, n_ctx, LRU_WIDTH),
               gg2.reshape(bsz, n, LRU_WIDTH), conv_w, conv_b[None, :], wg_lru, bg_lru,
               lru_lambda)

    w_att = (w_out[:ATTN_WIDTH].reshape(N_KV_HEADS, Q_PER_KV, HEAD_DIM, D_MODEL)
             .transpose(1, 0, 2, 3).reshape(ATTN_WIDTH, D_MODEL).astype(BF16))
    w_rnn = w_out[ATTN_WIDTH:].astype(BF16)
    x1t, h2t, lg4 = _outproj(o_s, rg3.reshape(bsz * n, LRU_WIDTH), x2, mod3, w_att, w_rnn, g2,
                             w_router.T.astype(BF16), bsz, n)

    cap = CAPACITY_FACTOR * n // N_EXPERTS
    idx, gate = _topk(lg4, cap)
    out = _moe(idx, gate, h2t, x1t, mod3, w_gate.astype(BF16), w_up.astype(BF16),
               w_down.astype(BF16), bsz, n)
    return out.reshape(bsz, n, D_MODEL)


def kernel(x, c, ctx, c_ctx, w_ada, b_ada, norm1_g, norm2_g, w_in, q_norm_g, k_norm_g, attn_sink, conv_w, conv_b, lru_w_r, lru_b_r, lru_w_i, lru_b_i, lru_lambda, w_out, w_router, w_gate, w_up, w_down):
    bsz = x.shape[0]
    depth = w_ada.shape[0]
    assert depth == 1, "context-stream update between layers is not implemented"
    pad_rows = -(bsz + 1) % SUBLANES
    cc = jnp.concatenate([c, jnp.zeros((pad_rows, D_MODEL), F32), c_ctx[None, :]], axis=0)
    for layer in range(depth):
        mod = _ada(cc, w_ada[layer], b_ada[layer][None, :])
        mod3 = mod.reshape(cc.shape[0], N_MOD, D_MODEL)
        x = _layer(x, ctx, mod3, norm1_g[layer], norm2_g[layer], w_in[layer], q_norm_g[layer],
                   k_norm_g[layer], attn_sink[layer], conv_w[layer], conv_b[layer],
                   lru_w_r[layer], lru_b_r[layer], lru_w_i[layer], lru_b_i[layer],
                   lru_lambda[layer], w_out[layer], w_router[layer], w_gate[layer],
                   w_up[layer], w_down[layer])
    return x
```

```python
import functools

import jax
import jax.numpy as jnp
from jax import lax
from jax.experimental import pallas as pl
from jax.experimental.pallas import tpu as pltpu

F32 = jnp.float32
BF16 = jnp.bfloat16

LANES = 128
SUBLANES = 8
VMEM_LIMIT_BYTES = 56 * 1024 * 1024

D_MODEL = 1024
HEAD_DIM = 64
N_Q_HEADS = 8
N_KV_HEADS = 2
Q_PER_KV = N_Q_HEADS // N_KV_HEADS
ATTN_WIDTH = N_Q_HEADS * HEAD_DIM
KV_WIDTH = N_KV_HEADS * HEAD_DIM
LRU_WIDTH = D_MODEL - ATTN_WIDTH
LRU_BLOCK_DIM = 64
IN_WIDTH = ATTN_WIDTH + 2 * KV_WIDTH + 2 * LRU_WIDTH
WINDOW = 128
BLOCK = 128
GRID_W = 64
ROPE_BASE = 10000.0
LRU_C = 8.0
N_EXPERTS = 16
CAPACITY_FACTOR = 2
EPS = 1e-6
NEG_INF = -1e30
LOG2_E = 1.4426950408889634
SCORE_SCALE = HEAD_DIM ** -0.5 * LOG2_E
N_MOD = 6
FEATURE_CHUNKS = D_MODEL // LANES
SCATTER_UNROLL = 16


def _dot(a, b):
    return jnp.dot(a, b, preferred_element_type=F32)


def _dot_nt(a, b):
    return lax.dot_general(a, b, (((1,), (1,)), ((), ())), preferred_element_type=F32)


def _dot_tn(a, b):
    return lax.dot_general(a, b, (((0,), (0,)), ((), ())), preferred_element_type=F32)


def _split_bf16(x, parts):
    out = []
    r = x
    for _ in range(parts):
        p = r.astype(BF16)
        out.append(p)
        r = r - p.astype(F32)
    return out


def _rmsnorm_rows(x, g):
    ms = jnp.mean(x * x, axis=-1, keepdims=True)
    return x * lax.rsqrt(ms + EPS) * g


def _ada_kernel(c_ref, w_ref, b_ref, o_ref):
    s = jax.nn.silu(c_ref[...])
    s_hi, s_lo = _split_bf16(s, 2)
    w_hi, w_lo = _split_bf16(w_ref[...], 2)
    o_ref[...] = _dot(s_hi, w_hi) + _dot(s_hi, w_lo) + _dot(s_lo, w_hi) + b_ref[...]


def _ada(cc, w_ada, b_ada):
    rows = cc.shape[0]
    width = w_ada.shape[1]
    return pl.pallas_call(
        _ada_kernel,
        out_shape=jax.ShapeDtypeStruct((rows, width), F32),
        grid=(width // D_MODEL,),
        in_specs=[
            pl.BlockSpec((rows, D_MODEL), lambda j: (0, 0)),
            pl.BlockSpec((D_MODEL, D_MODEL), lambda j: (0, j)),
            pl.BlockSpec((1, D_MODEL), lambda j: (0, j)),
        ],
        out_specs=pl.BlockSpec((rows, D_MODEL), lambda j: (0, j)),
        compiler_params=pltpu.CompilerParams(
            dimension_semantics=("arbitrary",), vmem_limit_bytes=VMEM_LIMIT_BYTES),
        name="ada",
    )(cc, w_ada, b_ada)


def _head_norm(p, g):
    lane = lax.broadcasted_iota(jnp.int32, p.shape, 1)
    low = lane < HEAD_DIM
    sq = p * p
    s_low = jnp.sum(jnp.where(low, sq, 0.0), axis=-1, keepdims=True)
    s_high = jnp.sum(jnp.where(low, 0.0, sq), axis=-1, keepdims=True)
    ms = jnp.where(low, s_low, s_high) * (1.0 / HEAD_DIM)
    return p * lax.rsqrt(ms + EPS) * g


def _rope(p, cos, sin_signed):
    lane = lax.broadcasted_iota(jnp.int32, p.shape, 1)
    partner = jnp.where((lane & 16) == 0,
                        pltpu.roll(p, LANES - 16, 1), pltpu.roll(p, 16, 1))
    return p * cos + partner * sin_signed


def _inproj_latent_kernel(x_ref, mod_ref, g1_ref, w_ref, qg_ref, kg_ref, cos_ref, sin_ref,
                          q_ref, k_ref, v_ref, xr_ref, gg_ref):
    x = x_ref[...]
    h = _rmsnorm_rows(x, g1_ref[...]) * (1.0 + mod_ref[0, 1:2, :]) + mod_ref[0, 0:1, :]
    y = _dot(h.astype(BF16), w_ref[...])
    cos = cos_ref[...]
    sin = sin_ref[...]
    pieces = []
    for j in range(ATTN_WIDTH // LANES):
        p = _head_norm(y[:, j * LANES:(j + 1) * LANES], qg_ref[...])
        pieces.append(_rope(p, cos, sin))
    lane = lax.broadcasted_iota(jnp.int32, pieces[0].shape, 1)
    low = lane < HEAD_DIM
    n_blocks = y.shape[0] // BLOCK
    for g in range(Q_PER_KV):
        src0 = pieces[g // 2]
        src1 = pieces[2 + g // 2]
        if g % 2 == 0:
            out = jnp.where(low, src0, pltpu.roll(src1, HEAD_DIM, 1))
        else:
            out = jnp.where(low, pltpu.roll(src0, HEAD_DIM, 1), src1)
        out = (out * SCORE_SCALE).astype(BF16)
        for j in range(n_blocks):
            q_ref[0, j, g * BLOCK:(g + 1) * BLOCK, :] = out[j * BLOCK:(j + 1) * BLOCK, :]
    k = _head_norm(y[:, ATTN_WIDTH:ATTN_WIDTH + KV_WIDTH], kg_ref[...])
    k_ref[...] = _rope(k, cos, sin).astype(BF16)
    o = ATTN_WIDTH + KV_WIDTH
    v_ref[...] = y[:, o:o + KV_WIDTH].astype(BF16)
    o += KV_WIDTH
    xr_ref[...] = y[:, o:o + LRU_WIDTH]
    o += LRU_WIDTH
    gg_ref[...] = jax.nn.gelu(y[:, o:o + LRU_WIDTH])


def _inproj_ctx_kernel(x_ref, mod_ref, g1_ref, w_ref, kg_ref, k_ref, v_ref, xr_ref):
    x = x_ref[...]
    h = _rmsnorm_rows(x, g1_ref[...]) * (1.0 + mod_ref[0, 1:2, :]) + mod_ref[0, 0:1, :]
    y = _dot(h.astype(BF16), w_ref[...])
    k_ref[...] = _head_norm(y[:, 0:KV_WIDTH], kg_ref[...]).astype(BF16)
    v_ref[...] = y[:, KV_WIDTH:2 * KV_WIDTH].astype(BF16)
    xr_ref[...] = y[:, 2 * KV_WIDTH:2 * KV_WIDTH + LRU_WIDTH]


def _row_tile(n):
    return min(512, n)


def _inproj_latent(x2, mod3, g1, w_in, qg, kg, cos, sin, bsz, n):
    tm = _row_tile(n)
    tps = n // tm
    rows = bsz * n
    const = lambda i: (0, 0)
    return pl.pallas_call(
        _inproj_latent_kernel,
        out_shape=(
            jax.ShapeDtypeStruct((bsz, n // BLOCK, Q_PER_KV * BLOCK, LANES), BF16),
            jax.ShapeDtypeStruct((rows, KV_WIDTH), BF16),
            jax.ShapeDtypeStruct((rows, KV_WIDTH), BF16),
            jax.ShapeDtypeStruct((rows, LRU_WIDTH), F32),
            jax.ShapeDtypeStruct((rows, LRU_WIDTH), F32),
        ),
        grid=(rows // tm,),
        in_specs=[
            pl.BlockSpec((tm, D_MODEL), lambda i: (i, 0)),
            pl.BlockSpec((1, N_MOD, D_MODEL), lambda i: (i // tps, 0, 0)),
            pl.BlockSpec((1, D_MODEL), const),
            pl.BlockSpec((D_MODEL, IN_WIDTH), const),
            pl.BlockSpec((1, LANES), const),
            pl.BlockSpec((1, LANES), const),
            pl.BlockSpec((tm, LANES), lambda i: (i % tps, 0)),
            pl.BlockSpec((tm, LANES), lambda i: (i % tps, 0)),
        ],
        out_specs=(
            pl.BlockSpec((1, tm // BLOCK, Q_PER_KV * BLOCK, LANES),
                         lambda i: (i // tps, i % tps, 0, 0)),
            pl.BlockSpec((tm, KV_WIDTH), lambda i: (i, 0)),
            pl.BlockSpec((tm, KV_WIDTH), lambda i: (i, 0)),
            pl.BlockSpec((tm, LRU_WIDTH), lambda i: (i, 0)),
            pl.BlockSpec((tm, LRU_WIDTH), lambda i: (i, 0)),
        ),
        compiler_params=pltpu.CompilerParams(
            dimension_semantics=("arbitrary",), vmem_limit_bytes=VMEM_LIMIT_BYTES),
        name="inproj_latent",
    )(x2, mod3, g1, w_in, qg, kg, cos, sin)


def _inproj_ctx(c2, mod3, g1, w_ctx, kg, bsz, n_ctx):
    tm = _row_tile(n_ctx)
    rows = bsz * n_ctx
    ctx_row = mod3.shape[0] - 1
    width = w_ctx.shape[1]
    const = lambda i: (0, 0)
    return pl.pallas_call(
        _inproj_ctx_kernel,
        out_shape=(
            jax.ShapeDtypeStruct((rows, KV_WIDTH), BF16),
            jax.ShapeDtypeStruct((rows, KV_WIDTH), BF16),
            jax.ShapeDtypeStruct((rows, LRU_WIDTH), F32),
        ),
        grid=(rows // tm,),
        in_specs=[
            pl.BlockSpec((tm, D_MODEL), lambda i: (i, 0)),
            pl.BlockSpec((1, N_MOD, D_MODEL), lambda i: (ctx_row, 0, 0)),
            pl.BlockSpec((1, D_MODEL), const),
            pl.BlockSpec((D_MODEL, width), const),
            pl.BlockSpec((1, LANES), const),
        ],
        out_specs=(
            pl.BlockSpec((tm, KV_WIDTH), lambda i: (i, 0)),
            pl.BlockSpec((tm, KV_WIDTH), lambda i: (i, 0)),
            pl.BlockSpec((tm, LRU_WIDTH), lambda i: (i, 0)),
        ),
        compiler_params=pltpu.CompilerParams(
            dimension_semantics=("arbitrary",), vmem_limit_bytes=VMEM_LIMIT_BYTES),
        name="inproj_ctx",
    )(c2, mod3, g1, w_ctx, kg)


def _attn_kernel(sink_ref, q_ref, k_ref, v_ref, kc_ref, vc_ref, bias_ref, o_ref, s0_s, s1_s,
                 *, n, nb):
    j = pl.program_id(1)
    span = 3 * BLOCK

    def window_start(i):
        return pl.multiple_of(jnp.clip((i - 1) * BLOCK, 0, n - span), BLOCK)

    def head_only(t, h):
        lane = lax.broadcasted_iota(jnp.int32, t.shape, 1)
        keep = (lane < HEAD_DIM) if h == 0 else (lane >= HEAD_DIM)
        return jnp.where(keep, t, jnp.zeros_like(t))

    stages = functools.partial(_attn_stages, sink_ref, q_ref, k_ref, v_ref, kc_ref, vc_ref,
                               bias_ref, o_ref, j=j, nb=nb, window_start=window_start,
                               head_only=head_only)

    @pl.when((pl.program_id(0) == 0) & (j == 0))
    def _():
        s1_s[...] = jnp.zeros_like(s1_s)

    @pl.when(j % 2 == 0)
    def _():
        stages(s0_s, s1_s)

    @pl.when(j % 2 == 1)
    def _():
        stages(s1_s, s0_s)


def _attn_stages(sink_ref, q_ref, k_ref, v_ref, kc_ref, vc_ref, bias_ref, o_ref, s_new, s_old,
                 j, nb, window_start, head_only):
    span = 3 * BLOCK
    n_ctx = kc_ref.shape[1]
    q = q_ref[0, 0]
    kw = k_ref[0, pl.ds(window_start(jnp.minimum(j, nb - 1)), span), :]
    kc = kc_ref[0]
    bias = bias_ref[0]
    rows = q.shape[0]
    group = lax.broadcasted_iota(jnp.int32, (rows, 1), 0) // BLOCK
    sinks = []
    for h in range(N_KV_HEADS):
        s_loc = _dot_nt(q, head_only(kw, h)) + bias
        s_ctx = _dot_nt(q, head_only(kc, h))
        sink = jnp.zeros((rows, 1), F32)
        for g in range(Q_PER_KV):
            sink = jnp.where(group == g, sink_ref[h * Q_PER_KV + g] * LOG2_E, sink)
        sinks.append(sink)
        m = jnp.maximum(jnp.maximum(jnp.max(s_loc, axis=-1, keepdims=True),
                                    jnp.max(s_ctx, axis=-1, keepdims=True)), sink)
        s_new[h, :, 0:span] = s_loc
        s_new[h, :, span:span + n_ctx] = s_ctx
        s_new[h, :, span + n_ctx:] = jnp.broadcast_to(m, (rows, LANES))

    vw = v_ref[0, pl.ds(window_start(jnp.maximum(j - 1, 0)), span), :]
    vc = vc_ref[0]
    acc = jnp.zeros((rows, LANES), F32)
    for h in range(N_KV_HEADS):
        m = s_old[h, :, span + n_ctx:]
        p = [jnp.exp2(s_old[h, :, c:c + LANES] - m) for c in range(0, span + n_ctx, LANES)]
        p_loc = jnp.concatenate(p[:span // LANES], axis=1)
        p_ctx = jnp.concatenate(p[span // LANES:], axis=1)
        denom = (jnp.sum(p_loc, axis=-1, keepdims=True)
                 + jnp.sum(p_ctx, axis=-1, keepdims=True) + jnp.exp2(sinks[h] - m[:, 0:1]))
        o = (_dot(p_loc.astype(BF16), head_only(vw, h))
             + _dot(p_ctx.astype(BF16), head_only(vc, h)))
        acc = acc + o / denom
    o_ref[0, 0] = acc.astype(BF16)


def _band_bias(n):
    span = 3 * BLOCK
    r = jnp.arange(Q_PER_KV * BLOCK)[:, None] % BLOCK
    c = jnp.arange(span)[None, :]
    offsets = jnp.arange(span // BLOCK)[:, None, None] * BLOCK
    valid = jnp.abs(r - c + offsets) <= WINDOW
    return jnp.where(valid, 0.0, NEG_INF).astype(F32)


def _attn(sink, q_s, k3, v3, kc3, vc3):
    bsz, nb = q_s.shape[0], q_s.shape[1]
    n = k3.shape[1]
    n_ctx = kc3.shape[1]
    assert nb >= 3
    bias = _band_bias(n)

    def scored(j):
        return jnp.minimum(j, nb - 1)

    def finished(j):
        return jnp.maximum(j - 1, 0)

    def placement(b, j):
        i = scored(j)
        return (jnp.minimum(i, 1) + (i == nb - 1).astype(jnp.int32), 0, 0)

    return pl.pallas_call(
        functools.partial(_attn_kernel, n=n, nb=nb),
        out_shape=jax.ShapeDtypeStruct(q_s.shape, BF16),
        grid=(bsz, nb + 1),
        in_specs=[
            pl.BlockSpec(memory_space=pltpu.SMEM),
            pl.BlockSpec((1, 1, Q_PER_KV * BLOCK, LANES), lambda b, j: (b, scored(j), 0, 0)),
            pl.BlockSpec((1, n, KV_WIDTH), lambda b, i: (b, 0, 0)),
            pl.BlockSpec((1, n, KV_WIDTH), lambda b, i: (b, 0, 0)),
            pl.BlockSpec((1, n_ctx, KV_WIDTH), lambda b, i: (b, 0, 0)),
            pl.BlockSpec((1, n_ctx, KV_WIDTH), lambda b, i: (b, 0, 0)),
            pl.BlockSpec((1, Q_PER_KV * BLOCK, 3 * BLOCK), placement),
        ],
        out_specs=pl.BlockSpec((1, 1, Q_PER_KV * BLOCK, LANES),
                               lambda b, j: (b, finished(j), 0, 0)),
        scratch_shapes=[
            pltpu.VMEM((N_KV_HEADS, Q_PER_KV * BLOCK, 3 * BLOCK + n_ctx + LANES), F32),
            pltpu.VMEM((N_KV_HEADS, Q_PER_KV * BLOCK, 3 * BLOCK + n_ctx + LANES), F32),
        ],
        compiler_params=pltpu.CompilerParams(
            dimension_semantics=("arbitrary", "arbitrary"), vmem_limit_bytes=VMEM_LIMIT_BYTES),
        name="attn",
    )(sink, q_s, k3, v3, kc3, vc3, bias)


def _softplus(z):
    return jnp.maximum(z, 0.0) + jnp.log1p(jnp.exp(-jnp.abs(z)))


LRU_MAIN_SEG = 36


def _lru_plan(length):
    main, rest = divmod(length // SUBLANES, LRU_MAIN_SEG)
    if rest == 0:
        tails = ()
    elif rest % SUBLANES == 0 and rest > SUBLANES:
        tails = (rest - 5, 5)
    else:
        tails = (rest,)
    assert length % SUBLANES == 0 and all(t >= 2 for t in tails)
    return main, tails


def _lru_kernel(xr_ref, xrc_ref, gg_ref, cw_ref, cb_ref, wg_ref, bg_ref, lam_ref, o_ref,
                xc_s, xcc_s, hf_s, hb_s, *, n, n_ctx):
    cw = cw_ref[...]
    cb = cb_ref[...]
    sub = lax.broadcasted_iota(jnp.int32, (SUBLANES, LANES), 0)
    main_rows = SUBLANES * LRU_MAIN_SEG
    n_main, tails = _lru_plan(n)
    c_main, c_tails = _lru_plan(n_ctx)

    def static_tiles(n_main_tiles, tail_segs, with_main):
        tiles = [(k * main_rows, LRU_MAIN_SEG) for k in range(n_main_tiles)] if with_main else []
        t0 = n_main_tiles * main_rows
        for seg in tail_segs:
            tiles.append((t0, seg))
            t0 += SUBLANES * seg
        return tiles

    def aligned(v):
        return v if isinstance(v, int) else pl.multiple_of(v, SUBLANES)

    def conv_tile(src_ref, dst_ref, t0, seg, length):
        x = [src_ref[0, pl.ds(t0 + j, SUBLANES, stride=seg), :] for j in range(seg)]
        lo = aligned(jnp.maximum(t0 - SUBLANES, 0))
        hi = aligned(jnp.minimum(t0 + SUBLANES * seg, length - SUBLANES))
        prev = jnp.where(t0 > 0, src_ref[0, pl.ds(lo, SUBLANES), :], 0.0)
        nxt = jnp.where(t0 + SUBLANES * seg < length, src_ref[0, pl.ds(hi, SUBLANES), :], 0.0)

        def from_prev_segment(v, row):
            return jnp.where(sub == 0, row, pltpu.roll(v, 1, 0))

        def from_next_segment(v, row):
            return jnp.where(sub == SUBLANES - 1, row, pltpu.roll(v, SUBLANES - 1, 0))

        m1 = [from_prev_segment(x[seg - 1], prev[7:8])] + x[:seg - 1]
        m2 = [from_prev_segment(x[seg - 2], prev[6:7])] + m1[:seg - 1]
        p1 = x[1:] + [from_next_segment(x[0], nxt[0:1])]
        for j in range(seg):
            y = cb + cw[0:1] * m2[j]
            y = y + cw[1:2] * m1[j]
            y = y + cw[2:3] * x[j]
            y = y + cw[3:4] * p1[j]
            dst_ref[pl.ds(t0 + j * SUBLANES, SUBLANES), :] = y

    half_rate = [-0.5 * LRU_C * _softplus(-lam_ref[d:d + 1, :]) for d in range(2)]

    def coeffs(xc, d):
        w = wg_ref[0, :, 2 * LANES * d:2 * LANES * (d + 1)]
        g = _dot(xc.astype(BF16), w) + bg_ref[0, :, 2 * LANES * d:2 * LANES * (d + 1)]
        t_r = jnp.tanh(g[:, :LANES])
        t_i = jnp.tanh(g[:, LANES:])
        log_a = t_r * half_rate[d] + half_rate[d]
        a = jnp.exp(log_a)
        mult = jnp.sqrt(-jnp.tanh(log_a) * (a * a + 1.0))
        return a, mult * (t_i + 1.0) * xc

    def scan_tile(a, b, h_in, seg, reverse):
        order = list(range(seg - 1, -1, -1) if reverse else range(seg))
        local = [None] * seg
        prod = [None] * seg
        h = p = None
        for j in order:
            aj = a[j * SUBLANES:(j + 1) * SUBLANES, :]
            bj = b[j * SUBLANES:(j + 1) * SUBLANES, :]
            h = bj if h is None else aj * h + bj
            p = aj if p is None else aj * p
            local[j], prod[j] = h, p
        end = order[-1]
        carry = jnp.zeros((SUBLANES, LANES), F32)
        c = h_in
        for s in (range(SUBLANES - 1, -1, -1) if reverse else range(SUBLANES)):
            carry = jnp.where(sub == s, c, carry)
            c = prod[end][s:s + 1, :] * c + local[end][s:s + 1, :]
        return [local[j] + prod[j] * carry for j in range(seg)], c

    def run_tile(xc_ref, t0, seg, d, h, emit=None):
        a, b = coeffs(xc_ref[pl.ds(t0, SUBLANES * seg), :], d)
        states, h = scan_tile(a, b, h, seg, d == 1)
        if emit is not None:
            emit(t0, seg, states)
        return h

    def keep(dst_ref):
        def emit(t0, seg, states):
            for j in range(seg):
                dst_ref[pl.ds(t0 + j * SUBLANES, SUBLANES), :] = states[j]
        return emit

    def write_output(other_ref):
        def emit(t0, seg, states):
            for j in range(seg):
                rows = pl.ds(t0 + j, SUBLANES, stride=seg)
                other = other_ref[pl.ds(t0 + j * SUBLANES, SUBLANES), :]
                o_ref[0, rows, :] = (other + states[j]) * gg_ref[0, rows, :]
        return emit

    def main_t0(k):
        return pl.multiple_of(k * main_rows, SUBLANES)

    for t0, seg in static_tiles(c_main, c_tails, True):
        conv_tile(xrc_ref, xcc_s, t0, seg, n_ctx)

    def conv_body(k, carry):
        conv_tile(xr_ref, xc_s, main_t0(k), LRU_MAIN_SEG, n)
        return carry
    lax.fori_loop(0, n_main, conv_body, 0)
    for t0, seg in static_tiles(n_main, tails, False):
        conv_tile(xr_ref, xc_s, t0, seg, n)

    hf = jnp.zeros((1, LANES), F32)
    hb = jnp.zeros((1, LANES), F32)
    for t0, seg in static_tiles(c_main, c_tails, True):
        hf = run_tile(xcc_s, t0, seg, 0, hf)
    for t0, seg in reversed(static_tiles(c_main, c_tails, True)):
        hb = run_tile(xcc_s, t0, seg, 1, hb)
    for t0, seg in reversed(static_tiles(n_main, tails, False)):
        hb = run_tile(xc_s, t0, seg, 1, hb, keep(hb_s))

    def pair_body(first_visit):
        def body(k, carry):
            hf, hb = carry
            fwd_emit = keep(hf_s) if first_visit else write_output(hb_s)
            rev_emit = keep(hb_s) if first_visit else write_output(hf_s)
            hf = run_tile(xc_s, main_t0(k), LRU_MAIN_SEG, 0, hf, fwd_emit)
            hb = run_tile(xc_s, main_t0(n_main - 1 - k), LRU_MAIN_SEG, 1, hb, rev_emit)
            return hf, hb
        return body

    half = n_main // 2
    hf, hb = lax.fori_loop(0, half, pair_body(True), (hf, hb))
    if n_main % 2:
        hf = run_tile(xc_s, half * main_rows, LRU_MAIN_SEG, 0, hf, keep(hf_s))
        hb = run_tile(xc_s, half * main_rows, LRU_MAIN_SEG, 1, hb, write_output(hf_s))
    hf, hb = lax.fori_loop(n_main - half, n_main, pair_body(False), (hf, hb))
    for t0, seg in static_tiles(n_main, tails, False):
        hf = run_tile(xc_s, t0, seg, 0, hf, write_output(hb_s))


def _lru(xr3, xrc3, gg3, conv_w, conv_b, wg, bg, lam):
    bsz, n, _ = xr3.shape
    n_ctx = xrc3.shape[1]
    n_p = LRU_WIDTH // LANES
    return pl.pallas_call(
        functools.partial(_lru_kernel, n=n, n_ctx=n_ctx),
        out_shape=jax.ShapeDtypeStruct((bsz, n, LRU_WIDTH), F32),
        grid=(bsz, n_p),
        in_specs=[
            pl.BlockSpec((1, n, LANES), lambda b, p: (b, 0, p)),
            pl.BlockSpec((1, n_ctx, LANES), lambda b, p: (b, 0, p)),
            pl.BlockSpec((1, n, LANES), lambda b, p: (b, 0, p)),
            pl.BlockSpec((conv_w.shape[0], LANES), lambda b, p: (0, p)),
            pl.BlockSpec((1, LANES), lambda b, p: (0, p)),
            pl.BlockSpec((1, LANES, 4 * LANES), lambda b, p: (p, 0, 0)),
            pl.BlockSpec((1, 1, 4 * LANES), lambda b, p: (p, 0, 0)),
            pl.BlockSpec((2, LANES), lambda b, p: (0, p)),
        ],
        out_specs=pl.BlockSpec((1, n, LANES), lambda b, p: (b, 0, p)),
        scratch_shapes=[
            pltpu.VMEM((n, LANES), F32),
            pltpu.VMEM((n_ctx, LANES), F32),
            pltpu.VMEM((n, LANES), F32),
            pltpu.VMEM((n, LANES), F32),
        ],
        compiler_params=pltpu.CompilerParams(
            dimension_semantics=("arbitrary", "arbitrary"), vmem_limit_bytes=VMEM_LIMIT_BYTES),
        name="lru",
    )(xr3, xrc3, gg3, conv_w, conv_b, wg, bg, lam)


def _outproj_kernel(o_ref, rg_ref, x_ref, mod_ref, watt_ref, wrnn_ref, g2_ref, wr_ref,
                    x1t_ref, h2t_ref, lg_ref):
    tm = x_ref.shape[0]
    n_blocks = tm // BLOCK
    att = jnp.concatenate(
        [jnp.concatenate([o_ref[0, j, g * BLOCK:(g + 1) * BLOCK, :] for g in range(Q_PER_KV)],
                         axis=1) for j in range(n_blocks)], axis=0)
    y = _dot(att, watt_ref[...]) + _dot(rg_ref[...].astype(BF16), wrnn_ref[...])
    x1 = x_ref[...] + mod_ref[0, 2:3, :] * y
    h2 = _rmsnorm_rows(x1, g2_ref[...]) * (1.0 + mod_ref[0, 4:5, :]) + mod_ref[0, 3:4, :]
    for s in range(FEATURE_CHUNKS):
        rows = pl.ds(s, tm, stride=FEATURE_CHUNKS)
        x1t_ref[rows, :] = x1[:, s * LANES:(s + 1) * LANES]
        h2t_ref[rows, :] = h2[:, s * LANES:(s + 1) * LANES]
    lt = _dot_nt(wr_ref[...], h2.astype(BF16))
    for j in range(n_blocks):
        lg_ref[0, j] = lt[:, j * LANES:(j + 1) * LANES]


def _outproj(o_s, rg2, x2, mod3, w_att, w_rnn, g2, wr_t, bsz, n):
    tm = _row_tile(n)
    tps = n // tm
    rows = bsz * n
    const = lambda i: (0, 0)
    return pl.pallas_call(
        _outproj_kernel,
        out_shape=(
            jax.ShapeDtypeStruct((rows * FEATURE_CHUNKS, LANES), F32),
            jax.ShapeDtypeStruct((rows * FEATURE_CHUNKS, LANES), F32),
            jax.ShapeDtypeStruct((bsz, n // LANES, N_EXPERTS, LANES), F32),
        ),
        grid=(rows // tm,),
        in_specs=[
            pl.BlockSpec((1, tm // BLOCK, Q_PER_KV * BLOCK, LANES),
                         lambda i: (i // tps, i % tps, 0, 0)),
            pl.BlockSpec((tm, LRU_WIDTH), lambda i: (i, 0)),
            pl.BlockSpec((tm, D_MODEL), lambda i: (i, 0)),
            pl.BlockSpec((1, N_MOD, D_MODEL), lambda i: (i // tps, 0, 0)),
            pl.BlockSpec((ATTN_WIDTH, D_MODEL), const),
            pl.BlockSpec((LRU_WIDTH, D_MODEL), const),
            pl.BlockSpec((1, D_MODEL), const),
            pl.BlockSpec((N_EXPERTS, D_MODEL), const),
        ],
        out_specs=(
            pl.BlockSpec((tm * FEATURE_CHUNKS, LANES), lambda i: (i, 0)),
            pl.BlockSpec((tm * FEATURE_CHUNKS, LANES), lambda i: (i, 0)),
            pl.BlockSpec((1, tm // LANES, N_EXPERTS, LANES), lambda i: (i // tps, i % tps, 0, 0)),
        ),
        compiler_params=pltpu.CompilerParams(
            dimension_semantics=("arbitrary",), vmem_limit_bytes=VMEM_LIMIT_BYTES),
        name="outproj",
    )(o_s, rg2, x2, mod3, w_att, w_rnn, g2, wr_t)


def _topk_kernel(lg_ref, idx_ref, gate_ref, aff_s, cum_s, *, cap):
    lg = lg_ref[0]
    n_chunks = lg.shape[0]
    m = jnp.max(lg, axis=1, keepdims=True)
    ex = jnp.exp(lg - m)
    aff = ex / jnp.sum(ex, axis=1, keepdims=True)
    aff_s[...] = aff

    def count(mask):
        c = jnp.sum(jnp.where(mask, 1.0, 0.0), axis=0, keepdims=True)
        return jnp.sum(c, axis=2, keepdims=True)

    def bit_body(it, thr):
        cand = thr | jnp.left_shift(jnp.int32(1), 30 - it)
        return jnp.where(count(aff >= pltpu.bitcast(cand, F32)) >= cap, cand, thr)
    thr = lax.fori_loop(0, 31, bit_body, jnp.zeros((1, N_EXPERTS, 1), jnp.int32))
    thr = pltpu.bitcast(thr, F32)

    tri = (lax.broadcasted_iota(jnp.int32, (LANES, LANES), 0)
           <= lax.broadcasted_iota(jnp.int32, (LANES, LANES), 1))
    tri = jnp.where(tri, 1.0, 0.0).astype(BF16)

    def chunk_cumsum(mask):
        flat = jnp.where(mask, 1.0, 0.0).astype(BF16).reshape(n_chunks * N_EXPERTS, LANES)
        inc = _dot(flat, tri).reshape(n_chunks, N_EXPERTS, LANES)
        run = jnp.zeros((N_EXPERTS, 1), F32)
        before = []
        for c in range(n_chunks):
            before.append(run)
            run = run + inc[c][:, LANES - 1:LANES]
        return inc, jnp.stack(before, axis=0)

    above = aff > thr
    tied = aff == thr
    need = cap - count(above)
    tie_inc, tie_before = chunk_cumsum(tied)
    tie_rank = tie_before + tie_inc - jnp.where(tied, 1.0, 0.0)
    sel = above | (tied & (tie_rank < need))
    sel_inc, _ = chunk_cumsum(sel)
    cum_s[...] = sel_inc

    lower = (lax.broadcasted_iota(jnp.int32, (n_chunks, n_chunks), 1)
             <= lax.broadcasted_iota(jnp.int32, (n_chunks, n_chunks), 0))
    lower = jnp.where(lower, 1.0, 0.0).astype(BF16)
    slot = lax.broadcasted_iota(jnp.int32, (1, cap), 1).astype(F32)
    chunk_id = lax.broadcasted_iota(jnp.int32, (n_chunks, cap), 0).astype(F32)
    lane_id = lax.broadcasted_iota(jnp.int32, (LANES, cap), 0).astype(F32)
    for e in range(N_EXPERTS):
        cin = cum_s[:, e, :]
        tot = jnp.broadcast_to(cin[:, LANES - 1:LANES], cin.shape)
        cc_inc = _dot(lower, tot.astype(BF16))[:, 0:1]
        cc_exc = cc_inc - tot[:, 0:1]
        kc = jnp.sum(jnp.where(cc_inc <= slot, 1.0, 0.0), axis=0, keepdims=True)
        onehot = chunk_id == kc
        onehot_b = jnp.where(onehot, 1.0, 0.0).astype(BF16)
        counts_t = _dot_tn(cin.astype(BF16), onehot_b)
        local = slot - jnp.sum(jnp.where(onehot, cc_exc, 0.0), axis=0, keepdims=True)
        il = jnp.sum(jnp.where(counts_t <= local, 1.0, 0.0), axis=0, keepdims=True)
        idx_ref[0, e:e + 1, :] = ((kc * LANES + il) * FEATURE_CHUNKS).astype(jnp.int32)
        aff_t = jnp.zeros((LANES, cap), F32)
        for part in _split_bf16(aff_s[:, e, :], 3):
            aff_t = aff_t + _dot_tn(part, onehot_b)
        gate_ref[0, e:e + 1, :] = jnp.sum(jnp.where(lane_id == il, aff_t, 0.0),
                                          axis=0, keepdims=True)


def _topk(lg4, cap):
    bsz, n_chunks = lg4.shape[0], lg4.shape[1]
    return pl.pallas_call(
        functools.partial(_topk_kernel, cap=cap),
        out_shape=(
            jax.ShapeDtypeStruct((bsz, N_EXPERTS, cap), jnp.int32),
            jax.ShapeDtypeStruct((bsz, N_EXPERTS, cap), F32),
        ),
        grid=(bsz,),
        in_specs=[pl.BlockSpec((1, n_chunks, N_EXPERTS, LANES), lambda b: (b, 0, 0, 0))],
        out_specs=(
            pl.BlockSpec((1, N_EXPERTS, cap), lambda b: (b, 0, 0)),
            pl.BlockSpec((1, N_EXPERTS, cap), lambda b: (b, 0, 0)),
        ),
        scratch_shapes=[
            pltpu.VMEM((n_chunks, N_EXPERTS, LANES), F32),
            pltpu.VMEM((n_chunks, N_EXPERTS, LANES), F32),
        ],
        compiler_params=pltpu.CompilerParams(
            dimension_semantics=("arbitrary",), vmem_limit_bytes=VMEM_LIMIT_BYTES),
        name="topk",
    )(lg4)


def _moe_kernel(idx_ref, gate_ref, h2t_hbm, x1t_hbm, mod_ref, wg_ref, wu_ref, wd_ref, out_hbm,
                h2_s, acc_s, xs_s, ys_s, stage_s, sems, *, n, cap, sub):
    b = pl.program_id(0)
    e = pl.program_id(1)
    n_b = pl.num_programs(0)
    n_e = pl.num_programs(1)
    tok_rows = n * FEATURE_CHUNKS
    n_parts = cap // sub

    def load_h2(sample):
        return pltpu.make_async_copy(
            h2t_hbm.at[pl.ds(sample * tok_rows, tok_rows), :], h2_s, sems.at[0])

    def load_x1(sample):
        return pltpu.make_async_copy(
            x1t_hbm.at[pl.ds(sample * tok_rows, tok_rows), :], acc_s, sems.at[1])

    gates = gate_ref[0, 0]
    gl = gates.shape[1]
    gates = jnp.pad(gates, ((0, SUBLANES - gates.shape[0]), (0, LANES - gl)))
    gate_cols = gates.T
    gate2 = mod_ref[0, N_MOD - 1:N_MOD, :]

    def token_tile(first_row):
        return pl.ds(pl.multiple_of(first_row, FEATURE_CHUNKS), FEATURE_CHUNKS)

    def gather(part):
        for r in range(sub):
            t = idx_ref[0, 0, 0, part * sub + r]
            xs_s[part, r * FEATURE_CHUNKS:(r + 1) * FEATURE_CHUNKS, :] = h2_s[token_tile(t), :]

    def expert(part):
        x = jnp.concatenate(
            [xs_s[part, pl.ds(s, sub, stride=FEATURE_CHUNKS), :] for s in range(FEATURE_CHUNKS)],
            axis=1).astype(BF16)
        hid = (jax.nn.silu(_dot(x, wg_ref[0])) * _dot(x, wu_ref[0])).astype(BF16)
        y = _dot(hid, wd_ref[0]) * gate2
        first_col = part * sub // gl
        y = jnp.concatenate(
            [y[k * gl:(k + 1) * gl, :] * gate_cols[:gl, first_col + k:first_col + k + 1]
             for k in range(sub // gl)], axis=0)
        for s in range(FEATURE_CHUNKS):
            ys_s[part, pl.ds(s, sub, stride=FEATURE_CHUNKS), :] = y[:, s * LANES:(s + 1) * LANES]

    def scatter(part):
        for r0 in range(0, sub, SCATTER_UNROLL):
            updates = []
            for r in range(r0, r0 + SCATTER_UNROLL):
                dst = token_tile(idx_ref[0, 0, 0, part * sub + r])
                row = ys_s[part, r * FEATURE_CHUNKS:(r + 1) * FEATURE_CHUNKS, :]
                updates.append((dst, acc_s[dst, :] + row))
            for dst, val in updates:
                acc_s[dst, :] = val

    @pl.when((e == 0) & (b == 0))
    def _():
        load_h2(b).start()
        load_x1(b).start()

    @pl.when(e == 0)
    def _():
        load_h2(b).wait()

    for part in range(n_parts):
        gather(part)

    @pl.when((e == n_e - 1) & (b + 1 < n_b))
    def _():
        load_h2(b + 1).start()

    expert(0)

    @pl.when(e == 0)
    def _():
        load_x1(b).wait()

    for part in range(1, n_parts):
        expert(part)
    for part in range(n_parts):
        scatter(part)

    @pl.when(e == n_e - 1)
    def _():
        n_out = n // sub

        def store(c, slot):
            return pltpu.make_async_copy(
                stage_s.at[slot], out_hbm.at[pl.ds(b * n + c * sub, sub), :], sems.at[2 + slot])

        def out_body(c, carry):
            slot = c % 2

            @pl.when(c >= 2)
            def _():
                store(c - 2, slot).wait()
            row0 = pl.multiple_of(c * (sub * FEATURE_CHUNKS), sub * FEATURE_CHUNKS)
            for s in range(FEATURE_CHUNKS):
                stage_s[slot, :, s * LANES:(s + 1) * LANES] = (
                    acc_s[pl.ds(row0 + s, sub, stride=FEATURE_CHUNKS), :])
            store(c, slot).start()
            return carry
        lax.fori_loop(0, n_out, out_body, 0)

        @pl.when(b + 1 < n_b)
        def _():
            load_x1(b + 1).start()
        for c in range(n_out - 2, n_out):
            store(c, c % 2).wait()


def _moe(idx, gate, h2t, x1t, mod3, wg, wu, wd, bsz, n):
    cap = idx.shape[-1]
    sub = min(256, cap)
    gl = min(LANES, cap)
    assert n // sub >= 2 and cap % sub == 0 and sub % SCATTER_UNROLL == 0 and sub % gl == 0
    assert cap // gl <= SUBLANES
    tok_rows = n * FEATURE_CHUNKS
    weight_spec = pl.BlockSpec((1, D_MODEL, D_MODEL), lambda b, e: (e, 0, 0))
    smem_spec = pl.BlockSpec((1, 1, 1, cap), lambda b, e: (b, e, 0, 0),
                             memory_space=pltpu.SMEM)
    return pl.pallas_call(
        functools.partial(_moe_kernel, n=n, cap=cap, sub=sub),
        out_shape=jax.ShapeDtypeStruct((bsz * n, D_MODEL), F32),
        grid=(bsz, N_EXPERTS),
        in_specs=[
            smem_spec,
            pl.BlockSpec((1, 1, cap // gl, gl), lambda b, e: (b, e, 0, 0)),
            pl.BlockSpec(memory_space=pl.ANY),
            pl.BlockSpec(memory_space=pl.ANY),
            pl.BlockSpec((1, N_MOD, D_MODEL), lambda b, e: (b, 0, 0)),
            weight_spec,
            weight_spec,
            weight_spec,
        ],
        out_specs=pl.BlockSpec(memory_space=pl.ANY),
        scratch_shapes=[
            pltpu.VMEM((tok_rows, LANES), F32),
            pltpu.VMEM((tok_rows, LANES), F32),
            pltpu.VMEM((cap // sub, sub * FEATURE_CHUNKS, LANES), F32),
            pltpu.VMEM((cap // sub, sub * FEATURE_CHUNKS, LANES), F32),
            pltpu.VMEM((2, sub, D_MODEL), F32),
            pltpu.SemaphoreType.DMA((4,)),
        ],
        compiler_params=pltpu.CompilerParams(
            dimension_semantics=("arbitrary", "arbitrary"), vmem_limit_bytes=VMEM_LIMIT_BYTES),
        name="moe",
    )(idx[:, :, None, :], gate.reshape(bsz, N_EXPERTS, cap // gl, gl), h2t, x1t, mod3, wg, wu, wd)


def _rope_tables(n):
    t = jnp.arange(n)
    row = (t // GRID_W).astype(F32)
    col = (t % GRID_W).astype(F32)
    n_freq = HEAD_DIM // 4
    inv_freq = ROPE_BASE ** (-jnp.arange(n_freq, dtype=F32) / n_freq)
    ang_r = row[:, None] * inv_freq
    ang_c = col[:, None] * inv_freq
    cos = jnp.concatenate([jnp.cos(ang_r)] * 2 + [jnp.cos(ang_c)] * 2, axis=1)
    sin = jnp.concatenate([-jnp.sin(ang_r), jnp.sin(ang_r), -jnp.sin(ang_c), jnp.sin(ang_c)],
                          axis=1)
    return jnp.tile(cos, (1, 2)), jnp.tile(sin, (1, 2))


def _pack_gate_weights(w_r, b_r, w_i, b_i):
    n_p = LRU_WIDTH // LANES
    zeros = jnp.zeros((LRU_BLOCK_DIM, LRU_BLOCK_DIM), F32)

    def pair(w, p):
        top = jnp.concatenate([w[2 * p], zeros], axis=1)
        bot = jnp.concatenate([zeros, w[2 * p + 1]], axis=1)
        return jnp.concatenate([top, bot], axis=0)

    ws, bs = [], []
    for p in range(n_p):
        ws.append(jnp.concatenate(
            [pair(w_r[0], p), pair(w_i[0], p), pair(w_r[1], p), pair(w_i[1], p)], axis=1))
        sl = slice(p * LANES, (p + 1) * LANES)
        bs.append(jnp.concatenate([b_r[0, sl], b_i[0, sl], b_r[1, sl], b_i[1, sl]])[None, :])
    return jnp.stack(ws).astype(BF16), 0.5 * jnp.stack(bs)


def _layer(x, ctx, mod3, norm1_g, norm2_g, w_in, q_norm_g, k_norm_g, attn_sink, conv_w, conv_b,
           lru_w_r, lru_b_r, lru_w_i, lru_b_i, lru_lambda, w_out, w_router, w_gate, w_up, w_down):
    bsz, n, _ = x.shape
    n_ctx = ctx.shape[1]
    x2 = x.reshape(bsz * n, D_MODEL)
    c2 = ctx.reshape(bsz * n_ctx, D_MODEL)
    g1 = norm1_g[None, :]
    g2 = norm2_g[None, :]
    qg = jnp.tile(q_norm_g, 2)[None, :]
    kg = jnp.tile(k_norm_g, 2)[None, :]
    cos, sin = _rope_tables(n)
    w_in_b = w_in.astype(BF16)
    w_ctx_b = w_in_b[:, ATTN_WIDTH:ATTN_WIDTH + 2 * KV_WIDTH + LRU_WIDTH]

    q_s, k2, v2, xr2, gg2 = _inproj_latent(x2, mod3, g1, w_in_b, qg, kg, cos, sin, bsz, n)
    kc2, vc2, xrc2 = _inproj_ctx(c2, mod3, g1, w_ctx_b, kg, bsz, n_ctx)

    o_s = _attn(attn_sink, q_s, k2.reshape(bsz, n, KV_WIDTH), v2.reshape(bsz, n, KV_WIDTH),
                kc2.reshape(bsz, n_ctx, KV_WIDTH), vc2.reshape(bsz, n_ctx, KV_WIDTH))

    wg_lru, bg_lru = _pack_gate_weights(lru_w_r, lru_b_r, lru_w_i, lru_b_i)
    rg3 = _lru(xr2.reshape(bsz, n, LRU_WIDTH), xrc2.reshape(bsz, n_ctx, LRU_WIDTH),
               gg2.reshape(bsz, n, LRU_WIDTH), 0.5 * conv_w, 0.5 * conv_b[None, :], wg_lru,
               bg_lru, lru_lambda)

    w_att = (w_out[:ATTN_WIDTH].reshape(N_KV_HEADS, Q_PER_KV, HEAD_DIM, D_MODEL)
             .transpose(1, 0, 2, 3).reshape(ATTN_WIDTH, D_MODEL).astype(BF16))
    w_rnn = w_out[ATTN_WIDTH:].astype(BF16)
    x1t, h2t, lg4 = _outproj(o_s, rg3.reshape(bsz * n, LRU_WIDTH), x2, mod3, w_att, w_rnn, g2,
                             w_router.T.astype(BF16), bsz, n)

    cap = CAPACITY_FACTOR * n // N_EXPERTS
    idx, gate = _topk(lg4, cap)
    out = _moe(idx, gate, h2t, x1t, mod3, w_gate.astype(BF16), w_up.astype(BF16),
               w_down.astype(BF16), bsz, n)
    return out.reshape(bsz, n, D_MODEL)


def kernel(x, c, ctx, c_ctx, w_ada, b_ada, norm1_g, norm2_g, w_in, q_norm_g, k_norm_g, attn_sink, conv_w, conv_b, lru_w_r, lru_b_r, lru_w_i, lru_b_i, lru_lambda, w_out, w_router, w_gate, w_up, w_down):
    bsz = x.shape[0]
    depth = w_ada.shape[0]
    assert depth == 1, "context-stream update between layers is not implemented"
    pad_rows = -(bsz + 1) % SUBLANES
    cc = jnp.concatenate([c, jnp.zeros((pad_rows, D_MODEL), F32), c_ctx[None, :]], axis=0)
    for layer in range(depth):
        mod = _ada(cc, w_ada[layer], b_ada[layer][None, :])
        mod3 = mod.reshape(cc.shape[0], N_MOD, D_MODEL)
        x = _layer(x, ctx, mod3, norm1_g[layer], norm2_g[layer], w_in[layer], q_norm_g[layer],
                   k_norm_g[layer], attn_sink[layer], conv_w[layer], conv_b[layer],
                   lru_w_r[layer], lru_b_r[layer], lru_w_i[layer], lru_b_i[layer],
                   lru_lambda[layer], w_out[layer], w_router[layer], w_gate[layer],
                   w_up[layer], w_down[layer])
    return x
```

```python
import functools

import jax
import jax.numpy as jnp
from jax import lax
from jax.experimental import pallas as pl
from jax.experimental.pallas import tpu as pltpu

F32 = jnp.float32
BF16 = jnp.bfloat16

LANES = 128
SUBLANES = 8
VMEM_LIMIT_BYTES = 56 * 1024 * 1024

D_MODEL = 1024
HEAD_DIM = 64
N_Q_HEADS = 8
N_KV_HEADS = 2
Q_PER_KV = N_Q_HEADS // N_KV_HEADS
ATTN_WIDTH = N_Q_HEADS * HEAD_DIM
KV_WIDTH = N_KV_HEADS * HEAD_DIM
LRU_WIDTH = D_MODEL - ATTN_WIDTH
LRU_BLOCK_DIM = 64
IN_WIDTH = ATTN_WIDTH + 2 * KV_WIDTH + 2 * LRU_WIDTH
WINDOW = 128
BLOCK = 128
GRID_W = 64
ROPE_BASE = 10000.0
LRU_C = 8.0
N_EXPERTS = 16
CAPACITY_FACTOR = 2
EPS = 1e-6
NEG_INF = -1e30
LOG2_E = 1.4426950408889634
SCORE_SCALE = HEAD_DIM ** -0.5 * LOG2_E
N_MOD = 6
FEATURE_CHUNKS = D_MODEL // LANES
SCATTER_UNROLL = 16


def _dot(a, b):
    return jnp.dot(a, b, preferred_element_type=F32)


def _dot_nt(a, b):
    return lax.dot_general(a, b, (((1,), (1,)), ((), ())), preferred_element_type=F32)


def _dot_tn(a, b):
    return lax.dot_general(a, b, (((0,), (0,)), ((), ())), preferred_element_type=F32)


def _split_bf16(x, parts):
    out = []
    r = x
    for _ in range(parts):
        p = r.astype(BF16)
        out.append(p)
        r = r - p.astype(F32)
    return out


def _rmsnorm_rows(x, g):
    ms = jnp.mean(x * x, axis=-1, keepdims=True)
    return x * lax.rsqrt(ms + EPS) * g


def _ada_kernel(c_ref, w_ref, b_ref, o_ref):
    s = jax.nn.silu(c_ref[...])
    s_hi, s_lo = _split_bf16(s, 2)
    w_hi, w_lo = _split_bf16(w_ref[...], 2)
    o_ref[...] = _dot(s_hi, w_hi) + _dot(s_hi, w_lo) + _dot(s_lo, w_hi) + b_ref[...]


def _ada(cc, w_ada, b_ada):
    rows = cc.shape[0]
    width = w_ada.shape[1]
    return pl.pallas_call(
        _ada_kernel,
        out_shape=jax.ShapeDtypeStruct((rows, width), F32),
        grid=(width // D_MODEL,),
        in_specs=[
            pl.BlockSpec((rows, D_MODEL), lambda j: (0, 0)),
            pl.BlockSpec((D_MODEL, D_MODEL), lambda j: (0, j)),
            pl.BlockSpec((1, D_MODEL), lambda j: (0, j)),
        ],
        out_specs=pl.BlockSpec((rows, D_MODEL), lambda j: (0, j)),
        compiler_params=pltpu.CompilerParams(
            dimension_semantics=("arbitrary",), vmem_limit_bytes=VMEM_LIMIT_BYTES),
        name="ada",
    )(cc, w_ada, b_ada)


def _head_norm(p, g):
    lane = lax.broadcasted_iota(jnp.int32, p.shape, 1)
    low = lane < HEAD_DIM
    sq = p * p
    s_low = jnp.sum(jnp.where(low, sq, 0.0), axis=-1, keepdims=True)
    s_high = jnp.sum(jnp.where(low, 0.0, sq), axis=-1, keepdims=True)
    ms = jnp.where(low, s_low, s_high) * (1.0 / HEAD_DIM)
    return p * lax.rsqrt(ms + EPS) * g


def _rope(p, cos, sin_signed):
    lane = lax.broadcasted_iota(jnp.int32, p.shape, 1)
    partner = jnp.where((lane & 16) == 0,
                        pltpu.roll(p, LANES - 16, 1), pltpu.roll(p, 16, 1))
    return p * cos + partner * sin_signed


def _inproj_latent_kernel(x_ref, mod_ref, g1_ref, w_ref, qg_ref, kg_ref, cos_ref, sin_ref,
                          q_ref, k_ref, v_ref, xr_ref, gg_ref):
    x = x_ref[...]
    h = _rmsnorm_rows(x, g1_ref[...]) * (1.0 + mod_ref[0, 1:2, :]) + mod_ref[0, 0:1, :]
    y = _dot(h.astype(BF16), w_ref[...])
    cos = cos_ref[...]
    sin = sin_ref[...]
    pieces = []
    for j in range(ATTN_WIDTH // LANES):
        p = _head_norm(y[:, j * LANES:(j + 1) * LANES], qg_ref[...])
        pieces.append(_rope(p, cos, sin))
    lane = lax.broadcasted_iota(jnp.int32, pieces[0].shape, 1)
    low = lane < HEAD_DIM
    n_blocks = y.shape[0] // BLOCK
    for g in range(Q_PER_KV):
        src0 = pieces[g // 2]
        src1 = pieces[2 + g // 2]
        if g % 2 == 0:
            out = jnp.where(low, src0, pltpu.roll(src1, HEAD_DIM, 1))
        else:
            out = jnp.where(low, pltpu.roll(src0, HEAD_DIM, 1), src1)
        out = (out * SCORE_SCALE).astype(BF16)
        for j in range(n_blocks):
            q_ref[0, j, g * BLOCK:(g + 1) * BLOCK, :] = out[j * BLOCK:(j + 1) * BLOCK, :]
    k = _head_norm(y[:, ATTN_WIDTH:ATTN_WIDTH + KV_WIDTH], kg_ref[...])
    k_ref[...] = _rope(k, cos, sin).astype(BF16)
    o = ATTN_WIDTH + KV_WIDTH
    v_ref[...] = y[:, o:o + KV_WIDTH].astype(BF16)
    o += KV_WIDTH
    xr_ref[...] = y[:, o:o + LRU_WIDTH]
    o += LRU_WIDTH
    gg_ref[...] = jax.nn.gelu(y[:, o:o + LRU_WIDTH])


def _inproj_ctx_kernel(x_ref, mod_ref, g1_ref, w_ref, kg_ref, k_ref, v_ref, xr_ref):
    x = x_ref[...]
    h = _rmsnorm_rows(x, g1_ref[...]) * (1.0 + mod_ref[0, 1:2, :]) + mod_ref[0, 0:1, :]
    y = _dot(h.astype(BF16), w_ref[...])
    k_ref[...] = _head_norm(y[:, 0:KV_WIDTH], kg_ref[...]).astype(BF16)
    v_ref[...] = y[:, KV_WIDTH:2 * KV_WIDTH].astype(BF16)
    xr_ref[...] = y[:, 2 * KV_WIDTH:2 * KV_WIDTH + LRU_WIDTH]


def _row_tile(n):
    return min(512, n)


def _inproj_latent(x2, mod3, g1, w_in, qg, kg, cos, sin, bsz, n):
    tm = _row_tile(n)
    tps = n // tm
    rows = bsz * n
    const = lambda i: (0, 0)
    return pl.pallas_call(
        _inproj_latent_kernel,
        out_shape=(
            jax.ShapeDtypeStruct((bsz, n // BLOCK, Q_PER_KV * BLOCK, LANES), BF16),
            jax.ShapeDtypeStruct((rows, KV_WIDTH), BF16),
            jax.ShapeDtypeStruct((rows, KV_WIDTH), BF16),
            jax.ShapeDtypeStruct((rows, LRU_WIDTH), F32),
            jax.ShapeDtypeStruct((rows, LRU_WIDTH), F32),
        ),
        grid=(rows // tm,),
        in_specs=[
            pl.BlockSpec((tm, D_MODEL), lambda i: (i, 0)),
            pl.BlockSpec((1, N_MOD, D_MODEL), lambda i: (i // tps, 0, 0)),
            pl.BlockSpec((1, D_MODEL), const),
            pl.BlockSpec((D_MODEL, IN_WIDTH), const),
            pl.BlockSpec((1, LANES), const),
            pl.BlockSpec((1, LANES), const),
            pl.BlockSpec((tm, LANES), lambda i: (i % tps, 0)),
            pl.BlockSpec((tm, LANES), lambda i: (i % tps, 0)),
        ],
        out_specs=(
            pl.BlockSpec((1, tm // BLOCK, Q_PER_KV * BLOCK, LANES),
                         lambda i: (i // tps, i % tps, 0, 0)),
            pl.BlockSpec((tm, KV_WIDTH), lambda i: (i, 0)),
            pl.BlockSpec((tm, KV_WIDTH), lambda i: (i, 0)),
            pl.BlockSpec((tm, LRU_WIDTH), lambda i: (i, 0)),
            pl.BlockSpec((tm, LRU_WIDTH), lambda i: (i, 0)),
        ),
        compiler_params=pltpu.CompilerParams(
            dimension_semantics=("arbitrary",), vmem_limit_bytes=VMEM_LIMIT_BYTES),
        name="inproj_latent",
    )(x2, mod3, g1, w_in, qg, kg, cos, sin)


def _inproj_ctx(c2, mod3, g1, w_ctx, kg, bsz, n_ctx):
    tm = _row_tile(n_ctx)
    rows = bsz * n_ctx
    ctx_row = mod3.shape[0] - 1
    width = w_ctx.shape[1]
    const = lambda i: (0, 0)
    return pl.pallas_call(
        _inproj_ctx_kernel,
        out_shape=(
            jax.ShapeDtypeStruct((rows, KV_WIDTH), BF16),
            jax.ShapeDtypeStruct((rows, KV_WIDTH), BF16),
            jax.ShapeDtypeStruct((rows, LRU_WIDTH), F32),
        ),
        grid=(rows // tm,),
        in_specs=[
            pl.BlockSpec((tm, D_MODEL), lambda i: (i, 0)),
            pl.BlockSpec((1, N_MOD, D_MODEL), lambda i: (ctx_row, 0, 0)),
            pl.BlockSpec((1, D_MODEL), const),
            pl.BlockSpec((D_MODEL, width), const),
            pl.BlockSpec((1, LANES), const),
        ],
        out_specs=(
            pl.BlockSpec((tm, KV_WIDTH), lambda i: (i, 0)),
            pl.BlockSpec((tm, KV_WIDTH), lambda i: (i, 0)),
            pl.BlockSpec((tm, LRU_WIDTH), lambda i: (i, 0)),
        ),
        compiler_params=pltpu.CompilerParams(
            dimension_semantics=("arbitrary",), vmem_limit_bytes=VMEM_LIMIT_BYTES),
        name="inproj_ctx",
    )(c2, mod3, g1, w_ctx, kg)


def _attn_kernel(sink_ref, q_ref, k_ref, v_ref, kc_ref, vc_ref, bias_ref, o_ref, s0_s, s1_s,
                 *, n, nb):
    t = pl.program_id(0)
    last = pl.num_programs(0) - 2
    j_scored = jnp.minimum(t, last) % nb
    j_finished = jnp.maximum(t - 1, 0) % nb
    span = 3 * BLOCK

    def window_start(i):
        return pl.multiple_of(jnp.clip((i - 1) * BLOCK, 0, n - span), BLOCK)

    def head_only(t, h):
        lane = lax.broadcasted_iota(jnp.int32, t.shape, 1)
        keep = (lane < HEAD_DIM) if h == 0 else (lane >= HEAD_DIM)
        return jnp.where(keep, t, jnp.zeros_like(t))

    stages = functools.partial(_attn_stages, sink_ref, q_ref, k_ref, v_ref, kc_ref, vc_ref,
                               bias_ref, o_ref, start_scored=window_start(j_scored),
                               start_finished=window_start(j_finished), head_only=head_only)

    @pl.when(t == 0)
    def _():
        s1_s[...] = jnp.zeros_like(s1_s)

    @pl.when(t % 2 == 0)
    def _():
        stages(s0_s, s1_s)

    @pl.when(t % 2 == 1)
    def _():
        stages(s1_s, s0_s)


def _attn_stages(sink_ref, q_ref, k_ref, v_ref, kc_ref, vc_ref, bias_ref, o_ref, s_new, s_old,
                 start_scored, start_finished, head_only):
    span = 3 * BLOCK
    q = q_ref[0, 0]
    kw = k_ref[0, pl.ds(start_scored, span), :]
    kc = kc_ref[0]
    bias = bias_ref[0]
    for h in range(N_KV_HEADS):
        s_new[h, :, 0:span] = _dot_nt(q, head_only(kw, h)) + bias
        s_new[h, :, span:] = _dot_nt(q, head_only(kc, h))

    vw = v_ref[0, pl.ds(start_finished, span), :]
    vc = vc_ref[0]
    rows = q.shape[0]
    group = lax.broadcasted_iota(jnp.int32, (rows, 1), 0) // BLOCK
    acc = jnp.zeros((rows, LANES), F32)
    for h in range(N_KV_HEADS):
        s_loc = s_old[h, :, 0:span]
        s_ctx = s_old[h, :, span:]
        sink = jnp.zeros((rows, 1), F32)
        for g in range(Q_PER_KV):
            sink = jnp.where(group == g, sink_ref[h * Q_PER_KV + g] * LOG2_E, sink)
        m = jnp.maximum(jnp.maximum(jnp.max(s_loc, axis=-1, keepdims=True),
                                    jnp.max(s_ctx, axis=-1, keepdims=True)), sink)
        p_loc = jnp.exp2(s_loc - m)
        p_ctx = jnp.exp2(s_ctx - m)
        denom = (jnp.sum(p_loc, axis=-1, keepdims=True)
                 + jnp.sum(p_ctx, axis=-1, keepdims=True) + jnp.exp2(sink - m))
        o = (_dot(p_loc.astype(BF16), head_only(vw, h))
             + _dot(p_ctx.astype(BF16), head_only(vc, h)))
        acc = acc + o / denom
    o_ref[0, 0] = acc.astype(BF16)


def _band_bias(n):
    span = 3 * BLOCK
    r = jnp.arange(Q_PER_KV * BLOCK)[:, None] % BLOCK
    c = jnp.arange(span)[None, :]
    offsets = jnp.arange(span // BLOCK)[:, None, None] * BLOCK
    valid = jnp.abs(r - c + offsets) <= WINDOW
    return jnp.where(valid, 0.0, NEG_INF).astype(F32)


def _attn(sink, q_s, k3, v3, kc3, vc3):
    bsz, nb = q_s.shape[0], q_s.shape[1]
    n = k3.shape[1]
    n_ctx = kc3.shape[1]
    assert nb >= 3
    bias = _band_bias(n)

    n_blocks = bsz * nb

    def scored(t):
        return jnp.divmod(jnp.minimum(t, n_blocks - 1), nb)

    def finished(t):
        return jnp.divmod(jnp.maximum(t - 1, 0), nb)

    def placement(t):
        i = scored(t)[1]
        return (jnp.minimum(i, 1) + (i == nb - 1).astype(jnp.int32), 0, 0)

    block = (1, 1, Q_PER_KV * BLOCK, LANES)

    return pl.pallas_call(
        functools.partial(_attn_kernel, n=n, nb=nb),
        out_shape=jax.ShapeDtypeStruct(q_s.shape, BF16),
        grid=(n_blocks + 1,),
        in_specs=[
            pl.BlockSpec(memory_space=pltpu.SMEM),
            pl.BlockSpec(block, lambda t: (*scored(t), 0, 0)),
            pl.BlockSpec((1, n, KV_WIDTH), lambda t: (scored(t)[0], 0, 0)),
            pl.BlockSpec((1, n, KV_WIDTH), lambda t: (finished(t)[0], 0, 0)),
            pl.BlockSpec((1, n_ctx, KV_WIDTH), lambda t: (scored(t)[0], 0, 0)),
            pl.BlockSpec((1, n_ctx, KV_WIDTH), lambda t: (finished(t)[0], 0, 0)),
            pl.BlockSpec((1, Q_PER_KV * BLOCK, 3 * BLOCK), placement),
        ],
        out_specs=pl.BlockSpec(block, lambda t: (*finished(t), 0, 0)),
        scratch_shapes=[
            pltpu.VMEM((N_KV_HEADS, Q_PER_KV * BLOCK, 3 * BLOCK + n_ctx), F32),
            pltpu.VMEM((N_KV_HEADS, Q_PER_KV * BLOCK, 3 * BLOCK + n_ctx), F32),
        ],
        compiler_params=pltpu.CompilerParams(
            dimension_semantics=("arbitrary",), vmem_limit_bytes=VMEM_LIMIT_BYTES),
        name="attn",
    )(sink, q_s, k3, v3, kc3, vc3, bias)


def _softplus(z):
    return jnp.maximum(z, 0.0) + jnp.log1p(jnp.exp(-jnp.abs(z)))


LRU_MAIN_SEG = 36


def _lru_plan(length):
    main, rest = divmod(length // SUBLANES, LRU_MAIN_SEG)
    if rest == 0:
        tails = ()
    elif rest % SUBLANES == 0 and rest > SUBLANES:
        tails = (rest - 5, 5)
    else:
        tails = (rest,)
    assert length % SUBLANES == 0 and all(t >= 2 for t in tails)
    return main, tails


def _lru_kernel(xr_ref, xrc_ref, gg_ref, cw_ref, cb_ref, wg_ref, bg_ref, lam_ref, o_ref,
                xc_s, xcc_s, hf_s, hb_s, *, n, n_ctx):
    cw = cw_ref[...]
    cb = cb_ref[...]
    sub = lax.broadcasted_iota(jnp.int32, (SUBLANES, LANES), 0)
    main_rows = SUBLANES * LRU_MAIN_SEG
    n_main, tails = _lru_plan(n)
    c_main, c_tails = _lru_plan(n_ctx)

    def static_tiles(n_main_tiles, tail_segs, with_main):
        tiles = [(k * main_rows, LRU_MAIN_SEG) for k in range(n_main_tiles)] if with_main else []
        t0 = n_main_tiles * main_rows
        for seg in tail_segs:
            tiles.append((t0, seg))
            t0 += SUBLANES * seg
        return tiles

    def aligned(v):
        return v if isinstance(v, int) else pl.multiple_of(v, SUBLANES)

    def conv_tile(src_ref, dst_ref, t0, seg, length):
        x = [src_ref[0, pl.ds(t0 + j, SUBLANES, stride=seg), :] for j in range(seg)]
        lo = aligned(jnp.maximum(t0 - SUBLANES, 0))
        hi = aligned(jnp.minimum(t0 + SUBLANES * seg, length - SUBLANES))
        prev = jnp.where(t0 > 0, src_ref[0, pl.ds(lo, SUBLANES), :], 0.0)
        nxt = jnp.where(t0 + SUBLANES * seg < length, src_ref[0, pl.ds(hi, SUBLANES), :], 0.0)

        def from_prev_segment(v, row):
            return jnp.where(sub == 0, row, pltpu.roll(v, 1, 0))

        def from_next_segment(v, row):
            return jnp.where(sub == SUBLANES - 1, row, pltpu.roll(v, SUBLANES - 1, 0))

        m1 = [from_prev_segment(x[seg - 1], prev[7:8])] + x[:seg - 1]
        m2 = [from_prev_segment(x[seg - 2], prev[6:7])] + m1[:seg - 1]
        p1 = x[1:] + [from_next_segment(x[0], nxt[0:1])]
        for j in range(seg):
            y = cb + cw[0:1] * m2[j]
            y = y + cw[1:2] * m1[j]
            y = y + cw[2:3] * x[j]
            y = y + cw[3:4] * p1[j]
            dst_ref[pl.ds(t0 + j * SUBLANES, SUBLANES), :] = y

    half_rate = [-0.5 * LRU_C * _softplus(-lam_ref[d:d + 1, :]) for d in range(2)]

    def coeffs(xc, d):
        w = wg_ref[0, :, 2 * LANES * d:2 * LANES * (d + 1)]
        g = _dot(xc.astype(BF16), w) + bg_ref[0, :, 2 * LANES * d:2 * LANES * (d + 1)]
        t_r = jnp.tanh(g[:, :LANES])
        t_i = jnp.tanh(g[:, LANES:])
        log_a = t_r * half_rate[d] + half_rate[d]
        a = jnp.exp(log_a)
        mult = jnp.sqrt(-jnp.tanh(log_a) * (a * a + 1.0))
        return a, mult * (t_i + 1.0) * xc

    def scan_tile(a, b, h_in, seg, reverse):
        order = list(range(seg - 1, -1, -1) if reverse else range(seg))
        local = [None] * seg
        prod = [None] * seg
        h = p = None
        for j in order:
            aj = a[j * SUBLANES:(j + 1) * SUBLANES, :]
            bj = b[j * SUBLANES:(j + 1) * SUBLANES, :]
            h = bj if h is None else aj * h + bj
            p = aj if p is None else aj * p
            local[j], prod[j] = h, p
        end = order[-1]
        carry = jnp.zeros((SUBLANES, LANES), F32)
        c = h_in
        for s in (range(SUBLANES - 1, -1, -1) if reverse else range(SUBLANES)):
            carry = jnp.where(sub == s, c, carry)
            c = prod[end][s:s + 1, :] * c + local[end][s:s + 1, :]
        return [local[j] + prod[j] * carry for j in range(seg)], c

    def run_tile(xc_ref, t0, seg, d, h, emit=None):
        a, b = coeffs(xc_ref[pl.ds(t0, SUBLANES * seg), :], d)
        states, h = scan_tile(a, b, h, seg, d == 1)
        if emit is not None:
            emit(t0, seg, states)
        return h

    def keep(dst_ref):
        def emit(t0, seg, states):
            for j in range(seg):
                dst_ref[pl.ds(t0 + j * SUBLANES, SUBLANES), :] = states[j]
        return emit

    def write_output(other_ref):
        def emit(t0, seg, states):
            for j in range(seg):
                rows = pl.ds(t0 + j, SUBLANES, stride=seg)
                other = other_ref[pl.ds(t0 + j * SUBLANES, SUBLANES), :]
                o_ref[0, rows, :] = (other + states[j]) * gg_ref[0, rows, :]
        return emit

    def main_t0(k):
        return pl.multiple_of(k * main_rows, SUBLANES)

    for t0, seg in static_tiles(c_main, c_tails, True):
        conv_tile(xrc_ref, xcc_s, t0, seg, n_ctx)

    def conv_body(k, carry):
        conv_tile(xr_ref, xc_s, main_t0(k), LRU_MAIN_SEG, n)
        return carry
    lax.fori_loop(0, n_main, conv_body, 0)
    for t0, seg in static_tiles(n_main, tails, False):
        conv_tile(xr_ref, xc_s, t0, seg, n)

    hf = jnp.zeros((1, LANES), F32)
    hb = jnp.zeros((1, LANES), F32)
    for t0, seg in static_tiles(c_main, c_tails, True):
        hf = run_tile(xcc_s, t0, seg, 0, hf)
    for t0, seg in reversed(static_tiles(c_main, c_tails, True)):
        hb = run_tile(xcc_s, t0, seg, 1, hb)
    for t0, seg in reversed(static_tiles(n_main, tails, False)):
        hb = run_tile(xc_s, t0, seg, 1, hb, keep(hb_s))

    def pair_body(first_visit):
        def body(k, carry):
            hf, hb = carry
            fwd_emit = keep(hf_s) if first_visit else write_output(hb_s)
            rev_emit = keep(hb_s) if first_visit else write_output(hf_s)
            hf = run_tile(xc_s, main_t0(k), LRU_MAIN_SEG, 0, hf, fwd_emit)
            hb = run_tile(xc_s, main_t0(n_main - 1 - k), LRU_MAIN_SEG, 1, hb, rev_emit)
            return hf, hb
        return body

    half = n_main // 2
    hf, hb = lax.fori_loop(0, half, pair_body(True), (hf, hb))
    if n_main % 2:
        hf = run_tile(xc_s, half * main_rows, LRU_MAIN_SEG, 0, hf, keep(hf_s))
        hb = run_tile(xc_s, half * main_rows, LRU_MAIN_SEG, 1, hb, write_output(hf_s))
    hf, hb = lax.fori_loop(n_main - half, n_main, pair_body(False), (hf, hb))
    for t0, seg in static_tiles(n_main, tails, False):
        hf = run_tile(xc_s, t0, seg, 0, hf, write_output(hb_s))


def _lru(xr3, xrc3, gg3, conv_w, conv_b, wg, bg, lam):
    bsz, n, _ = xr3.shape
    n_ctx = xrc3.shape[1]
    n_p = LRU_WIDTH // LANES
    return pl.pallas_call(
        functools.partial(_lru_kernel, n=n, n_ctx=n_ctx),
        out_shape=jax.ShapeDtypeStruct((bsz, n, LRU_WIDTH), F32),
        grid=(bsz, n_p),
        in_specs=[
            pl.BlockSpec((1, n, LANES), lambda b, p: (b, 0, p)),
            pl.BlockSpec((1, n_ctx, LANES), lambda b, p: (b, 0, p)),
            pl.BlockSpec((1, n, LANES), lambda b, p: (b, 0, p)),
            pl.BlockSpec((conv_w.shape[0], LANES), lambda b, p: (0, p)),
            pl.BlockSpec((1, LANES), lambda b, p: (0, p)),
            pl.BlockSpec((1, LANES, 4 * LANES), lambda b, p: (p, 0, 0)),
            pl.BlockSpec((1, 1, 4 * LANES), lambda b, p: (p, 0, 0)),
            pl.BlockSpec((2, LANES), lambda b, p: (0, p)),
        ],
        out_specs=pl.BlockSpec((1, n, LANES), lambda b, p: (b, 0, p)),
        scratch_shapes=[
            pltpu.VMEM((n, LANES), F32),
            pltpu.VMEM((n_ctx, LANES), F32),
            pltpu.VMEM((n, LANES), F32),
            pltpu.VMEM((n, LANES), F32),
        ],
        compiler_params=pltpu.CompilerParams(
            dimension_semantics=("arbitrary", "arbitrary"), vmem_limit_bytes=VMEM_LIMIT_BYTES),
        name="lru",
    )(xr3, xrc3, gg3, conv_w, conv_b, wg, bg, lam)


def _outproj_kernel(o_ref, rg_ref, x_ref, mod_ref, watt_ref, wrnn_ref, g2_ref, wr_ref,
                    x1t_ref, h2t_ref, lg_ref):
    tm = x_ref.shape[0]
    n_blocks = tm // BLOCK
    att = jnp.concatenate(
        [jnp.concatenate([o_ref[0, j, g * BLOCK:(g + 1) * BLOCK, :] for g in range(Q_PER_KV)],
                         axis=1) for j in range(n_blocks)], axis=0)
    y = _dot(att, watt_ref[...]) + _dot(rg_ref[...].astype(BF16), wrnn_ref[...])
    x1 = x_ref[...] + mod_ref[0, 2:3, :] * y
    h2 = _rmsnorm_rows(x1, g2_ref[...]) * (1.0 + mod_ref[0, 4:5, :]) + mod_ref[0, 3:4, :]
    for s in range(FEATURE_CHUNKS):
        rows = pl.ds(s, tm, stride=FEATURE_CHUNKS)
        x1t_ref[rows, :] = x1[:, s * LANES:(s + 1) * LANES]
        h2t_ref[rows, :] = h2[:, s * LANES:(s + 1) * LANES]
    lt = _dot_nt(wr_ref[...], h2.astype(BF16))
    for j in range(n_blocks):
        lg_ref[0, j] = lt[:, j * LANES:(j + 1) * LANES]


def _outproj(o_s, rg2, x2, mod3, w_att, w_rnn, g2, wr_t, bsz, n):
    tm = _row_tile(n)
    tps = n // tm
    rows = bsz * n
    const = lambda i: (0, 0)
    return pl.pallas_call(
        _outproj_kernel,
        out_shape=(
            jax.ShapeDtypeStruct((rows * FEATURE_CHUNKS, LANES), F32),
            jax.ShapeDtypeStruct((rows * FEATURE_CHUNKS, LANES), F32),
            jax.ShapeDtypeStruct((bsz, n // LANES, N_EXPERTS, LANES), F32),
        ),
        grid=(rows // tm,),
        in_specs=[
            pl.BlockSpec((1, tm // BLOCK, Q_PER_KV * BLOCK, LANES),
                         lambda i: (i // tps, i % tps, 0, 0)),
            pl.BlockSpec((tm, LRU_WIDTH), lambda i: (i, 0)),
            pl.BlockSpec((tm, D_MODEL), lambda i: (i, 0)),
            pl.BlockSpec((1, N_MOD, D_MODEL), lambda i: (i // tps, 0, 0)),
            pl.BlockSpec((ATTN_WIDTH, D_MODEL), const),
            pl.BlockSpec((LRU_WIDTH, D_MODEL), const),
            pl.BlockSpec((1, D_MODEL), const),
            pl.BlockSpec((N_EXPERTS, D_MODEL), const),
        ],
        out_specs=(
            pl.BlockSpec((tm * FEATURE_CHUNKS, LANES), lambda i: (i, 0)),
            pl.BlockSpec((tm * FEATURE_CHUNKS, LANES), lambda i: (i, 0)),
            pl.BlockSpec((1, tm // LANES, N_EXPERTS, LANES), lambda i: (i // tps, i % tps, 0, 0)),
        ),
        compiler_params=pltpu.CompilerParams(
            dimension_semantics=("arbitrary",), vmem_limit_bytes=VMEM_LIMIT_BYTES),
        name="outproj",
    )(o_s, rg2, x2, mod3, w_att, w_rnn, g2, wr_t)


def _topk_kernel(lg_ref, idx_ref, gate_ref, aff_s, cum_s, before_s, *, cap):
    lg = lg_ref[0]
    n_chunks = lg.shape[0]
    m = jnp.max(lg, axis=1, keepdims=True)
    ex = jnp.exp(lg - m)
    aff = ex / jnp.sum(ex, axis=1, keepdims=True)
    aff_s[...] = aff

    def count(mask):
        c = jnp.sum(jnp.where(mask, 1.0, 0.0), axis=0, keepdims=True)
        return jnp.sum(c, axis=2, keepdims=True)

    def bit_body(it, thr):
        cand = thr | jnp.left_shift(jnp.int32(1), 30 - it)
        return jnp.where(count(aff >= pltpu.bitcast(cand, F32)) >= cap, cand, thr)
    thr = lax.fori_loop(0, 31, bit_body, jnp.zeros((1, N_EXPERTS, 1), jnp.int32))
    thr = pltpu.bitcast(thr, F32)

    tri = (lax.broadcasted_iota(jnp.int32, (LANES, LANES), 0)
           <= lax.broadcasted_iota(jnp.int32, (LANES, LANES), 1))
    tri = jnp.where(tri, 1.0, 0.0).astype(BF16)

    def chunk_cumsum(mask):
        flat = jnp.where(mask, 1.0, 0.0).astype(BF16).reshape(n_chunks * N_EXPERTS, LANES)
        inc = _dot(flat, tri).reshape(n_chunks, N_EXPERTS, LANES)
        run = jnp.zeros((N_EXPERTS, 1), F32)
        before = []
        for c in range(n_chunks):
            before.append(run)
            run = run + inc[c][:, LANES - 1:LANES]
        return inc, jnp.stack(before, axis=0)

    above = aff > thr
    tied = aff == thr
    need = cap - count(above)
    tie_inc, tie_before = chunk_cumsum(tied)
    tie_rank = tie_before + tie_inc - jnp.where(tied, 1.0, 0.0)
    sel = above | (tied & (tie_rank < need))
    sel_inc, sel_before = chunk_cumsum(sel)
    cum_s[...] = sel_inc
    before_s[...] = jnp.broadcast_to(sel_before, sel_inc.shape)

    slot = lax.broadcasted_iota(jnp.int32, (1, cap), 1).astype(F32)
    chunk_id = lax.broadcasted_iota(jnp.int32, (n_chunks, cap), 0).astype(F32)
    lane_id = lax.broadcasted_iota(jnp.int32, (LANES, cap), 0).astype(F32)
    for e in range(N_EXPERTS):
        cin = cum_s[:, e, :]
        cc_exc = before_s[:, e, :][:, 0:1]
        cc_inc = cc_exc + cin[:, LANES - 1:LANES]
        kc = jnp.sum(jnp.where(cc_inc <= slot, 1.0, 0.0), axis=0, keepdims=True)
        onehot = chunk_id == kc
        onehot_b = jnp.where(onehot, 1.0, 0.0).astype(BF16)
        counts_t = _dot_tn(cin.astype(BF16), onehot_b)
        local = slot - jnp.sum(jnp.where(onehot, cc_exc, 0.0), axis=0, keepdims=True)
        il = jnp.sum(jnp.where(counts_t <= local, 1.0, 0.0), axis=0, keepdims=True)
        idx_ref[0, e:e + 1, :] = ((kc * LANES + il) * FEATURE_CHUNKS).astype(jnp.int32)
        aff_t = jnp.zeros((LANES, cap), F32)
        for part in _split_bf16(aff_s[:, e, :], 3):
            aff_t = aff_t + _dot_tn(part, onehot_b)
        gate_ref[0, e:e + 1, :] = jnp.sum(jnp.where(lane_id == il, aff_t, 0.0),
                                          axis=0, keepdims=True)


def _topk(lg4, cap):
    bsz, n_chunks = lg4.shape[0], lg4.shape[1]
    return pl.pallas_call(
        functools.partial(_topk_kernel, cap=cap),
        out_shape=(
            jax.ShapeDtypeStruct((bsz, N_EXPERTS, cap), jnp.int32),
            jax.ShapeDtypeStruct((bsz, N_EXPERTS, cap), F32),
        ),
        grid=(bsz,),
        in_specs=[pl.BlockSpec((1, n_chunks, N_EXPERTS, LANES), lambda b: (b, 0, 0, 0))],
        out_specs=(
            pl.BlockSpec((1, N_EXPERTS, cap), lambda b: (b, 0, 0)),
            pl.BlockSpec((1, N_EXPERTS, cap), lambda b: (b, 0, 0)),
        ),
        scratch_shapes=[
            pltpu.VMEM((n_chunks, N_EXPERTS, LANES), F32),
            pltpu.VMEM((n_chunks, N_EXPERTS, LANES), F32),
            pltpu.VMEM((n_chunks, N_EXPERTS, LANES), F32),
        ],
        compiler_params=pltpu.CompilerParams(
            dimension_semantics=("arbitrary",), vmem_limit_bytes=VMEM_LIMIT_BYTES),
        name="topk",
    )(lg4)


def _moe_kernel(idx_ref, gate_ref, h2t_hbm, x1t_hbm, mod_ref, wg_ref, wu_ref, wd_ref, out_hbm,
                h2_s, acc_s, xs_s, ys_s, stage_s, sems, *, n, cap, sub):
    b = pl.program_id(0)
    e = pl.program_id(1)
    n_b = pl.num_programs(0)
    n_e = pl.num_programs(1)
    tok_rows = n * FEATURE_CHUNKS
    n_parts = cap // sub

    def load_h2(sample):
        return pltpu.make_async_copy(
            h2t_hbm.at[pl.ds(sample * tok_rows, tok_rows), :], h2_s, sems.at[0])

    def load_x1(sample):
        return pltpu.make_async_copy(
            x1t_hbm.at[pl.ds(sample * tok_rows, tok_rows), :], acc_s, sems.at[1])

    def load_x1_chunk(sample, c):
        rows = pl.ds(pl.multiple_of(c * (sub * FEATURE_CHUNKS), sub * FEATURE_CHUNKS),
                     sub * FEATURE_CHUNKS)
        return pltpu.make_async_copy(
            x1t_hbm.at[pl.ds(sample * tok_rows + c * (sub * FEATURE_CHUNKS),
                             sub * FEATURE_CHUNKS), :], acc_s.at[rows, :], sems.at[1])

    gates = gate_ref[0, 0]
    gl = gates.shape[1]
    gates = jnp.pad(gates, ((0, SUBLANES - gates.shape[0]), (0, LANES - gl)))
    gate_cols = gates.T
    gate2 = mod_ref[0, N_MOD - 1:N_MOD, :]

    def token_tile(first_row):
        return pl.ds(pl.multiple_of(first_row, FEATURE_CHUNKS), FEATURE_CHUNKS)

    def gather(part):
        for r in range(sub):
            t = idx_ref[0, 0, 0, part * sub + r]
            xs_s[part, r * FEATURE_CHUNKS:(r + 1) * FEATURE_CHUNKS, :] = h2_s[token_tile(t), :]

    def expert(part):
        x = jnp.concatenate(
            [xs_s[part, pl.ds(s, sub, stride=FEATURE_CHUNKS), :] for s in range(FEATURE_CHUNKS)],
            axis=1).astype(BF16)
        hid = (jax.nn.silu(_dot(x, wg_ref[0])) * _dot(x, wu_ref[0])).astype(BF16)
        y = _dot(hid, wd_ref[0]) * gate2
        first_col = part * sub // gl
        y = jnp.concatenate(
            [y[k * gl:(k + 1) * gl, :] * gate_cols[:gl, first_col + k:first_col + k + 1]
             for k in range(sub // gl)], axis=0)
        for s in range(FEATURE_CHUNKS):
            ys_s[part, pl.ds(s, sub, stride=FEATURE_CHUNKS), :] = y[:, s * LANES:(s + 1) * LANES]

    def scatter(part):
        for r0 in range(0, sub, SCATTER_UNROLL):
            updates = []
            for r in range(r0, r0 + SCATTER_UNROLL):
                dst = token_tile(idx_ref[0, 0, 0, part * sub + r])
                row = ys_s[part, r * FEATURE_CHUNKS:(r + 1) * FEATURE_CHUNKS, :]
                updates.append((dst, acc_s[dst, :] + row))
            for dst, val in updates:
                acc_s[dst, :] = val

    @pl.when((e == 0) & (b == 0))
    def _():
        load_h2(b).start()
        load_x1(b).start()

    @pl.when(e == 0)
    def _():
        load_h2(b).wait()

    for part in range(n_parts):
        gather(part)

    @pl.when((e == n_e - 1) & (b + 1 < n_b))
    def _():
        load_h2(b + 1).start()

    expert(0)

    @pl.when(e == 0)
    def _():
        load_x1(b).wait()

    for part in range(1, n_parts):
        expert(part)
    for part in range(n_parts):
        scatter(part)

    @pl.when(e == n_e - 1)
    def _():
        n_out = n // sub

        def store(c, slot):
            return pltpu.make_async_copy(
                stage_s.at[slot], out_hbm.at[pl.ds(b * n + c * sub, sub), :], sems.at[2 + slot])

        def out_body(c, carry):
            slot = c % 2

            @pl.when(c >= 2)
            def _():
                store(c - 2, slot).wait()
            row0 = pl.multiple_of(c * (sub * FEATURE_CHUNKS), sub * FEATURE_CHUNKS)
            for s in range(FEATURE_CHUNKS):
                stage_s[slot, :, s * LANES:(s + 1) * LANES] = (
                    acc_s[pl.ds(row0 + s, sub, stride=FEATURE_CHUNKS), :])
            store(c, slot).start()

            @pl.when(b + 1 < n_b)
            def _():
                load_x1_chunk(b + 1, c).start()
            return carry
        lax.fori_loop(0, n_out, out_body, 0)
        for c in range(n_out - 2, n_out):
            store(c, c % 2).wait()


def _moe(idx, gate, h2t, x1t, mod3, wg, wu, wd, bsz, n):
    cap = idx.shape[-1]
    sub = min(256, cap)
    gl = min(LANES, cap)
    assert n // sub >= 2 and cap % sub == 0 and sub % SCATTER_UNROLL == 0 and sub % gl == 0
    assert cap // gl <= SUBLANES
    tok_rows = n * FEATURE_CHUNKS
    weight_spec = pl.BlockSpec((1, D_MODEL, D_MODEL), lambda b, e: (e, 0, 0))
    smem_spec = pl.BlockSpec((1, 1, 1, cap), lambda b, e: (b, e, 0, 0),
                             memory_space=pltpu.SMEM)
    return pl.pallas_call(
        functools.partial(_moe_kernel, n=n, cap=cap, sub=sub),
        out_shape=jax.ShapeDtypeStruct((bsz * n, D_MODEL), F32),
        grid=(bsz, N_EXPERTS),
        in_specs=[
            smem_spec,
            pl.BlockSpec((1, 1, cap // gl, gl), lambda b, e: (b, e, 0, 0)),
            pl.BlockSpec(memory_space=pl.ANY),
            pl.BlockSpec(memory_space=pl.ANY),
            pl.BlockSpec((1, N_MOD, D_MODEL), lambda b, e: (b, 0, 0)),
            weight_spec,
            weight_spec,
            weight_spec,
        ],
        out_specs=pl.BlockSpec(memory_space=pl.ANY),
        scratch_shapes=[
            pltpu.VMEM((tok_rows, LANES), F32),
            pltpu.VMEM((tok_rows, LANES), F32),
            pltpu.VMEM((cap // sub, sub * FEATURE_CHUNKS, LANES), F32),
            pltpu.VMEM((cap // sub, sub * FEATURE_CHUNKS, LANES), F32),
            pltpu.VMEM((2, sub, D_MODEL), F32),
            pltpu.SemaphoreType.DMA((4,)),
        ],
        compiler_params=pltpu.CompilerParams(
            dimension_semantics=("arbitrary", "arbitrary"), vmem_limit_bytes=VMEM_LIMIT_BYTES),
        name="moe",
    )(idx[:, :, None, :], gate.reshape(bsz, N_EXPERTS, cap // gl, gl), h2t, x1t, mod3, wg, wu, wd)


def _rope_tables(n):
    t = jnp.arange(n)
    row = (t // GRID_W).astype(F32)
    col = (t % GRID_W).astype(F32)
    n_freq = HEAD_DIM // 4
    inv_freq = ROPE_BASE ** (-jnp.arange(n_freq, dtype=F32) / n_freq)
    ang_r = row[:, None] * inv_freq
    ang_c = col[:, None] * inv_freq
    cos = jnp.concatenate([jnp.cos(ang_r)] * 2 + [jnp.cos(ang_c)] * 2, axis=1)
    sin = jnp.concatenate([-jnp.sin(ang_r), jnp.sin(ang_r), -jnp.sin(ang_c), jnp.sin(ang_c)],
                          axis=1)
    return jnp.tile(cos, (1, 2)), jnp.tile(sin, (1, 2))


def _pack_gate_weights(w_r, b_r, w_i, b_i):
    n_p = LRU_WIDTH // LANES
    zeros = jnp.zeros((LRU_BLOCK_DIM, LRU_BLOCK_DIM), F32)

    def pair(w, p):
        top = jnp.concatenate([w[2 * p], zeros], axis=1)
        bot = jnp.concatenate([zeros, w[2 * p + 1]], axis=1)
        return jnp.concatenate([top, bot], axis=0)

    ws, bs = [], []
    for p in range(n_p):
        ws.append(jnp.concatenate(
            [pair(w_r[0], p), pair(w_i[0], p), pair(w_r[1], p), pair(w_i[1], p)], axis=1))
        sl = slice(p * LANES, (p + 1) * LANES)
        bs.append(jnp.concatenate([b_r[0, sl], b_i[0, sl], b_r[1, sl], b_i[1, sl]])[None, :])
    return jnp.stack(ws).astype(BF16), 0.5 * jnp.stack(bs)


def _layer(x, ctx, mod3, norm1_g, norm2_g, w_in, q_norm_g, k_norm_g, attn_sink, conv_w, conv_b,
           lru_w_r, lru_b_r, lru_w_i, lru_b_i, lru_lambda, w_out, w_router, w_gate, w_up, w_down):
    bsz, n, _ = x.shape
    n_ctx = ctx.shape[1]
    x2 = x.reshape(bsz * n, D_MODEL)
    c2 = ctx.reshape(bsz * n_ctx, D_MODEL)
    g1 = norm1_g[None, :]
    g2 = norm2_g[None, :]
    qg = jnp.tile(q_norm_g, 2)[None, :]
    kg = jnp.tile(k_norm_g, 2)[None, :]
    cos, sin = _rope_tables(n)
    w_in_b = w_in.astype(BF16)
    w_ctx_b = w_in_b[:, ATTN_WIDTH:ATTN_WIDTH + 2 * KV_WIDTH + LRU_WIDTH]

    q_s, k2, v2, xr2, gg2 = _inproj_latent(x2, mod3, g1, w_in_b, qg, kg, cos, sin, bsz, n)
    kc2, vc2, xrc2 = _inproj_ctx(c2, mod3, g1, w_ctx_b, kg, bsz, n_ctx)

    o_s = _attn(attn_sink, q_s, k2.reshape(bsz, n, KV_WIDTH), v2.reshape(bsz, n, KV_WIDTH),
                kc2.reshape(bsz, n_ctx, KV_WIDTH), vc2.reshape(bsz, n_ctx, KV_WIDTH))

    wg_lru, bg_lru = _pack_gate_weights(lru_w_r, lru_b_r, lru_w_i, lru_b_i)
    rg3 = _lru(xr2.reshape(bsz, n, LRU_WIDTH), xrc2.reshape(bsz, n_ctx, LRU_WIDTH),
               gg2.reshape(bsz, n, LRU_WIDTH), 0.5 * conv_w, 0.5 * conv_b[None, :], wg_lru,
               bg_lru, lru_lambda)

    w_att = (w_out[:ATTN_WIDTH].reshape(N_KV_HEADS, Q_PER_KV, HEAD_DIM, D_MODEL)
             .transpose(1, 0, 2, 3).reshape(ATTN_WIDTH, D_MODEL).astype(BF16))
    w_rnn = w_out[ATTN_WIDTH:].astype(BF16)
    x1t, h2t, lg4 = _outproj(o_s, rg3.reshape(bsz * n, LRU_WIDTH), x2, mod3, w_att, w_rnn, g2,
                             w_router.T.astype(BF16), bsz, n)

    cap = CAPACITY_FACTOR * n // N_EXPERTS
    idx, gate = _topk(lg4, cap)
    out = _moe(idx, gate, h2t, x1t, mod3, w_gate.astype(BF16), w_up.astype(BF16),
               w_down.astype(BF16), bsz, n)
    return out.reshape(bsz, n, D_MODEL)


def kernel(x, c, ctx, c_ctx, w_ada, b_ada, norm1_g, norm2_g, w_in, q_norm_g, k_norm_g, attn_sink, conv_w, conv_b, lru_w_r, lru_b_r, lru_w_i, lru_b_i, lru_lambda, w_out, w_router, w_gate, w_up, w_down):
    bsz = x.shape[0]
    depth = w_ada.shape[0]
    assert depth == 1, "context-stream update between layers is not implemented"
    pad_rows = -(bsz + 1) % SUBLANES
    cc = jnp.concatenate([c, jnp.zeros((pad_rows, D_MODEL), F32), c_ctx[None, :]], axis=0)
    for layer in range(depth):
        mod = _ada(cc, w_ada[layer], b_ada[layer][None, :])
        mod3 = mod.reshape(cc.shape[0], N_MOD, D_MODEL)
        x = _layer(x, ctx, mod3, norm1_g[layer], norm2_g[layer], w_in[layer], q_norm_g[layer],
                   k_norm_g[layer], attn_sink[layer], conv_w[layer], conv_b[layer],
                   lru_w_r[layer], lru_b_r[layer], lru_w_i[layer], lru_b_i[layer],
                   lru_lambda[layer], w_out[layer], w_router[layer], w_gate[layer],
                   w_up[layer], w_down[layer])
    return x
```

```python
import functools

import jax
import jax.numpy as jnp
from jax import lax
from jax.experimental import pallas as pl
from jax.experimental.pallas import tpu as pltpu

F32 = jnp.float32
BF16 = jnp.bfloat16

LANES = 128
SUBLANES = 8
VMEM_LIMIT_BYTES = 56 * 1024 * 1024

D_MODEL = 1024
HEAD_DIM = 64
N_Q_HEADS = 8
N_KV_HEADS = 2
Q_PER_KV = N_Q_HEADS // N_KV_HEADS
ATTN_WIDTH = N_Q_HEADS * HEAD_DIM
KV_WIDTH = N_KV_HEADS * HEAD_DIM
LRU_WIDTH = D_MODEL - ATTN_WIDTH
LRU_BLOCK_DIM = 64
IN_WIDTH = ATTN_WIDTH + 2 * KV_WIDTH + 2 * LRU_WIDTH
WINDOW = 128
BLOCK = 128
GRID_W = 64
ROPE_BASE = 10000.0
LRU_C = 8.0
N_EXPERTS = 16
CAPACITY_FACTOR = 2
EPS = 1e-6
NEG_INF = -1e30
LOG2_E = 1.4426950408889634
SCORE_SCALE = HEAD_DIM ** -0.5 * LOG2_E
N_MOD = 6
FEATURE_CHUNKS = D_MODEL // LANES
SCATTER_UNROLL = 16


def _dot(a, b):
    return jnp.dot(a, b, preferred_element_type=F32)


def _dot_nt(a, b):
    return lax.dot_general(a, b, (((1,), (1,)), ((), ())), preferred_element_type=F32)


def _dot_tn(a, b):
    return lax.dot_general(a, b, (((0,), (0,)), ((), ())), preferred_element_type=F32)


def _split_bf16(x, parts):
    out = []
    r = x
    for _ in range(parts):
        p = r.astype(BF16)
        out.append(p)
        r = r - p.astype(F32)
    return out


def _rmsnorm_rows(x, g):
    ms = jnp.mean(x * x, axis=-1, keepdims=True)
    return x * lax.rsqrt(ms + EPS) * g


def _ada_kernel(c_ref, w_ref, b_ref, o_ref):
    s = jax.nn.silu(c_ref[...])
    s_hi, s_lo = _split_bf16(s, 2)
    w_hi, w_lo = _split_bf16(w_ref[...], 2)
    o_ref[...] = _dot(s_hi, w_hi) + _dot(s_hi, w_lo) + _dot(s_lo, w_hi) + b_ref[...]


def _ada(cc, w_ada, b_ada):
    rows = cc.shape[0]
    width = w_ada.shape[1]
    return pl.pallas_call(
        _ada_kernel,
        out_shape=jax.ShapeDtypeStruct((rows, width), F32),
        grid=(width // D_MODEL,),
        in_specs=[
            pl.BlockSpec((rows, D_MODEL), lambda j: (0, 0)),
            pl.BlockSpec((D_MODEL, D_MODEL), lambda j: (0, j)),
            pl.BlockSpec((1, D_MODEL), lambda j: (0, j)),
        ],
        out_specs=pl.BlockSpec((rows, D_MODEL), lambda j: (0, j)),
        compiler_params=pltpu.CompilerParams(
            dimension_semantics=("arbitrary",), vmem_limit_bytes=VMEM_LIMIT_BYTES),
        name="ada",
    )(cc, w_ada, b_ada)


def _head_norm(p, g):
    lane = lax.broadcasted_iota(jnp.int32, p.shape, 1)
    low = lane < HEAD_DIM
    sq = p * p
    s_low = jnp.sum(jnp.where(low, sq, 0.0), axis=-1, keepdims=True)
    s_high = jnp.sum(jnp.where(low, 0.0, sq), axis=-1, keepdims=True)
    ms = jnp.where(low, s_low, s_high) * (1.0 / HEAD_DIM)
    return p * lax.rsqrt(ms + EPS) * g


def _rope(p, cos, sin_signed):
    lane = lax.broadcasted_iota(jnp.int32, p.shape, 1)
    partner = jnp.where((lane & 16) == 0,
                        pltpu.roll(p, LANES - 16, 1), pltpu.roll(p, 16, 1))
    return p * cos + partner * sin_signed


def _inproj_latent_kernel(x_ref, mod_ref, g1_ref, w_ref, qg_ref, kg_ref, cos_ref, sin_ref,
                          q_ref, k_ref, v_ref, xr_ref, gg_ref):
    x = x_ref[...]
    h = _rmsnorm_rows(x, g1_ref[...]) * (1.0 + mod_ref[0, 1:2, :]) + mod_ref[0, 0:1, :]
    y = _dot(h.astype(BF16), w_ref[...])
    cos = cos_ref[...]
    sin = sin_ref[...]
    pieces = []
    for j in range(ATTN_WIDTH // LANES):
        p = _head_norm(y[:, j * LANES:(j + 1) * LANES], qg_ref[...])
        pieces.append(_rope(p, cos, sin))
    lane = lax.broadcasted_iota(jnp.int32, pieces[0].shape, 1)
    low = lane < HEAD_DIM
    n_blocks = y.shape[0] // BLOCK
    for g in range(Q_PER_KV):
        src0 = pieces[g // 2]
        src1 = pieces[2 + g // 2]
        if g % 2 == 0:
            out = jnp.where(low, src0, pltpu.roll(src1, HEAD_DIM, 1))
        else:
            out = jnp.where(low, pltpu.roll(src0, HEAD_DIM, 1), src1)
        out = (out * SCORE_SCALE).astype(BF16)
        for j in range(n_blocks):
            q_ref[0, j, g * BLOCK:(g + 1) * BLOCK, :] = out[j * BLOCK:(j + 1) * BLOCK, :]
    k = _head_norm(y[:, ATTN_WIDTH:ATTN_WIDTH + KV_WIDTH], kg_ref[...])
    k_ref[...] = _rope(k, cos, sin).astype(BF16)
    o = ATTN_WIDTH + KV_WIDTH
    v_ref[...] = y[:, o:o + KV_WIDTH].astype(BF16)
    o += KV_WIDTH
    xr_ref[...] = y[:, o:o + LRU_WIDTH]
    o += LRU_WIDTH
    gg_ref[...] = jax.nn.gelu(y[:, o:o + LRU_WIDTH])


def _inproj_ctx_kernel(x_ref, mod_ref, g1_ref, w_ref, kg_ref, k_ref, v_ref, xr_ref):
    x = x_ref[...]
    h = _rmsnorm_rows(x, g1_ref[...]) * (1.0 + mod_ref[0, 1:2, :]) + mod_ref[0, 0:1, :]
    y = _dot(h.astype(BF16), w_ref[...])
    k_ref[...] = _head_norm(y[:, 0:KV_WIDTH], kg_ref[...]).astype(BF16)
    v_ref[...] = y[:, KV_WIDTH:2 * KV_WIDTH].astype(BF16)
    xr_ref[...] = y[:, 2 * KV_WIDTH:2 * KV_WIDTH + LRU_WIDTH]


OUTPROJ_ROW_TILE = 1024


def _row_tile(n):
    return min(512, n)


def _inproj_latent(x2, mod3, g1, w_in, qg, kg, cos, sin, bsz, n):
    tm = _row_tile(n)
    tps = n // tm
    rows = bsz * n
    const = lambda i: (0, 0)
    return pl.pallas_call(
        _inproj_latent_kernel,
        out_shape=(
            jax.ShapeDtypeStruct((bsz, n // BLOCK, Q_PER_KV * BLOCK, LANES), BF16),
            jax.ShapeDtypeStruct((rows, KV_WIDTH), BF16),
            jax.ShapeDtypeStruct((rows, KV_WIDTH), BF16),
            jax.ShapeDtypeStruct((rows, LRU_WIDTH), F32),
            jax.ShapeDtypeStruct((rows, LRU_WIDTH), F32),
        ),
        grid=(rows // tm,),
        in_specs=[
            pl.BlockSpec((tm, D_MODEL), lambda i: (i, 0)),
            pl.BlockSpec((1, N_MOD, D_MODEL), lambda i: (i // tps, 0, 0)),
            pl.BlockSpec((1, D_MODEL), const),
            pl.BlockSpec((D_MODEL, IN_WIDTH), const),
            pl.BlockSpec((1, LANES), const),
            pl.BlockSpec((1, LANES), const),
            pl.BlockSpec((tm, LANES), lambda i: (i % tps, 0)),
            pl.BlockSpec((tm, LANES), lambda i: (i % tps, 0)),
        ],
        out_specs=(
            pl.BlockSpec((1, tm // BLOCK, Q_PER_KV * BLOCK, LANES),
                         lambda i: (i // tps, i % tps, 0, 0)),
            pl.BlockSpec((tm, KV_WIDTH), lambda i: (i, 0)),
            pl.BlockSpec((tm, KV_WIDTH), lambda i: (i, 0)),
            pl.BlockSpec((tm, LRU_WIDTH), lambda i: (i, 0)),
            pl.BlockSpec((tm, LRU_WIDTH), lambda i: (i, 0)),
        ),
        compiler_params=pltpu.CompilerParams(
            dimension_semantics=("arbitrary",), vmem_limit_bytes=VMEM_LIMIT_BYTES),
        name="inproj_latent",
    )(x2, mod3, g1, w_in, qg, kg, cos, sin)


def _inproj_ctx(c2, mod3, g1, w_ctx, kg, bsz, n_ctx):
    tm = _row_tile(n_ctx)
    rows = bsz * n_ctx
    ctx_row = mod3.shape[0] - 1
    width = w_ctx.shape[1]
    const = lambda i: (0, 0)
    return pl.pallas_call(
        _inproj_ctx_kernel,
        out_shape=(
            jax.ShapeDtypeStruct((rows, KV_WIDTH), BF16),
            jax.ShapeDtypeStruct((rows, KV_WIDTH), BF16),
            jax.ShapeDtypeStruct((rows, LRU_WIDTH), F32),
        ),
        grid=(rows // tm,),
        in_specs=[
            pl.BlockSpec((tm, D_MODEL), lambda i: (i, 0)),
            pl.BlockSpec((1, N_MOD, D_MODEL), lambda i: (ctx_row, 0, 0)),
            pl.BlockSpec((1, D_MODEL), const),
            pl.BlockSpec((D_MODEL, width), const),
            pl.BlockSpec((1, LANES), const),
        ],
        out_specs=(
            pl.BlockSpec((tm, KV_WIDTH), lambda i: (i, 0)),
            pl.BlockSpec((tm, KV_WIDTH), lambda i: (i, 0)),
            pl.BlockSpec((tm, LRU_WIDTH), lambda i: (i, 0)),
        ),
        compiler_params=pltpu.CompilerParams(
            dimension_semantics=("arbitrary",), vmem_limit_bytes=VMEM_LIMIT_BYTES),
        name="inproj_ctx",
    )(c2, mod3, g1, w_ctx, kg)


def _attn_kernel(sink_ref, q_ref, k_ref, v_ref, kc_ref, vc_ref, bias_ref, o_ref, s0_s, s1_s,
                 *, n, nb):
    t = pl.program_id(0)
    last = pl.num_programs(0) - 2
    j_scored = jnp.minimum(t, last) % nb
    j_finished = jnp.maximum(t - 1, 0) % nb
    span = 3 * BLOCK

    def window_start(i):
        return pl.multiple_of(jnp.clip((i - 1) * BLOCK, 0, n - span), BLOCK)

    def head_only(t, h):
        lane = lax.broadcasted_iota(jnp.int32, t.shape, 1)
        keep = (lane < HEAD_DIM) if h == 0 else (lane >= HEAD_DIM)
        return jnp.where(keep, t, jnp.zeros_like(t))

    stages = functools.partial(_attn_stages, sink_ref, q_ref, k_ref, v_ref, kc_ref, vc_ref,
                               bias_ref, o_ref, start_scored=window_start(j_scored),
                               start_finished=window_start(j_finished), head_only=head_only)

    @pl.when(t == 0)
    def _():
        s1_s[...] = jnp.zeros_like(s1_s)

    @pl.when(t % 2 == 0)
    def _():
        stages(s0_s, s1_s)

    @pl.when(t % 2 == 1)
    def _():
        stages(s1_s, s0_s)


def _attn_stages(sink_ref, q_ref, k_ref, v_ref, kc_ref, vc_ref, bias_ref, o_ref, s_new, s_old,
                 start_scored, start_finished, head_only):
    span = 3 * BLOCK

    def with_ones(t, lane_index):
        lane = lax.broadcasted_iota(jnp.int32, t.shape, 1)
        return jnp.where(lane == lane_index, jnp.ones_like(t), t)

    q = q_ref[0, 0]
    kw = k_ref[0, pl.ds(start_scored, span), :]
    kc = kc_ref[0]
    bias = bias_ref[0]
    for h in range(N_KV_HEADS):
        s_new[h, :, 0:span] = _dot_nt(q, head_only(kw, h)) + bias
        s_new[h, :, span:] = _dot_nt(q, head_only(kc, h))

    vw = v_ref[0, pl.ds(start_finished, span), :]
    vc = vc_ref[0]
    rows = q.shape[0]
    group = lax.broadcasted_iota(jnp.int32, (rows, 1), 0) // BLOCK
    acc = jnp.zeros((rows, LANES), F32)
    for h in range(N_KV_HEADS):
        s_loc = s_old[h, :, 0:span]
        s_ctx = s_old[h, :, span:]
        sink = jnp.zeros((rows, 1), F32)
        for g in range(Q_PER_KV):
            sink = jnp.where(group == g, sink_ref[h * Q_PER_KV + g] * LOG2_E, sink)
        m = jnp.maximum(jnp.maximum(jnp.max(s_loc, axis=-1, keepdims=True),
                                    jnp.max(s_ctx, axis=-1, keepdims=True)), sink)
        p_loc = jnp.exp2(s_loc - m).astype(BF16)
        p_ctx = jnp.exp2(s_ctx - m).astype(BF16)
        sum_lane = HEAD_DIM if h == 0 else 0
        o = (_dot(p_loc, with_ones(head_only(vw, h), sum_lane))
             + _dot(p_ctx, with_ones(head_only(vc, h), sum_lane)))
        denom = o[:, sum_lane:sum_lane + 1] + jnp.exp2(sink - m)
        lane = lax.broadcasted_iota(jnp.int32, o.shape, 1)
        own = (lane < HEAD_DIM) if h == 0 else (lane >= HEAD_DIM)
        acc = acc + jnp.where(own, o, 0.0) / denom
    o_ref[0, 0] = acc.astype(BF16)


def _band_bias(n):
    span = 3 * BLOCK
    r = jnp.arange(Q_PER_KV * BLOCK)[:, None] % BLOCK
    c = jnp.arange(span)[None, :]
    offsets = jnp.arange(span // BLOCK)[:, None, None] * BLOCK
    valid = jnp.abs(r - c + offsets) <= WINDOW
    return jnp.where(valid, 0.0, NEG_INF).astype(F32)


def _attn(sink, q_s, k3, v3, kc3, vc3):
    bsz, nb = q_s.shape[0], q_s.shape[1]
    n = k3.shape[1]
    n_ctx = kc3.shape[1]
    assert nb >= 3
    bias = _band_bias(n)

    n_blocks = bsz * nb

    def scored(t):
        return jnp.divmod(jnp.minimum(t, n_blocks - 1), nb)

    def finished(t):
        return jnp.divmod(jnp.maximum(t - 1, 0), nb)

    def placement(t):
        i = scored(t)[1]
        return (jnp.minimum(i, 1) + (i == nb - 1).astype(jnp.int32), 0, 0)

    block = (1, 1, Q_PER_KV * BLOCK, LANES)

    return pl.pallas_call(
        functools.partial(_attn_kernel, n=n, nb=nb),
        out_shape=jax.ShapeDtypeStruct(q_s.shape, BF16),
        grid=(n_blocks + 1,),
        in_specs=[
            pl.BlockSpec(memory_space=pltpu.SMEM),
            pl.BlockSpec(block, lambda t: (*scored(t), 0, 0)),
            pl.BlockSpec((1, n, KV_WIDTH), lambda t: (scored(t)[0], 0, 0)),
            pl.BlockSpec((1, n, KV_WIDTH), lambda t: (finished(t)[0], 0, 0)),
            pl.BlockSpec((1, n_ctx, KV_WIDTH), lambda t: (scored(t)[0], 0, 0)),
            pl.BlockSpec((1, n_ctx, KV_WIDTH), lambda t: (finished(t)[0], 0, 0)),
            pl.BlockSpec((1, Q_PER_KV * BLOCK, 3 * BLOCK), placement),
        ],
        out_specs=pl.BlockSpec(block, lambda t: (*finished(t), 0, 0)),
        scratch_shapes=[
            pltpu.VMEM((N_KV_HEADS, Q_PER_KV * BLOCK, 3 * BLOCK + n_ctx), F32),
            pltpu.VMEM((N_KV_HEADS, Q_PER_KV * BLOCK, 3 * BLOCK + n_ctx), F32),
        ],
        compiler_params=pltpu.CompilerParams(
            dimension_semantics=("arbitrary",), vmem_limit_bytes=VMEM_LIMIT_BYTES),
        name="attn",
    )(sink, q_s, k3, v3, kc3, vc3, bias)


def _softplus(z):
    return jnp.maximum(z, 0.0) + jnp.log1p(jnp.exp(-jnp.abs(z)))


LRU_MAIN_SEG = 36


def _lru_plan(length):
    main, rest = divmod(length // SUBLANES, LRU_MAIN_SEG)
    if rest == 0:
        tails = ()
    elif rest % SUBLANES == 0 and rest > SUBLANES:
        tails = (rest - 5, 5)
    else:
        tails = (rest,)
    assert length % SUBLANES == 0 and all(t >= 2 for t in tails)
    return main, tails


def _lru_kernel(xr_ref, xrc_ref, gg_ref, cw_ref, cb_ref, wg_ref, bg_ref, lam_ref, o_ref,
                xc_s, xcc_s, hf_s, hb_s, *, n, n_ctx):
    cw = cw_ref[...]
    cb = cb_ref[...]
    sub = lax.broadcasted_iota(jnp.int32, (SUBLANES, LANES), 0)
    main_rows = SUBLANES * LRU_MAIN_SEG
    n_main, tails = _lru_plan(n)
    c_main, c_tails = _lru_plan(n_ctx)

    def static_tiles(n_main_tiles, tail_segs, with_main):
        tiles = [(k * main_rows, LRU_MAIN_SEG) for k in range(n_main_tiles)] if with_main else []
        t0 = n_main_tiles * main_rows
        for seg in tail_segs:
            tiles.append((t0, seg))
            t0 += SUBLANES * seg
        return tiles

    def aligned(v):
        return v if isinstance(v, int) else pl.multiple_of(v, SUBLANES)

    def conv_tile(src_ref, dst_ref, t0, seg, length):
        x = [src_ref[0, pl.ds(t0 + j, SUBLANES, stride=seg), :] for j in range(seg)]
        lo = aligned(jnp.maximum(t0 - SUBLANES, 0))
        hi = aligned(jnp.minimum(t0 + SUBLANES * seg, length - SUBLANES))
        prev = jnp.where(t0 > 0, src_ref[0, pl.ds(lo, SUBLANES), :], 0.0)
        nxt = jnp.where(t0 + SUBLANES * seg < length, src_ref[0, pl.ds(hi, SUBLANES), :], 0.0)

        def from_prev_segment(v, row):
            return jnp.where(sub == 0, row, pltpu.roll(v, 1, 0))

        def from_next_segment(v, row):
            return jnp.where(sub == SUBLANES - 1, row, pltpu.roll(v, SUBLANES - 1, 0))

        m1 = [from_prev_segment(x[seg - 1], prev[7:8])] + x[:seg - 1]
        m2 = [from_prev_segment(x[seg - 2], prev[6:7])] + m1[:seg - 1]
        p1 = x[1:] + [from_next_segment(x[0], nxt[0:1])]
        for j in range(seg):
            y = cb + cw[0:1] * m2[j]
            y = y + cw[1:2] * m1[j]
            y = y + cw[2:3] * x[j]
            y = y + cw[3:4] * p1[j]
            dst_ref[pl.ds(t0 + j * SUBLANES, SUBLANES), :] = y

    half_rate = [-0.5 * LRU_C * _softplus(-lam_ref[d:d + 1, :]) for d in range(2)]

    def coeffs(xc, d):
        w = wg_ref[0, :, 2 * LANES * d:2 * LANES * (d + 1)]
        g = _dot(xc.astype(BF16), w) + bg_ref[0, :, 2 * LANES * d:2 * LANES * (d + 1)]
        t_r = jnp.tanh(g[:, :LANES])
        t_i = jnp.tanh(g[:, LANES:])
        log_a = t_r * half_rate[d] + half_rate[d]
        a = jnp.exp(log_a)
        mult = jnp.sqrt(-jnp.tanh(log_a) * (a * a + 1.0))
        return a, mult * (t_i + 1.0) * xc

    def scan_tile(a, b, h_in, seg, reverse):
        order = list(range(seg - 1, -1, -1) if reverse else range(seg))
        local = [None] * seg
        prod = [None] * seg
        h = p = None
        for j in order:
            aj = a[j * SUBLANES:(j + 1) * SUBLANES, :]
            bj = b[j * SUBLANES:(j + 1) * SUBLANES, :]
            h = bj if h is None else aj * h + bj
            p = aj if p is None else aj * p
            local[j], prod[j] = h, p
        end = order[-1]
        carry = jnp.zeros((SUBLANES, LANES), F32)
        c = h_in
        for s in (range(SUBLANES - 1, -1, -1) if reverse else range(SUBLANES)):
            carry = jnp.where(sub == s, c, carry)
            c = prod[end][s:s + 1, :] * c + local[end][s:s + 1, :]
        return [local[j] + prod[j] * carry for j in range(seg)], c

    def run_tile(xc_ref, t0, seg, d, h, emit=None):
        a, b = coeffs(xc_ref[pl.ds(t0, SUBLANES * seg), :], d)
        states, h = scan_tile(a, b, h, seg, d == 1)
        if emit is not None:
            emit(t0, seg, states)
        return h

    def keep(dst_ref):
        def emit(t0, seg, states):
            for j in range(seg):
                dst_ref[pl.ds(t0 + j * SUBLANES, SUBLANES), :] = states[j]
        return emit

    def write_output(other_ref):
        def emit(t0, seg, states):
            for j in range(seg):
                rows = pl.ds(t0 + j, SUBLANES, stride=seg)
                other = other_ref[pl.ds(t0 + j * SUBLANES, SUBLANES), :]
                o_ref[0, rows, :] = (other + states[j]) * gg_ref[0, rows, :]
        return emit

    def main_t0(k):
        return pl.multiple_of(k * main_rows, SUBLANES)

    for t0, seg in static_tiles(c_main, c_tails, True):
        conv_tile(xrc_ref, xcc_s, t0, seg, n_ctx)

    def conv_body(k, carry):
        conv_tile(xr_ref, xc_s, main_t0(k), LRU_MAIN_SEG, n)
        return carry
    lax.fori_loop(0, n_main, conv_body, 0)
    for t0, seg in static_tiles(n_main, tails, False):
        conv_tile(xr_ref, xc_s, t0, seg, n)

    hf = jnp.zeros((1, LANES), F32)
    hb = jnp.zeros((1, LANES), F32)
    for t0, seg in static_tiles(c_main, c_tails, True):
        hf = run_tile(xcc_s, t0, seg, 0, hf)
    for t0, seg in reversed(static_tiles(c_main, c_tails, True)):
        hb = run_tile(xcc_s, t0, seg, 1, hb)
    for t0, seg in reversed(static_tiles(n_main, tails, False)):
        hb = run_tile(xc_s, t0, seg, 1, hb, keep(hb_s))

    def pair_body(first_visit):
        def body(k, carry):
            hf, hb = carry
            fwd_emit = keep(hf_s) if first_visit else write_output(hb_s)
            rev_emit = keep(hb_s) if first_visit else write_output(hf_s)
            hf = run_tile(xc_s, main_t0(k), LRU_MAIN_SEG, 0, hf, fwd_emit)
            hb = run_tile(xc_s, main_t0(n_main - 1 - k), LRU_MAIN_SEG, 1, hb, rev_emit)
            return hf, hb
        return body

    half = n_main // 2
    hf, hb = lax.fori_loop(0, half, pair_body(True), (hf, hb))
    if n_main % 2:
        hf = run_tile(xc_s, half * main_rows, LRU_MAIN_SEG, 0, hf, keep(hf_s))
        hb = run_tile(xc_s, half * main_rows, LRU_MAIN_SEG, 1, hb, write_output(hf_s))
    hf, hb = lax.fori_loop(n_main - half, n_main, pair_body(False), (hf, hb))
    for t0, seg in static_tiles(n_main, tails, False):
        hf = run_tile(xc_s, t0, seg, 0, hf, write_output(hb_s))


def _lru(xr3, xrc3, gg3, conv_w, conv_b, wg, bg, lam):
    bsz, n, _ = xr3.shape
    n_ctx = xrc3.shape[1]
    n_p = LRU_WIDTH // LANES
    return pl.pallas_call(
        functools.partial(_lru_kernel, n=n, n_ctx=n_ctx),
        out_shape=jax.ShapeDtypeStruct((bsz, n, LRU_WIDTH), F32),
        grid=(bsz, n_p),
        in_specs=[
            pl.BlockSpec((1, n, LANES), lambda b, p: (b, 0, p)),
            pl.BlockSpec((1, n_ctx, LANES), lambda b, p: (b, 0, p)),
            pl.BlockSpec((1, n, LANES), lambda b, p: (b, 0, p)),
            pl.BlockSpec((conv_w.shape[0], LANES), lambda b, p: (0, p)),
            pl.BlockSpec((1, LANES), lambda b, p: (0, p)),
            pl.BlockSpec((1, LANES, 4 * LANES), lambda b, p: (p, 0, 0)),
            pl.BlockSpec((1, 1, 4 * LANES), lambda b, p: (p, 0, 0)),
            pl.BlockSpec((2, LANES), lambda b, p: (0, p)),
        ],
        out_specs=pl.BlockSpec((1, n, LANES), lambda b, p: (b, 0, p)),
        scratch_shapes=[
            pltpu.VMEM((n, LANES), F32),
            pltpu.VMEM((n_ctx, LANES), F32),
            pltpu.VMEM((n, LANES), F32),
            pltpu.VMEM((n, LANES), F32),
        ],
        compiler_params=pltpu.CompilerParams(
            dimension_semantics=("arbitrary", "arbitrary"), vmem_limit_bytes=VMEM_LIMIT_BYTES),
        name="lru",
    )(xr3, xrc3, gg3, conv_w, conv_b, wg, bg, lam)


def _outproj_kernel(o_ref, rg_ref, x_ref, mod_ref, watt_ref, wrnn_ref, g2_ref, wr_ref,
                    x1t_ref, h2t_ref, lg_ref):
    tm = x_ref.shape[0]
    n_blocks = tm // BLOCK
    att = jnp.concatenate(
        [jnp.concatenate([o_ref[0, j, g * BLOCK:(g + 1) * BLOCK, :] for g in range(Q_PER_KV)],
                         axis=1) for j in range(n_blocks)], axis=0)
    y = _dot(att, watt_ref[...]) + _dot(rg_ref[...].astype(BF16), wrnn_ref[...])
    x1 = x_ref[...] + mod_ref[0, 2:3, :] * y
    h2 = _rmsnorm_rows(x1, g2_ref[...]) * (1.0 + mod_ref[0, 4:5, :]) + mod_ref[0, 3:4, :]
    for s in range(FEATURE_CHUNKS):
        rows = pl.ds(s, tm, stride=FEATURE_CHUNKS)
        x1t_ref[rows, :] = x1[:, s * LANES:(s + 1) * LANES]
        h2t_ref[rows, :] = h2[:, s * LANES:(s + 1) * LANES]
    lt = _dot_nt(wr_ref[...], h2.astype(BF16))
    for j in range(n_blocks):
        lg_ref[0, j] = lt[:, j * LANES:(j + 1) * LANES]


def _outproj(o_s, rg2, x2, mod3, w_att, w_rnn, g2, wr_t, bsz, n):
    tm = min(OUTPROJ_ROW_TILE, n)
    tps = n // tm
    rows = bsz * n
    const = lambda i: (0, 0)
    return pl.pallas_call(
        _outproj_kernel,
        out_shape=(
            jax.ShapeDtypeStruct((rows * FEATURE_CHUNKS, LANES), F32),
            jax.ShapeDtypeStruct((rows * FEATURE_CHUNKS, LANES), F32),
            jax.ShapeDtypeStruct((bsz, n // LANES, N_EXPERTS, LANES), F32),
        ),
        grid=(rows // tm,),
        in_specs=[
            pl.BlockSpec((1, tm // BLOCK, Q_PER_KV * BLOCK, LANES),
                         lambda i: (i // tps, i % tps, 0, 0)),
            pl.BlockSpec((tm, LRU_WIDTH), lambda i: (i, 0)),
            pl.BlockSpec((tm, D_MODEL), lambda i: (i, 0)),
            pl.BlockSpec((1, N_MOD, D_MODEL), lambda i: (i // tps, 0, 0)),
            pl.BlockSpec((ATTN_WIDTH, D_MODEL), const),
            pl.BlockSpec((LRU_WIDTH, D_MODEL), const),
            pl.BlockSpec((1, D_MODEL), const),
            pl.BlockSpec((N_EXPERTS, D_MODEL), const),
        ],
        out_specs=(
            pl.BlockSpec((tm * FEATURE_CHUNKS, LANES), lambda i: (i, 0)),
            pl.BlockSpec((tm * FEATURE_CHUNKS, LANES), lambda i: (i, 0)),
            pl.BlockSpec((1, tm // LANES, N_EXPERTS, LANES), lambda i: (i // tps, i % tps, 0, 0)),
        ),
        compiler_params=pltpu.CompilerParams(
            dimension_semantics=("arbitrary",), vmem_limit_bytes=VMEM_LIMIT_BYTES),
        name="outproj",
    )(o_s, rg2, x2, mod3, w_att, w_rnn, g2, wr_t)


def _topk_kernel(lg_ref, idx_ref, gate_ref, aff_s, cum_s, before_s, *, cap):
    lg = lg_ref[0]
    n_chunks = lg.shape[0]
    m = jnp.max(lg, axis=1, keepdims=True)
    ex = jnp.exp(lg - m)
    aff = ex / jnp.sum(ex, axis=1, keepdims=True)
    aff_s[...] = aff

    def count(mask):
        c = jnp.sum(jnp.where(mask, 1.0, 0.0), axis=0, keepdims=True)
        return jnp.sum(c, axis=2, keepdims=True)

    def bit_body(it, thr):
        cand = thr | jnp.left_shift(jnp.int32(1), 30 - it)
        return jnp.where(count(aff >= pltpu.bitcast(cand, F32)) >= cap, cand, thr)
    thr = lax.fori_loop(0, 31, bit_body, jnp.zeros((1, N_EXPERTS, 1), jnp.int32))
    thr = pltpu.bitcast(thr, F32)

    tri = (lax.broadcasted_iota(jnp.int32, (LANES, LANES), 0)
           <= lax.broadcasted_iota(jnp.int32, (LANES, LANES), 1))
    tri = jnp.where(tri, 1.0, 0.0).astype(BF16)

    def chunk_cumsum(mask):
        flat = jnp.where(mask, 1.0, 0.0).astype(BF16).reshape(n_chunks * N_EXPERTS, LANES)
        inc = _dot(flat, tri).reshape(n_chunks, N_EXPERTS, LANES)
        run = jnp.zeros((N_EXPERTS, 1), F32)
        before = []
        for c in range(n_chunks):
            before.append(run)
            run = run + inc[c][:, LANES - 1:LANES]
        return inc, jnp.stack(before, axis=0)

    above = aff > thr
    tied = aff == thr
    need = cap - count(above)
    tie_inc, tie_before = chunk_cumsum(tied)
    tie_rank = tie_before + tie_inc - jnp.where(tied, 1.0, 0.0)
    sel = above | (tied & (tie_rank < need))
    sel_inc, sel_before = chunk_cumsum(sel)
    cum_s[...] = sel_inc
    before_s[...] = jnp.broadcast_to(sel_before, sel_inc.shape)

    slot = lax.broadcasted_iota(jnp.int32, (1, cap), 1).astype(F32)
    chunk_id = lax.broadcasted_iota(jnp.int32, (n_chunks, cap), 0).astype(F32)
    lane_id = lax.broadcasted_iota(jnp.int32, (LANES, cap), 0).astype(F32)
    for e in range(N_EXPERTS):
        cin = cum_s[:, e, :]
        cc_exc = before_s[:, e, :][:, 0:1]
        cc_inc = cc_exc + cin[:, LANES - 1:LANES]
        kc = jnp.sum(jnp.where(cc_inc <= slot, 1.0, 0.0), axis=0, keepdims=True)
        onehot = chunk_id == kc
        onehot_b = jnp.where(onehot, 1.0, 0.0).astype(BF16)
        counts_t = _dot_tn(cin.astype(BF16), onehot_b)
        local = slot - jnp.sum(jnp.where(onehot, cc_exc, 0.0), axis=0, keepdims=True)
        il = jnp.sum(jnp.where(counts_t <= local, 1.0, 0.0), axis=0, keepdims=True)
        idx_ref[0, e:e + 1, :] = ((kc * LANES + il) * FEATURE_CHUNKS).astype(jnp.int32)
        aff_t = jnp.zeros((LANES, cap), F32)
        for part in _split_bf16(aff_s[:, e, :], 3):
            aff_t = aff_t + _dot_tn(part, onehot_b)
        gate_ref[0, e:e + 1, :] = jnp.sum(jnp.where(lane_id == il, aff_t, 0.0),
                                          axis=0, keepdims=True)


def _topk(lg4, cap):
    bsz, n_chunks = lg4.shape[0], lg4.shape[1]
    return pl.pallas_call(
        functools.partial(_topk_kernel, cap=cap),
        out_shape=(
            jax.ShapeDtypeStruct((bsz, N_EXPERTS, cap), jnp.int32),
            jax.ShapeDtypeStruct((bsz, N_EXPERTS, cap), F32),
        ),
        grid=(bsz,),
        in_specs=[pl.BlockSpec((1, n_chunks, N_EXPERTS, LANES), lambda b: (b, 0, 0, 0))],
        out_specs=(
            pl.BlockSpec((1, N_EXPERTS, cap), lambda b: (b, 0, 0)),
            pl.BlockSpec((1, N_EXPERTS, cap), lambda b: (b, 0, 0)),
        ),
        scratch_shapes=[
            pltpu.VMEM((n_chunks, N_EXPERTS, LANES), F32),
            pltpu.VMEM((n_chunks, N_EXPERTS, LANES), F32),
            pltpu.VMEM((n_chunks, N_EXPERTS, LANES), F32),
        ],
        compiler_params=pltpu.CompilerParams(
            dimension_semantics=("arbitrary",), vmem_limit_bytes=VMEM_LIMIT_BYTES),
        name="topk",
    )(lg4)


def _moe_kernel(idx_ref, gate_ref, h2t_hbm, x1t_hbm, mod_ref, wg_ref, wu_ref, wd_ref, out_hbm,
                h2_s, acc_s, xs_s, ys_s, stage_s, sems, *, n, cap, sub):
    b = pl.program_id(0)
    e = pl.program_id(1)
    n_b = pl.num_programs(0)
    n_e = pl.num_programs(1)
    tok_rows = n * FEATURE_CHUNKS
    n_parts = cap // sub

    def load_h2(sample):
        return pltpu.make_async_copy(
            h2t_hbm.at[pl.ds(sample * tok_rows, tok_rows), :], h2_s, sems.at[0])

    def load_x1(sample):
        return pltpu.make_async_copy(
            x1t_hbm.at[pl.ds(sample * tok_rows, tok_rows), :], acc_s, sems.at[1])

    def load_x1_chunk(sample, c):
        rows = pl.ds(pl.multiple_of(c * (sub * FEATURE_CHUNKS), sub * FEATURE_CHUNKS),
                     sub * FEATURE_CHUNKS)
        return pltpu.make_async_copy(
            x1t_hbm.at[pl.ds(sample * tok_rows + c * (sub * FEATURE_CHUNKS),
                             sub * FEATURE_CHUNKS), :], acc_s.at[rows, :], sems.at[1])

    gates = gate_ref[0, 0]
    gl = gates.shape[1]
    gates = jnp.pad(gates, ((0, SUBLANES - gates.shape[0]), (0, LANES - gl)))
    gate_cols = gates.T
    gate2 = mod_ref[0, N_MOD - 1:N_MOD, :]

    def token_tile(first_row):
        return pl.ds(pl.multiple_of(first_row, FEATURE_CHUNKS), FEATURE_CHUNKS)

    def gather(part):
        for r in range(sub):
            t = idx_ref[0, 0, 0, part * sub + r]
            xs_s[part, r * FEATURE_CHUNKS:(r + 1) * FEATURE_CHUNKS, :] = h2_s[token_tile(t), :]

    def expert(part):
        x = jnp.concatenate(
            [xs_s[part, pl.ds(s, sub, stride=FEATURE_CHUNKS), :] for s in range(FEATURE_CHUNKS)],
            axis=1).astype(BF16)
        hid = (jax.nn.silu(_dot(x, wg_ref[0])) * _dot(x, wu_ref[0])).astype(BF16)
        y = _dot(hid, wd_ref[0]) * gate2
        first_col = part * sub // gl
        y = jnp.concatenate(
            [y[k * gl:(k + 1) * gl, :] * gate_cols[:gl, first_col + k:first_col + k + 1]
             for k in range(sub // gl)], axis=0)
        for s in range(FEATURE_CHUNKS):
            ys_s[part, pl.ds(s, sub, stride=FEATURE_CHUNKS), :] = y[:, s * LANES:(s + 1) * LANES]

    def scatter(part):
        for r0 in range(0, sub, SCATTER_UNROLL):
            updates = []
            for r in range(r0, r0 + SCATTER_UNROLL):
                dst = token_tile(idx_ref[0, 0, 0, part * sub + r])
                row = ys_s[part, r * FEATURE_CHUNKS:(r + 1) * FEATURE_CHUNKS, :]
                updates.append((dst, acc_s[dst, :] + row))
            for dst, val in updates:
                acc_s[dst, :] = val

    @pl.when((e == 0) & (b == 0))
    def _():
        load_h2(b).start()
        load_x1(b).start()

    @pl.when(e == 0)
    def _():
        load_h2(b).wait()

    for part in range(n_parts):
        gather(part)

    @pl.when((e == n_e - 1) & (b + 1 < n_b))
    def _():
        load_h2(b + 1).start()

    expert(0)

    @pl.when(e == 0)
    def _():
        load_x1(b).wait()

    for part in range(1, n_parts):
        expert(part)
    for part in range(n_parts):
        scatter(part)

    @pl.when(e == n_e - 1)
    def _():
        n_out = n // sub

        def store(c, slot):
            return pltpu.make_async_copy(
                stage_s.at[slot], out_hbm.at[pl.ds(b * n + c * sub, sub), :], sems.at[2 + slot])

        def out_body(c, carry):
            slot = c % 2

            @pl.when(c >= 2)
            def _():
                store(c - 2, slot).wait()
            row0 = pl.multiple_of(c * (sub * FEATURE_CHUNKS), sub * FEATURE_CHUNKS)
            for s in range(FEATURE_CHUNKS):
                stage_s[slot, :, s * LANES:(s + 1) * LANES] = (
                    acc_s[pl.ds(row0 + s, sub, stride=FEATURE_CHUNKS), :])
            store(c, slot).start()

            @pl.when(b + 1 < n_b)
            def _():
                load_x1_chunk(b + 1, c).start()
            return carry
        lax.fori_loop(0, n_out, out_body, 0)
        for c in range(n_out - 2, n_out):
            store(c, c % 2).wait()


def _moe(idx, gate, h2t, x1t, mod3, wg, wu, wd, bsz, n):
    cap = idx.shape[-1]
    sub = min(256, cap)
    gl = min(LANES, cap)
    assert n // sub >= 2 and cap % sub == 0 and sub % SCATTER_UNROLL == 0 and sub % gl == 0
    assert cap // gl <= SUBLANES
    tok_rows = n * FEATURE_CHUNKS
    weight_spec = pl.BlockSpec((1, D_MODEL, D_MODEL), lambda b, e: (e, 0, 0))
    smem_spec = pl.BlockSpec((1, 1, 1, cap), lambda b, e: (b, e, 0, 0),
                             memory_space=pltpu.SMEM)
    return pl.pallas_call(
        functools.partial(_moe_kernel, n=n, cap=cap, sub=sub),
        out_shape=jax.ShapeDtypeStruct((bsz * n, D_MODEL), F32),
        grid=(bsz, N_EXPERTS),
        in_specs=[
            smem_spec,
            pl.BlockSpec((1, 1, cap // gl, gl), lambda b, e: (b, e, 0, 0)),
            pl.BlockSpec(memory_space=pl.ANY),
            pl.BlockSpec(memory_space=pl.ANY),
            pl.BlockSpec((1, N_MOD, D_MODEL), lambda b, e: (b, 0, 0)),
            weight_spec,
            weight_spec,
            weight_spec,
        ],
        out_specs=pl.BlockSpec(memory_space=pl.ANY),
        scratch_shapes=[
            pltpu.VMEM((tok_rows, LANES), F32),
            pltpu.VMEM((tok_rows, LANES), F32),
            pltpu.VMEM((cap // sub, sub * FEATURE_CHUNKS, LANES), F32),
            pltpu.VMEM((cap // sub, sub * FEATURE_CHUNKS, LANES), F32),
            pltpu.VMEM((2, sub, D_MODEL), F32),
            pltpu.SemaphoreType.DMA((4,)),
        ],
        compiler_params=pltpu.CompilerParams(
            dimension_semantics=("arbitrary", "arbitrary"), vmem_limit_bytes=VMEM_LIMIT_BYTES),
        name="moe",
    )(idx[:, :, None, :], gate.reshape(bsz, N_EXPERTS, cap // gl, gl), h2t, x1t, mod3, wg, wu, wd)


def _rope_tables(n):
    t = jnp.arange(n)
    row = (t // GRID_W).astype(F32)
    col = (t % GRID_W).astype(F32)
    n_freq = HEAD_DIM // 4
    inv_freq = ROPE_BASE ** (-jnp.arange(n_freq, dtype=F32) / n_freq)
    ang_r = row[:, None] * inv_freq
    ang_c = col[:, None] * inv_freq
    cos = jnp.concatenate([jnp.cos(ang_r)] * 2 + [jnp.cos(ang_c)] * 2, axis=1)
    sin = jnp.concatenate([-jnp.sin(ang_r), jnp.sin(ang_r), -jnp.sin(ang_c), jnp.sin(ang_c)],
                          axis=1)
    return jnp.tile(cos, (1, 2)), jnp.tile(sin, (1, 2))


def _pack_gate_weights(w_r, b_r, w_i, b_i):
    n_p = LRU_WIDTH // LANES
    zeros = jnp.zeros((LRU_BLOCK_DIM, LRU_BLOCK_DIM), F32)

    def pair(w, p):
        top = jnp.concatenate([w[2 * p], zeros], axis=1)
        bot = jnp.concatenate([zeros, w[2 * p + 1]], axis=1)
        return jnp.concatenate([top, bot], axis=0)

    ws, bs = [], []
    for p in range(n_p):
        ws.append(jnp.concatenate(
            [pair(w_r[0], p), pair(w_i[0], p), pair(w_r[1], p), pair(w_i[1], p)], axis=1))
        sl = slice(p * LANES, (p + 1) * LANES)
        bs.append(jnp.concatenate([b_r[0, sl], b_i[0, sl], b_r[1, sl], b_i[1, sl]])[None, :])
    return jnp.stack(ws).astype(BF16), 0.5 * jnp.stack(bs)


def _layer(x, ctx, mod3, norm1_g, norm2_g, w_in, q_norm_g, k_norm_g, attn_sink, conv_w, conv_b,
           lru_w_r, lru_b_r, lru_w_i, lru_b_i, lru_lambda, w_out, w_router, w_gate, w_up, w_down):
    bsz, n, _ = x.shape
    n_ctx = ctx.shape[1]
    x2 = x.reshape(bsz * n, D_MODEL)
    c2 = ctx.reshape(bsz * n_ctx, D_MODEL)
    g1 = norm1_g[None, :]
    g2 = norm2_g[None, :]
    qg = jnp.tile(q_norm_g, 2)[None, :]
    kg = jnp.tile(k_norm_g, 2)[None, :]
    cos, sin = _rope_tables(n)
    w_in_b = w_in.astype(BF16)
    w_ctx_b = w_in_b[:, ATTN_WIDTH:ATTN_WIDTH + 2 * KV_WIDTH + LRU_WIDTH]

    q_s, k2, v2, xr2, gg2 = _inproj_latent(x2, mod3, g1, w_in_b, qg, kg, cos, sin, bsz, n)
    kc2, vc2, xrc2 = _inproj_ctx(c2, mod3, g1, w_ctx_b, kg, bsz, n_ctx)

    o_s = _attn(attn_sink, q_s, k2.reshape(bsz, n, KV_WIDTH), v2.reshape(bsz, n, KV_WIDTH),
                kc2.reshape(bsz, n_ctx, KV_WIDTH), vc2.reshape(bsz, n_ctx, KV_WIDTH))

    wg_lru, bg_lru = _pack_gate_weights(lru_w_r, lru_b_r, lru_w_i, lru_b_i)
    rg3 = _lru(xr2.reshape(bsz, n, LRU_WIDTH), xrc2.reshape(bsz, n_ctx, LRU_WIDTH),
               gg2.reshape(bsz, n, LRU_WIDTH), 0.5 * conv_w, 0.5 * conv_b[None, :], wg_lru,
               bg_lru, lru_lambda)

    w_att = (w_out[:ATTN_WIDTH].reshape(N_KV_HEADS, Q_PER_KV, HEAD_DIM, D_MODEL)
             .transpose(1, 0, 2, 3).reshape(ATTN_WIDTH, D_MODEL).astype(BF16))
    w_rnn = w_out[ATTN_WIDTH:].astype(BF16)
    x1t, h2t, lg4 = _outproj(o_s, rg3.reshape(bsz * n, LRU_WIDTH), x2, mod3, w_att, w_rnn, g2,
                             w_router.T.astype(BF16), bsz, n)

    cap = CAPACITY_FACTOR * n // N_EXPERTS
    idx, gate = _topk(lg4, cap)
    out = _moe(idx, gate, h2t, x1t, mod3, w_gate.astype(BF16), w_up.astype(BF16),
               w_down.astype(BF16), bsz, n)
    return out.reshape(bsz, n, D_MODEL)


def kernel(x, c, ctx, c_ctx, w_ada, b_ada, norm1_g, norm2_g, w_in, q_norm_g, k_norm_g, attn_sink, conv_w, conv_b, lru_w_r, lru_b_r, lru_w_i, lru_b_i, lru_lambda, w_out, w_router, w_gate, w_up, w_down):
    bsz = x.shape[0]
    depth = w_ada.shape[0]
    assert depth == 1, "context-stream update between layers is not implemented"
    pad_rows = -(bsz + 1) % SUBLANES
    cc = jnp.concatenate([c, jnp.zeros((pad_rows, D_MODEL), F32), c_ctx[None, :]], axis=0)
    for layer in range(depth):
        mod = _ada(cc, w_ada[layer], b_ada[layer][None, :])
        mod3 = mod.reshape(cc.shape[0], N_MOD, D_MODEL)
        x = _layer(x, ctx, mod3, norm1_g[layer], norm2_g[layer], w_in[layer], q_norm_g[layer],
                   k_norm_g[layer], attn_sink[layer], conv_w[layer], conv_b[layer],
                   lru_w_r[layer], lru_b_r[layer], lru_w_i[layer], lru_b_i[layer],
                   lru_lambda[layer], w_out[layer], w_router[layer], w_gate[layer],
                   w_up[layer], w_down[layer])
    return x
```

```python
import functools

import jax
import jax.numpy as jnp
from jax import lax
from jax.experimental import pallas as pl
from jax.experimental.pallas import tpu as pltpu

F32 = jnp.float32
BF16 = jnp.bfloat16

LANES = 128
SUBLANES = 8
VMEM_LIMIT_BYTES = 56 * 1024 * 1024

D_MODEL = 1024
HEAD_DIM = 64
N_Q_HEADS = 8
N_KV_HEADS = 2
Q_PER_KV = N_Q_HEADS // N_KV_HEADS
ATTN_WIDTH = N_Q_HEADS * HEAD_DIM
KV_WIDTH = N_KV_HEADS * HEAD_DIM
LRU_WIDTH = D_MODEL - ATTN_WIDTH
LRU_BLOCK_DIM = 64
IN_WIDTH = ATTN_WIDTH + 2 * KV_WIDTH + 2 * LRU_WIDTH
WINDOW = 128
BLOCK = 128
GRID_W = 64
ROPE_BASE = 10000.0
LRU_C = 8.0
N_EXPERTS = 16
CAPACITY_FACTOR = 2
EPS = 1e-6
NEG_INF = -1e30
LOG2_E = 1.4426950408889634
SCORE_SCALE = HEAD_DIM ** -0.5 * LOG2_E
N_MOD = 6
FEATURE_CHUNKS = D_MODEL // LANES
CAST_ROWS = 128
SCATTER_UNROLL = 16


def _dot(a, b):
    return jnp.dot(a, b, preferred_element_type=F32)


def _dot_nt(a, b):
    return lax.dot_general(a, b, (((1,), (1,)), ((), ())), preferred_element_type=F32)


def _dot_tn(a, b):
    return lax.dot_general(a, b, (((0,), (0,)), ((), ())), preferred_element_type=F32)


def _split_bf16(x, parts):
    out = []
    r = x
    for _ in range(parts):
        p = r.astype(BF16)
        out.append(p)
        r = r - p.astype(F32)
    return out


def _rmsnorm_rows(x, g):
    ms = jnp.mean(x * x, axis=-1, keepdims=True)
    return x * lax.rsqrt(ms + EPS) * g


def _ada_kernel(c_ref, w_ref, b_ref, o_ref):
    s = jax.nn.silu(c_ref[...])
    s_hi, s_lo = _split_bf16(s, 2)
    w_hi, w_lo = _split_bf16(w_ref[...], 2)
    o_ref[...] = _dot(s_hi, w_hi) + _dot(s_hi, w_lo) + _dot(s_lo, w_hi) + b_ref[...]


def _ada(cc, w_ada, b_ada):
    rows = cc.shape[0]
    width = w_ada.shape[1]
    return pl.pallas_call(
        _ada_kernel,
        out_shape=jax.ShapeDtypeStruct((rows, width), F32),
        grid=(width // D_MODEL,),
        in_specs=[
            pl.BlockSpec((rows, D_MODEL), lambda j: (0, 0)),
            pl.BlockSpec((D_MODEL, D_MODEL), lambda j: (0, j)),
            pl.BlockSpec((1, D_MODEL), lambda j: (0, j)),
        ],
        out_specs=pl.BlockSpec((rows, D_MODEL), lambda j: (0, j)),
        compiler_params=pltpu.CompilerParams(
            dimension_semantics=("arbitrary",), vmem_limit_bytes=VMEM_LIMIT_BYTES),
        name="ada",
    )(cc, w_ada, b_ada)


def _head_norm(p, g):
    lane = lax.broadcasted_iota(jnp.int32, p.shape, 1)
    low = lane < HEAD_DIM
    sq = p * p
    s_low = jnp.sum(jnp.where(low, sq, 0.0), axis=-1, keepdims=True)
    s_high = jnp.sum(jnp.where(low, 0.0, sq), axis=-1, keepdims=True)
    ms = jnp.where(low, s_low, s_high) * (1.0 / HEAD_DIM)
    return p * lax.rsqrt(ms + EPS) * g


def _rope(p, cos, sin_signed):
    lane = lax.broadcasted_iota(jnp.int32, p.shape, 1)
    partner = jnp.where((lane & 16) == 0,
                        pltpu.roll(p, LANES - 16, 1), pltpu.roll(p, 16, 1))
    return p * cos + partner * sin_signed


def _inproj_latent_kernel(x_ref, mod_ref, g1_ref, w_ref, qg_ref, kg_ref, cos_ref, sin_ref,
                          q_ref, k_ref, v_ref, xr_ref, gg_ref):
    x = x_ref[...]
    h = _rmsnorm_rows(x, g1_ref[...]) * (1.0 + mod_ref[0, 1:2, :]) + mod_ref[0, 0:1, :]
    y = _dot(h.astype(BF16), w_ref[...])
    cos = cos_ref[...]
    sin = sin_ref[...]
    pieces = []
    for j in range(ATTN_WIDTH // LANES):
        p = _head_norm(y[:, j * LANES:(j + 1) * LANES], qg_ref[...])
        pieces.append(_rope(p, cos, sin))
    lane = lax.broadcasted_iota(jnp.int32, pieces[0].shape, 1)
    low = lane < HEAD_DIM
    n_blocks = y.shape[0] // BLOCK
    for g in range(Q_PER_KV):
        src0 = pieces[g // 2]
        src1 = pieces[2 + g // 2]
        if g % 2 == 0:
            out = jnp.where(low, src0, pltpu.roll(src1, HEAD_DIM, 1))
        else:
            out = jnp.where(low, pltpu.roll(src0, HEAD_DIM, 1), src1)
        out = (out * SCORE_SCALE).astype(BF16)
        for j in range(n_blocks):
            q_ref[0, j, g * BLOCK:(g + 1) * BLOCK, :] = out[j * BLOCK:(j + 1) * BLOCK, :]
    k = _head_norm(y[:, ATTN_WIDTH:ATTN_WIDTH + KV_WIDTH], kg_ref[...])
    k_ref[...] = _rope(k, cos, sin).astype(BF16)
    o = ATTN_WIDTH + KV_WIDTH
    v_ref[...] = y[:, o:o + KV_WIDTH].astype(BF16)
    o += KV_WIDTH
    xr_ref[...] = y[:, o:o + LRU_WIDTH]
    o += LRU_WIDTH
    gg_ref[...] = jax.nn.gelu(y[:, o:o + LRU_WIDTH])


def _inproj_ctx_kernel(x_ref, mod_ref, g1_ref, w_ref, kg_ref, k_ref, v_ref, xr_ref):
    x = x_ref[...]
    h = _rmsnorm_rows(x, g1_ref[...]) * (1.0 + mod_ref[0, 1:2, :]) + mod_ref[0, 0:1, :]
    y = _dot(h.astype(BF16), w_ref[...])
    k_ref[...] = _head_norm(y[:, 0:KV_WIDTH], kg_ref[...]).astype(BF16)
    v_ref[...] = y[:, KV_WIDTH:2 * KV_WIDTH].astype(BF16)
    xr_ref[...] = y[:, 2 * KV_WIDTH:2 * KV_WIDTH + LRU_WIDTH]


OUTPROJ_ROW_TILE = 1024


def _row_tile(n):
    return min(512, n)


def _inproj_latent(x2, mod3, g1, w_in, qg, kg, cos, sin, bsz, n):
    tm = _row_tile(n)
    tps = n // tm
    rows = bsz * n
    const = lambda i: (0, 0)
    return pl.pallas_call(
        _inproj_latent_kernel,
        out_shape=(
            jax.ShapeDtypeStruct((bsz, n // BLOCK, Q_PER_KV * BLOCK, LANES), BF16),
            jax.ShapeDtypeStruct((rows, KV_WIDTH), BF16),
            jax.ShapeDtypeStruct((rows, KV_WIDTH), BF16),
            jax.ShapeDtypeStruct((rows, LRU_WIDTH), F32),
            jax.ShapeDtypeStruct((rows, LRU_WIDTH), F32),
        ),
        grid=(rows // tm,),
        in_specs=[
            pl.BlockSpec((tm, D_MODEL), lambda i: (i, 0)),
            pl.BlockSpec((1, N_MOD, D_MODEL), lambda i: (i // tps, 0, 0)),
            pl.BlockSpec((1, D_MODEL), const),
            pl.BlockSpec((D_MODEL, IN_WIDTH), const),
            pl.BlockSpec((1, LANES), const),
            pl.BlockSpec((1, LANES), const),
            pl.BlockSpec((tm, LANES), lambda i: (i % tps, 0)),
            pl.BlockSpec((tm, LANES), lambda i: (i % tps, 0)),
        ],
        out_specs=(
            pl.BlockSpec((1, tm // BLOCK, Q_PER_KV * BLOCK, LANES),
                         lambda i: (i // tps, i % tps, 0, 0)),
            pl.BlockSpec((tm, KV_WIDTH), lambda i: (i, 0)),
            pl.BlockSpec((tm, KV_WIDTH), lambda i: (i, 0)),
            pl.BlockSpec((tm, LRU_WIDTH), lambda i: (i, 0)),
            pl.BlockSpec((tm, LRU_WIDTH), lambda i: (i, 0)),
        ),
        compiler_params=pltpu.CompilerParams(
            dimension_semantics=("arbitrary",), vmem_limit_bytes=VMEM_LIMIT_BYTES),
        name="inproj_latent",
    )(x2, mod3, g1, w_in, qg, kg, cos, sin)


def _inproj_ctx(c2, mod3, g1, w_ctx, kg, bsz, n_ctx):
    tm = _row_tile(n_ctx)
    rows = bsz * n_ctx
    ctx_row = mod3.shape[0] - 1
    width = w_ctx.shape[1]
    const = lambda i: (0, 0)
    return pl.pallas_call(
        _inproj_ctx_kernel,
        out_shape=(
            jax.ShapeDtypeStruct((rows, KV_WIDTH), BF16),
            jax.ShapeDtypeStruct((rows, KV_WIDTH), BF16),
            jax.ShapeDtypeStruct((rows, LRU_WIDTH), F32),
        ),
        grid=(rows // tm,),
        in_specs=[
            pl.BlockSpec((tm, D_MODEL), lambda i: (i, 0)),
            pl.BlockSpec((1, N_MOD, D_MODEL), lambda i: (ctx_row, 0, 0)),
            pl.BlockSpec((1, D_MODEL), const),
            pl.BlockSpec((D_MODEL, width), const),
            pl.BlockSpec((1, LANES), const),
        ],
        out_specs=(
            pl.BlockSpec((tm, KV_WIDTH), lambda i: (i, 0)),
            pl.BlockSpec((tm, KV_WIDTH), lambda i: (i, 0)),
            pl.BlockSpec((tm, LRU_WIDTH), lambda i: (i, 0)),
        ),
        compiler_params=pltpu.CompilerParams(
            dimension_semantics=("arbitrary",), vmem_limit_bytes=VMEM_LIMIT_BYTES),
        name="inproj_ctx",
    )(c2, mod3, g1, w_ctx, kg)


def _attn_kernel(sink_ref, q_ref, k_ref, v_ref, kc_ref, vc_ref, bias_ref, o_ref, s0_s, s1_s,
                 *, n, nb):
    t = pl.program_id(0)
    last = pl.num_programs(0) - 2
    j_scored = jnp.minimum(t, last) % nb
    j_finished = jnp.maximum(t - 1, 0) % nb
    span = 3 * BLOCK

    def window_start(i):
        return pl.multiple_of(jnp.clip((i - 1) * BLOCK, 0, n - span), BLOCK)

    def head_only(t, h):
        lane = lax.broadcasted_iota(jnp.int32, t.shape, 1)
        keep = (lane < HEAD_DIM) if h == 0 else (lane >= HEAD_DIM)
        return jnp.where(keep, t, jnp.zeros_like(t))

    stages = functools.partial(_attn_stages, sink_ref, q_ref, k_ref, v_ref, kc_ref, vc_ref,
                               bias_ref, o_ref, start_scored=window_start(j_scored),
                               start_finished=window_start(j_finished), head_only=head_only)

    @pl.when(t == 0)
    def _():
        s1_s[...] = jnp.zeros_like(s1_s)

    @pl.when(t % 2 == 0)
    def _():
        stages(s0_s, s1_s)

    @pl.when(t % 2 == 1)
    def _():
        stages(s1_s, s0_s)


def _attn_stages(sink_ref, q_ref, k_ref, v_ref, kc_ref, vc_ref, bias_ref, o_ref, s_new, s_old,
                 start_scored, start_finished, head_only):
    span = 3 * BLOCK

    def with_ones(t, lane_index):
        lane = lax.broadcasted_iota(jnp.int32, t.shape, 1)
        return jnp.where(lane == lane_index, jnp.ones_like(t), t)

    q = q_ref[0, 0]
    kw = k_ref[0, pl.ds(start_scored, span), :]
    kc = kc_ref[0]
    bias = bias_ref[0]
    for h in range(N_KV_HEADS):
        s_new[h, :, 0:span] = _dot_nt(q, head_only(kw, h)) + bias
        s_new[h, :, span:] = _dot_nt(q, head_only(kc, h))

    vw = v_ref[0, pl.ds(start_finished, span), :]
    vc = vc_ref[0]
    rows = q.shape[0]
    group = lax.broadcasted_iota(jnp.int32, (rows, 1), 0) // BLOCK
    acc = jnp.zeros((rows, LANES), F32)
    for h in range(N_KV_HEADS):
        s_loc = s_old[h, :, 0:span]
        s_ctx = s_old[h, :, span:]
        sink = jnp.zeros((rows, 1), F32)
        for g in range(Q_PER_KV):
            sink = jnp.where(group == g, sink_ref[h * Q_PER_KV + g] * LOG2_E, sink)
        m = jnp.maximum(jnp.maximum(jnp.max(s_loc, axis=-1, keepdims=True),
                                    jnp.max(s_ctx, axis=-1, keepdims=True)), sink)
        p_loc = jnp.exp2(s_loc - m).astype(BF16)
        p_ctx = jnp.exp2(s_ctx - m).astype(BF16)
        sum_lane = HEAD_DIM if h == 0 else 0
        o = (_dot(p_loc, with_ones(head_only(vw, h), sum_lane))
             + _dot(p_ctx, with_ones(head_only(vc, h), sum_lane)))
        denom = o[:, sum_lane:sum_lane + 1] + jnp.exp2(sink - m)
        lane = lax.broadcasted_iota(jnp.int32, o.shape, 1)
        own = (lane < HEAD_DIM) if h == 0 else (lane >= HEAD_DIM)
        acc = acc + jnp.where(own, o, 0.0) / denom
    o_ref[0, 0] = acc.astype(BF16)


def _band_bias(n):
    span = 3 * BLOCK
    r = jnp.arange(Q_PER_KV * BLOCK)[:, None] % BLOCK
    c = jnp.arange(span)[None, :]
    offsets = jnp.arange(span // BLOCK)[:, None, None] * BLOCK
    valid = jnp.abs(r - c + offsets) <= WINDOW
    return jnp.where(valid, 0.0, NEG_INF).astype(F32)


def _attn(sink, q_s, k3, v3, kc3, vc3):
    bsz, nb = q_s.shape[0], q_s.shape[1]
    n = k3.shape[1]
    n_ctx = kc3.shape[1]
    assert nb >= 3
    bias = _band_bias(n)

    n_blocks = bsz * nb

    def scored(t):
        return jnp.divmod(jnp.minimum(t, n_blocks - 1), nb)

    def finished(t):
        return jnp.divmod(jnp.maximum(t - 1, 0), nb)

    def placement(t):
        i = scored(t)[1]
        return (jnp.minimum(i, 1) + (i == nb - 1).astype(jnp.int32), 0, 0)

    block = (1, 1, Q_PER_KV * BLOCK, LANES)

    return pl.pallas_call(
        functools.partial(_attn_kernel, n=n, nb=nb),
        out_shape=jax.ShapeDtypeStruct(q_s.shape, BF16),
        grid=(n_blocks + 1,),
        in_specs=[
            pl.BlockSpec(memory_space=pltpu.SMEM),
            pl.BlockSpec(block, lambda t: (*scored(t), 0, 0)),
            pl.BlockSpec((1, n, KV_WIDTH), lambda t: (scored(t)[0], 0, 0)),
            pl.BlockSpec((1, n, KV_WIDTH), lambda t: (finished(t)[0], 0, 0)),
            pl.BlockSpec((1, n_ctx, KV_WIDTH), lambda t: (scored(t)[0], 0, 0)),
            pl.BlockSpec((1, n_ctx, KV_WIDTH), lambda t: (finished(t)[0], 0, 0)),
            pl.BlockSpec((1, Q_PER_KV * BLOCK, 3 * BLOCK), placement),
        ],
        out_specs=pl.BlockSpec(block, lambda t: (*finished(t), 0, 0)),
        scratch_shapes=[
            pltpu.VMEM((N_KV_HEADS, Q_PER_KV * BLOCK, 3 * BLOCK + n_ctx), F32),
            pltpu.VMEM((N_KV_HEADS, Q_PER_KV * BLOCK, 3 * BLOCK + n_ctx), F32),
        ],
        compiler_params=pltpu.CompilerParams(
            dimension_semantics=("arbitrary",), vmem_limit_bytes=VMEM_LIMIT_BYTES),
        name="attn",
    )(sink, q_s, k3, v3, kc3, vc3, bias)


def _softplus(z):
    return jnp.maximum(z, 0.0) + jnp.log1p(jnp.exp(-jnp.abs(z)))


LRU_MAIN_SEG = 36


def _lru_plan(length):
    main, rest = divmod(length // SUBLANES, LRU_MAIN_SEG)
    if rest == 0:
        tails = ()
    elif rest % SUBLANES == 0 and rest > SUBLANES:
        tails = (rest - 5, 5)
    else:
        tails = (rest,)
    assert length % SUBLANES == 0 and all(t >= 2 for t in tails)
    return main, tails


def _lru_kernel(xr_ref, xrc_ref, gg_ref, cw_ref, cb_ref, wg_ref, bg_ref, lam_ref, o_ref,
                xc_s, xcc_s, hf_s, hb_s, *, n, n_ctx):
    cw = cw_ref[...]
    cb = cb_ref[...]
    sub = lax.broadcasted_iota(jnp.int32, (SUBLANES, LANES), 0)
    main_rows = SUBLANES * LRU_MAIN_SEG
    n_main, tails = _lru_plan(n)
    c_main, c_tails = _lru_plan(n_ctx)

    def static_tiles(n_main_tiles, tail_segs, with_main):
        tiles = [(k * main_rows, LRU_MAIN_SEG) for k in range(n_main_tiles)] if with_main else []
        t0 = n_main_tiles * main_rows
        for seg in tail_segs:
            tiles.append((t0, seg))
            t0 += SUBLANES * seg
        return tiles

    def aligned(v):
        return v if isinstance(v, int) else pl.multiple_of(v, SUBLANES)

    def conv_tile(src_ref, dst_ref, t0, seg, length):
        x = [src_ref[0, pl.ds(t0 + j, SUBLANES, stride=seg), :] for j in range(seg)]
        lo = aligned(jnp.maximum(t0 - SUBLANES, 0))
        hi = aligned(jnp.minimum(t0 + SUBLANES * seg, length - SUBLANES))
        prev = jnp.where(t0 > 0, src_ref[0, pl.ds(lo, SUBLANES), :], 0.0)
        nxt = jnp.where(t0 + SUBLANES * seg < length, src_ref[0, pl.ds(hi, SUBLANES), :], 0.0)

        def from_prev_segment(v, row):
            return jnp.where(sub == 0, row, pltpu.roll(v, 1, 0))

        def from_next_segment(v, row):
            return jnp.where(sub == SUBLANES - 1, row, pltpu.roll(v, SUBLANES - 1, 0))

        m1 = [from_prev_segment(x[seg - 1], prev[7:8])] + x[:seg - 1]
        m2 = [from_prev_segment(x[seg - 2], prev[6:7])] + m1[:seg - 1]
        p1 = x[1:] + [from_next_segment(x[0], nxt[0:1])]
        for j in range(seg):
            y = cb + cw[0:1] * m2[j]
            y = y + cw[1:2] * m1[j]
            y = y + cw[2:3] * x[j]
            y = y + cw[3:4] * p1[j]
            dst_ref[pl.ds(t0 + j * SUBLANES, SUBLANES), :] = y

    half_rate = [-0.5 * LRU_C * _softplus(-lam_ref[d:d + 1, :]) for d in range(2)]

    def coeffs(xc, d):
        w = wg_ref[0, :, 2 * LANES * d:2 * LANES * (d + 1)]
        g = _dot(xc.astype(BF16), w) + bg_ref[0, :, 2 * LANES * d:2 * LANES * (d + 1)]
        t_r = jnp.tanh(g[:, :LANES])
        t_i = jnp.tanh(g[:, LANES:])
        log_a = t_r * half_rate[d] + half_rate[d]
        a = jnp.exp(log_a)
        mult = jnp.sqrt(-jnp.tanh(log_a) * (a * a + 1.0))
        return a, mult * (t_i + 1.0) * xc

    def scan_tile(a, b, h_in, seg, reverse):
        order = list(range(seg - 1, -1, -1) if reverse else range(seg))
        local = [None] * seg
        prod = [None] * seg
        h = p = None
        for j in order:
            aj = a[j * SUBLANES:(j + 1) * SUBLANES, :]
            bj = b[j * SUBLANES:(j + 1) * SUBLANES, :]
            h = bj if h is None else aj * h + bj
            p = aj if p is None else aj * p
            local[j], prod[j] = h, p
        end = order[-1]
        carry = jnp.zeros((SUBLANES, LANES), F32)
        c = h_in
        for s in (range(SUBLANES - 1, -1, -1) if reverse else range(SUBLANES)):
            carry = jnp.where(sub == s, c, carry)
            c = prod[end][s:s + 1, :] * c + local[end][s:s + 1, :]
        return [local[j] + prod[j] * carry for j in range(seg)], c

    def run_tile(xc_ref, t0, seg, d, h, emit=None):
        a, b = coeffs(xc_ref[pl.ds(t0, SUBLANES * seg), :], d)
        states, h = scan_tile(a, b, h, seg, d == 1)
        if emit is not None:
            emit(t0, seg, states)
        return h

    def keep(dst_ref):
        def emit(t0, seg, states):
            for j in range(seg):
                dst_ref[pl.ds(t0 + j * SUBLANES, SUBLANES), :] = states[j]
        return emit

    def write_output(other_ref):
        def emit(t0, seg, states):
            for j in range(seg):
                rows = pl.ds(t0 + j, SUBLANES, stride=seg)
                other = other_ref[pl.ds(t0 + j * SUBLANES, SUBLANES), :]
                o_ref[0, rows, :] = (other + states[j]) * gg_ref[0, rows, :]
        return emit

    def main_t0(k):
        return pl.multiple_of(k * main_rows, SUBLANES)

    for t0, seg in static_tiles(c_main, c_tails, True):
        conv_tile(xrc_ref, xcc_s, t0, seg, n_ctx)

    def conv_body(k, carry):
        conv_tile(xr_ref, xc_s, main_t0(k), LRU_MAIN_SEG, n)
        return carry
    lax.fori_loop(0, n_main, conv_body, 0)
    for t0, seg in static_tiles(n_main, tails, False):
        conv_tile(xr_ref, xc_s, t0, seg, n)

    hf = jnp.zeros((1, LANES), F32)
    hb = jnp.zeros((1, LANES), F32)
    for t0, seg in static_tiles(c_main, c_tails, True):
        hf = run_tile(xcc_s, t0, seg, 0, hf)
    for t0, seg in reversed(static_tiles(c_main, c_tails, True)):
        hb = run_tile(xcc_s, t0, seg, 1, hb)
    for t0, seg in reversed(static_tiles(n_main, tails, False)):
        hb = run_tile(xc_s, t0, seg, 1, hb, keep(hb_s))

    def pair_body(first_visit):
        def body(k, carry):
            hf, hb = carry
            fwd_emit = keep(hf_s) if first_visit else write_output(hb_s)
            rev_emit = keep(hb_s) if first_visit else write_output(hf_s)
            hf = run_tile(xc_s, main_t0(k), LRU_MAIN_SEG, 0, hf, fwd_emit)
            hb = run_tile(xc_s, main_t0(n_main - 1 - k), LRU_MAIN_SEG, 1, hb, rev_emit)
            return hf, hb
        return body

    half = n_main // 2
    hf, hb = lax.fori_loop(0, half, pair_body(True), (hf, hb))
    if n_main % 2:
        hf = run_tile(xc_s, half * main_rows, LRU_MAIN_SEG, 0, hf, keep(hf_s))
        hb = run_tile(xc_s, half * main_rows, LRU_MAIN_SEG, 1, hb, write_output(hf_s))
    hf, hb = lax.fori_loop(n_main - half, n_main, pair_body(False), (hf, hb))
    for t0, seg in static_tiles(n_main, tails, False):
        hf = run_tile(xc_s, t0, seg, 0, hf, write_output(hb_s))


def _lru(xr3, xrc3, gg3, conv_w, conv_b, wg, bg, lam):
    bsz, n, _ = xr3.shape
    n_ctx = xrc3.shape[1]
    n_p = LRU_WIDTH // LANES
    return pl.pallas_call(
        functools.partial(_lru_kernel, n=n, n_ctx=n_ctx),
        out_shape=jax.ShapeDtypeStruct((bsz, n, LRU_WIDTH), F32),
        grid=(bsz, n_p),
        in_specs=[
            pl.BlockSpec((1, n, LANES), lambda b, p: (b, 0, p)),
            pl.BlockSpec((1, n_ctx, LANES), lambda b, p: (b, 0, p)),
            pl.BlockSpec((1, n, LANES), lambda b, p: (b, 0, p)),
            pl.BlockSpec((conv_w.shape[0], LANES), lambda b, p: (0, p)),
            pl.BlockSpec((1, LANES), lambda b, p: (0, p)),
            pl.BlockSpec((1, LANES, 4 * LANES), lambda b, p: (p, 0, 0)),
            pl.BlockSpec((1, 1, 4 * LANES), lambda b, p: (p, 0, 0)),
            pl.BlockSpec((2, LANES), lambda b, p: (0, p)),
        ],
        out_specs=pl.BlockSpec((1, n, LANES), lambda b, p: (b, 0, p)),
        scratch_shapes=[
            pltpu.VMEM((n, LANES), F32),
            pltpu.VMEM((n_ctx, LANES), F32),
            pltpu.VMEM((n, LANES), F32),
            pltpu.VMEM((n, LANES), F32),
        ],
        compiler_params=pltpu.CompilerParams(
            dimension_semantics=("arbitrary", "arbitrary"), vmem_limit_bytes=VMEM_LIMIT_BYTES),
        name="lru",
    )(xr3, xrc3, gg3, conv_w, conv_b, wg, bg, lam)


def _outproj_kernel(o_ref, rg_ref, x_ref, mod_ref, watt_ref, wrnn_ref, g2_ref, wr_ref,
                    x1t_ref, h2t_ref, lg_ref):
    tm = x_ref.shape[0]
    n_blocks = tm // BLOCK
    att = jnp.concatenate(
        [jnp.concatenate([o_ref[0, j, g * BLOCK:(g + 1) * BLOCK, :] for g in range(Q_PER_KV)],
                         axis=1) for j in range(n_blocks)], axis=0)
    y = _dot(att, watt_ref[...]) + _dot(rg_ref[...].astype(BF16), wrnn_ref[...])
    x1 = x_ref[...] + mod_ref[0, 2:3, :] * y
    h2 = _rmsnorm_rows(x1, g2_ref[...]) * (1.0 + mod_ref[0, 4:5, :]) + mod_ref[0, 3:4, :]
    for s in range(FEATURE_CHUNKS):
        rows = pl.ds(s, tm, stride=FEATURE_CHUNKS)
        x1t_ref[rows, :] = x1[:, s * LANES:(s + 1) * LANES]
        h2t_ref[rows, :] = h2[:, s * LANES:(s + 1) * LANES]
    lt = _dot_nt(wr_ref[...], h2.astype(BF16))
    for j in range(n_blocks):
        lg_ref[0, j] = lt[:, j * LANES:(j + 1) * LANES]


def _outproj(o_s, rg2, x2, mod3, w_att, w_rnn, g2, wr_t, bsz, n):
    tm = min(OUTPROJ_ROW_TILE, n)
    tps = n // tm
    rows = bsz * n
    const = lambda i: (0, 0)
    return pl.pallas_call(
        _outproj_kernel,
        out_shape=(
            jax.ShapeDtypeStruct((rows * FEATURE_CHUNKS, LANES), F32),
            jax.ShapeDtypeStruct((rows * FEATURE_CHUNKS, LANES), F32),
            jax.ShapeDtypeStruct((bsz, n // LANES, N_EXPERTS, LANES), F32),
        ),
        grid=(rows // tm,),
        in_specs=[
            pl.BlockSpec((1, tm // BLOCK, Q_PER_KV * BLOCK, LANES),
                         lambda i: (i // tps, i % tps, 0, 0)),
            pl.BlockSpec((tm, LRU_WIDTH), lambda i: (i, 0)),
            pl.BlockSpec((tm, D_MODEL), lambda i: (i, 0)),
            pl.BlockSpec((1, N_MOD, D_MODEL), lambda i: (i // tps, 0, 0)),
            pl.BlockSpec((ATTN_WIDTH, D_MODEL), const),
            pl.BlockSpec((LRU_WIDTH, D_MODEL), const),
            pl.BlockSpec((1, D_MODEL), const),
            pl.BlockSpec((N_EXPERTS, D_MODEL), const),
        ],
        out_specs=(
            pl.BlockSpec((tm * FEATURE_CHUNKS, LANES), lambda i: (i, 0)),
            pl.BlockSpec((tm * FEATURE_CHUNKS, LANES), lambda i: (i, 0)),
            pl.BlockSpec((1, tm // LANES, N_EXPERTS, LANES), lambda i: (i // tps, i % tps, 0, 0)),
        ),
        compiler_params=pltpu.CompilerParams(
            dimension_semantics=("arbitrary",), vmem_limit_bytes=VMEM_LIMIT_BYTES),
        name="outproj",
    )(o_s, rg2, x2, mod3, w_att, w_rnn, g2, wr_t)


def _topk_kernel(lg_ref, *refs, cap, n_cast):
    w_refs = refs[:n_cast]
    idx_ref, gate_ref = refs[n_cast:n_cast + 2]
    wb_refs = refs[n_cast + 2:2 * n_cast + 2]
    aff_s, cum_s, before_s = refs[2 * n_cast + 2:]
    for w_ref, wb_ref in zip(w_refs, wb_refs):
        def cast_rows(i, carry):
            rows = pl.ds(pl.multiple_of(i * CAST_ROWS, CAST_ROWS), CAST_ROWS)
            for k in range(w_ref.shape[0]):
                wb_ref[k, rows, :] = w_ref[k, rows, :].astype(BF16)
            return carry
        lax.fori_loop(0, w_ref.shape[1] // CAST_ROWS, cast_rows, 0)

    lg = lg_ref[0]
    n_chunks = lg.shape[0]
    m = jnp.max(lg, axis=1, keepdims=True)
    ex = jnp.exp(lg - m)
    aff = ex / jnp.sum(ex, axis=1, keepdims=True)
    aff_s[...] = aff

    def count(mask):
        c = jnp.sum(jnp.where(mask, 1.0, 0.0), axis=0, keepdims=True)
        return jnp.sum(c, axis=2, keepdims=True)

    def bit_body(it, thr):
        cand = thr | jnp.left_shift(jnp.int32(1), 30 - it)
        return jnp.where(count(aff >= pltpu.bitcast(cand, F32)) >= cap, cand, thr)
    thr = lax.fori_loop(0, 31, bit_body, jnp.zeros((1, N_EXPERTS, 1), jnp.int32))
    thr = pltpu.bitcast(thr, F32)

    tri = (lax.broadcasted_iota(jnp.int32, (LANES, LANES), 0)
           <= lax.broadcasted_iota(jnp.int32, (LANES, LANES), 1))
    tri = jnp.where(tri, 1.0, 0.0).astype(BF16)

    def chunk_cumsum(mask):
        flat = jnp.where(mask, 1.0, 0.0).astype(BF16).reshape(n_chunks * N_EXPERTS, LANES)
        inc = _dot(flat, tri).reshape(n_chunks, N_EXPERTS, LANES)
        run = jnp.zeros((N_EXPERTS, 1), F32)
        before = []
        for c in range(n_chunks):
            before.append(run)
            run = run + inc[c][:, LANES - 1:LANES]
        return inc, jnp.stack(before, axis=0)

    above = aff > thr
    tied = aff == thr
    need = cap - count(above)
    tie_inc, tie_before = chunk_cumsum(tied)
    tie_rank = tie_before + tie_inc - jnp.where(tied, 1.0, 0.0)
    sel = above | (tied & (tie_rank < need))
    sel_inc, sel_before = chunk_cumsum(sel)
    cum_s[...] = sel_inc
    before_s[...] = jnp.broadcast_to(sel_before, sel_inc.shape)

    slot = lax.broadcasted_iota(jnp.int32, (1, cap), 1).astype(F32)
    chunk_id = lax.broadcasted_iota(jnp.int32, (n_chunks, cap), 0).astype(F32)
    lane_id = lax.broadcasted_iota(jnp.int32, (LANES, cap), 0).astype(F32)
    for e in range(N_EXPERTS):
        cin = cum_s[:, e, :]
        cc_exc = before_s[:, e, :][:, 0:1]
        cc_inc = cc_exc + cin[:, LANES - 1:LANES]
        kc = jnp.sum(jnp.where(cc_inc <= slot, 1.0, 0.0), axis=0, keepdims=True)
        onehot = chunk_id == kc
        onehot_b = jnp.where(onehot, 1.0, 0.0).astype(BF16)
        counts_t = _dot_tn(cin.astype(BF16), onehot_b)
        local = slot - jnp.sum(jnp.where(onehot, cc_exc, 0.0), axis=0, keepdims=True)
        il = jnp.sum(jnp.where(counts_t <= local, 1.0, 0.0), axis=0, keepdims=True)
        idx_ref[0, e:e + 1, :] = ((kc * LANES + il) * FEATURE_CHUNKS).astype(jnp.int32)
        aff_t = jnp.zeros((LANES, cap), F32)
        for part in _split_bf16(aff_s[:, e, :], 3):
            aff_t = aff_t + _dot_tn(part, onehot_b)
        gate_ref[0, e:e + 1, :] = jnp.sum(jnp.where(lane_id == il, aff_t, 0.0),
                                          axis=0, keepdims=True)


def _topk(lg4, cap, expert_weights):
    bsz, n_chunks = lg4.shape[0], lg4.shape[1]
    if N_EXPERTS % bsz:
        cast_in, cast_out = (), tuple(w.astype(BF16) for w in expert_weights)
    else:
        cast_in, cast_out = tuple(expert_weights), ()
    per_step = N_EXPERTS // bsz if cast_in else 0
    w_specs = [pl.BlockSpec((per_step,) + w.shape[1:], lambda b: (b, 0, 0)) for w in cast_in]
    outs = pl.pallas_call(
        functools.partial(_topk_kernel, cap=cap, n_cast=len(cast_in)),
        out_shape=(
            jax.ShapeDtypeStruct((bsz, N_EXPERTS, cap), jnp.int32),
            jax.ShapeDtypeStruct((bsz, N_EXPERTS, cap), F32),
        ) + tuple(jax.ShapeDtypeStruct(w.shape, BF16) for w in cast_in),
        grid=(bsz,),
        in_specs=[pl.BlockSpec((1, n_chunks, N_EXPERTS, LANES), lambda b: (b, 0, 0, 0))] + w_specs,
        out_specs=(
            pl.BlockSpec((1, N_EXPERTS, cap), lambda b: (b, 0, 0)),
            pl.BlockSpec((1, N_EXPERTS, cap), lambda b: (b, 0, 0)),
        ) + tuple(w_specs),
        scratch_shapes=[
            pltpu.VMEM((n_chunks, N_EXPERTS, LANES), F32),
            pltpu.VMEM((n_chunks, N_EXPERTS, LANES), F32),
            pltpu.VMEM((n_chunks, N_EXPERTS, LANES), F32),
        ],
        compiler_params=pltpu.CompilerParams(
            dimension_semantics=("arbitrary",), vmem_limit_bytes=VMEM_LIMIT_BYTES),
        name="topk",
    )(lg4, *cast_in)
    return outs[0], outs[1], tuple(outs[2:]) + cast_out


def _moe_kernel(idx_ref, gate_ref, h2t_hbm, x1t_hbm, mod_ref, wg_ref, wu_ref, wd_ref, out_hbm,
                h2_s, acc_s, xs_s, ys_s, stage_s, sems, *, n, cap, sub):
    b = pl.program_id(0)
    e = pl.program_id(1)
    n_b = pl.num_programs(0)
    n_e = pl.num_programs(1)
    tok_rows = n * FEATURE_CHUNKS
    n_parts = cap // sub

    def load_h2(sample):
        return pltpu.make_async_copy(
            h2t_hbm.at[pl.ds(sample * tok_rows, tok_rows), :], h2_s, sems.at[0])

    def load_x1(sample):
        return pltpu.make_async_copy(
            x1t_hbm.at[pl.ds(sample * tok_rows, tok_rows), :], acc_s, sems.at[1])

    def load_x1_chunk(sample, c):
        rows = pl.ds(pl.multiple_of(c * (sub * FEATURE_CHUNKS), sub * FEATURE_CHUNKS),
                     sub * FEATURE_CHUNKS)
        return pltpu.make_async_copy(
            x1t_hbm.at[pl.ds(sample * tok_rows + c * (sub * FEATURE_CHUNKS),
                             sub * FEATURE_CHUNKS), :], acc_s.at[rows, :], sems.at[1])

    gates = gate_ref[0, 0]
    gl = gates.shape[1]
    gates = jnp.pad(gates, ((0, SUBLANES - gates.shape[0]), (0, LANES - gl)))
    gate_cols = gates.T
    gate2 = mod_ref[0, N_MOD - 1:N_MOD, :]

    def token_tile(first_row):
        return pl.ds(pl.multiple_of(first_row, FEATURE_CHUNKS), FEATURE_CHUNKS)

    def gather(part):
        for r in range(sub):
            t = idx_ref[0, 0, 0, part * sub + r]
            xs_s[part, r * FEATURE_CHUNKS:(r + 1) * FEATURE_CHUNKS, :] = h2_s[token_tile(t), :]

    def expert(part):
        x = jnp.concatenate(
            [xs_s[part, pl.ds(s, sub, stride=FEATURE_CHUNKS), :] for s in range(FEATURE_CHUNKS)],
            axis=1).astype(BF16)
        hid = (jax.nn.silu(_dot(x, wg_ref[0])) * _dot(x, wu_ref[0])).astype(BF16)
        y = _dot(hid, wd_ref[0]) * gate2
        first_col = part * sub // gl
        y = jnp.concatenate(
            [y[k * gl:(k + 1) * gl, :] * gate_cols[:gl, first_col + k:first_col + k + 1]
             for k in range(sub // gl)], axis=0)
        for s in range(FEATURE_CHUNKS):
            ys_s[part, pl.ds(s, sub, stride=FEATURE_CHUNKS), :] = y[:, s * LANES:(s + 1) * LANES]

    def scatter(part):
        for r0 in range(0, sub, SCATTER_UNROLL):
            updates = []
            for r in range(r0, r0 + SCATTER_UNROLL):
                dst = token_tile(idx_ref[0, 0, 0, part * sub + r])
                row = ys_s[part, r * FEATURE_CHUNKS:(r + 1) * FEATURE_CHUNKS, :]
                updates.append((dst, acc_s[dst, :] + row))
            for dst, val in updates:
                acc_s[dst, :] = val

    @pl.when((e == 0) & (b == 0))
    def _():
        load_h2(b).start()
        load_x1(b).start()

    @pl.when(e == 0)
    def _():
        load_h2(b).wait()

    for part in range(n_parts):
        gather(part)

    @pl.when((e == n_e - 1) & (b + 1 < n_b))
    def _():
        load_h2(b + 1).start()

    expert(0)

    @pl.when(e == 0)
    def _():
        load_x1(b).wait()

    for part in range(1, n_parts):
        expert(part)
    for part in range(n_parts):
        scatter(part)

    @pl.when(e == n_e - 1)
    def _():
        n_out = n // sub

        def store(c, slot):
            return pltpu.make_async_copy(
                stage_s.at[slot], out_hbm.at[pl.ds(b * n + c * sub, sub), :], sems.at[2 + slot])

        def out_body(c, carry):
            slot = c % 2

            @pl.when(c >= 2)
            def _():
                store(c - 2, slot).wait()
            row0 = pl.multiple_of(c * (sub * FEATURE_CHUNKS), sub * FEATURE_CHUNKS)
            for s in range(FEATURE_CHUNKS):
                stage_s[slot, :, s * LANES:(s + 1) * LANES] = (
                    acc_s[pl.ds(row0 + s, sub, stride=FEATURE_CHUNKS), :])
            store(c, slot).start()

            @pl.when(b + 1 < n_b)
            def _():
                load_x1_chunk(b + 1, c).start()
            return carry
        lax.fori_loop(0, n_out, out_body, 0)
        for c in range(n_out - 2, n_out):
            store(c, c % 2).wait()


def _moe(idx, gate, h2t, x1t, mod3, wg, wu, wd, bsz, n):
    cap = idx.shape[-1]
    sub = min(256, cap)
    gl = min(LANES, cap)
    assert n // sub >= 2 and cap % sub == 0 and sub % SCATTER_UNROLL == 0 and sub % gl == 0
    assert cap // gl <= SUBLANES
    tok_rows = n * FEATURE_CHUNKS
    weight_spec = pl.BlockSpec((1, D_MODEL, D_MODEL), lambda b, e: (e, 0, 0))
    smem_spec = pl.BlockSpec((1, 1, 1, cap), lambda b, e: (b, e, 0, 0),
                             memory_space=pltpu.SMEM)
    return pl.pallas_call(
        functools.partial(_moe_kernel, n=n, cap=cap, sub=sub),
        out_shape=jax.ShapeDtypeStruct((bsz * n, D_MODEL), F32),
        grid=(bsz, N_EXPERTS),
        in_specs=[
            smem_spec,
            pl.BlockSpec((1, 1, cap // gl, gl), lambda b, e: (b, e, 0, 0)),
            pl.BlockSpec(memory_space=pl.ANY),
            pl.BlockSpec(memory_space=pl.ANY),
            pl.BlockSpec((1, N_MOD, D_MODEL), lambda b, e: (b, 0, 0)),
            weight_spec,
            weight_spec,
            weight_spec,
        ],
        out_specs=pl.BlockSpec(memory_space=pl.ANY),
        scratch_shapes=[
            pltpu.VMEM((tok_rows, LANES), F32),
            pltpu.VMEM((tok_rows, LANES), F32),
            pltpu.VMEM((cap // sub, sub * FEATURE_CHUNKS, LANES), F32),
            pltpu.VMEM((cap // sub, sub * FEATURE_CHUNKS, LANES), F32),
            pltpu.VMEM((2, sub, D_MODEL), F32),
            pltpu.SemaphoreType.DMA((4,)),
        ],
        compiler_params=pltpu.CompilerParams(
            dimension_semantics=("arbitrary", "arbitrary"), vmem_limit_bytes=VMEM_LIMIT_BYTES),
        name="moe",
    )(idx[:, :, None, :], gate.reshape(bsz, N_EXPERTS, cap // gl, gl), h2t, x1t, mod3, wg, wu, wd)


def _rope_tables(n):
    t = jnp.arange(n)
    row = (t // GRID_W).astype(F32)
    col = (t % GRID_W).astype(F32)
    n_freq = HEAD_DIM // 4
    inv_freq = ROPE_BASE ** (-jnp.arange(n_freq, dtype=F32) / n_freq)
    ang_r = row[:, None] * inv_freq
    ang_c = col[:, None] * inv_freq
    cos = jnp.concatenate([jnp.cos(ang_r)] * 2 + [jnp.cos(ang_c)] * 2, axis=1)
    sin = jnp.concatenate([-jnp.sin(ang_r), jnp.sin(ang_r), -jnp.sin(ang_c), jnp.sin(ang_c)],
                          axis=1)
    return jnp.tile(cos, (1, 2)), jnp.tile(sin, (1, 2))


def _pack_gate_weights(w_r, b_r, w_i, b_i):
    n_p = LRU_WIDTH // LANES
    zeros = jnp.zeros((LRU_BLOCK_DIM, LRU_BLOCK_DIM), F32)

    def pair(w, p):
        top = jnp.concatenate([w[2 * p], zeros], axis=1)
        bot = jnp.concatenate([zeros, w[2 * p + 1]], axis=1)
        return jnp.concatenate([top, bot], axis=0)

    ws, bs = [], []
    for p in range(n_p):
        ws.append(jnp.concatenate(
            [pair(w_r[0], p), pair(w_i[0], p), pair(w_r[1], p), pair(w_i[1], p)], axis=1))
        sl = slice(p * LANES, (p + 1) * LANES)
        bs.append(jnp.concatenate([b_r[0, sl], b_i[0, sl], b_r[1, sl], b_i[1, sl]])[None, :])
    return jnp.stack(ws).astype(BF16), 0.5 * jnp.stack(bs)


def _layer(x, ctx, mod3, norm1_g, norm2_g, w_in, q_norm_g, k_norm_g, attn_sink, conv_w, conv_b,
           lru_w_r, lru_b_r, lru_w_i, lru_b_i, lru_lambda, w_out, w_router, w_gate, w_up, w_down):
    bsz, n, _ = x.shape
    n_ctx = ctx.shape[1]
    x2 = x.reshape(bsz * n, D_MODEL)
    c2 = ctx.reshape(bsz * n_ctx, D_MODEL)
    g1 = norm1_g[None, :]
    g2 = norm2_g[None, :]
    qg = jnp.tile(q_norm_g, 2)[None, :]
    kg = jnp.tile(k_norm_g, 2)[None, :]
    cos, sin = _rope_tables(n)
    w_in_b = w_in.astype(BF16)
    w_ctx_b = w_in_b[:, ATTN_WIDTH:ATTN_WIDTH + 2 * KV_WIDTH + LRU_WIDTH]

    q_s, k2, v2, xr2, gg2 = _inproj_latent(x2, mod3, g1, w_in_b, qg, kg, cos, sin, bsz, n)
    kc2, vc2, xrc2 = _inproj_ctx(c2, mod3, g1, w_ctx_b, kg, bsz, n_ctx)

    o_s = _attn(attn_sink, q_s, k2.reshape(bsz, n, KV_WIDTH), v2.reshape(bsz, n, KV_WIDTH),
                kc2.reshape(bsz, n_ctx, KV_WIDTH), vc2.reshape(bsz, n_ctx, KV_WIDTH))

    wg_lru, bg_lru = _pack_gate_weights(lru_w_r, lru_b_r, lru_w_i, lru_b_i)
    rg3 = _lru(xr2.reshape(bsz, n, LRU_WIDTH), xrc2.reshape(bsz, n_ctx, LRU_WIDTH),
               gg2.reshape(bsz, n, LRU_WIDTH), 0.5 * conv_w, 0.5 * conv_b[None, :], wg_lru,
               bg_lru, lru_lambda)

    w_att = (w_out[:ATTN_WIDTH].reshape(N_KV_HEADS, Q_PER_KV, HEAD_DIM, D_MODEL)
             .transpose(1, 0, 2, 3).reshape(ATTN_WIDTH, D_MODEL).astype(BF16))
    w_rnn = w_out[ATTN_WIDTH:].astype(BF16)
    x1t, h2t, lg4 = _outproj(o_s, rg3.reshape(bsz * n, LRU_WIDTH), x2, mod3, w_att, w_rnn, g2,
                             w_router.T.astype(BF16), bsz, n)

    cap = CAPACITY_FACTOR * n // N_EXPERTS
    idx, gate, (wg_b, wu_b, wd_b) = _topk(lg4, cap, (w_gate, w_up, w_down))
    out = _moe(idx, gate, h2t, x1t, mod3, wg_b, wu_b, wd_b, bsz, n)
    return out.reshape(bsz, n, D_MODEL)


def kernel(x, c, ctx, c_ctx, w_ada, b_ada, norm1_g, norm2_g, w_in, q_norm_g, k_norm_g, attn_sink, conv_w, conv_b, lru_w_r, lru_b_r, lru_w_i, lru_b_i, lru_lambda, w_out, w_router, w_gate, w_up, w_down):
    bsz = x.shape[0]
    depth = w_ada.shape[0]
    assert depth == 1, "context-stream update between layers is not implemented"
    pad_rows = -(bsz + 1) % SUBLANES
    cc = jnp.concatenate([c, jnp.zeros((pad_rows, D_MODEL), F32), c_ctx[None, :]], axis=0)
    for layer in range(depth):
        mod = _ada(cc, w_ada[layer], b_ada[layer][None, :])
        mod3 = mod.reshape(cc.shape[0], N_MOD, D_MODEL)
        x = _layer(x, ctx, mod3, norm1_g[layer], norm2_g[layer], w_in[layer], q_norm_g[layer],
                   k_norm_g[layer], attn_sink[layer], conv_w[layer], conv_b[layer],
                   lru_w_r[layer], lru_b_r[layer], lru_w_i[layer], lru_b_i[layer],
                   lru_lambda[layer], w_out[layer], w_router[layer], w_gate[layer],
                   w_up[layer], w_down[layer])
    return x
```

```python
import functools

import jax
import jax.numpy as jnp
from jax import lax
from jax.experimental import pallas as pl
from jax.experimental.pallas import tpu as pltpu

F32 = jnp.float32
BF16 = jnp.bfloat16

LANES = 128
SUBLANES = 8
VMEM_LIMIT_BYTES = 56 * 1024 * 1024

D_MODEL = 1024
HEAD_DIM = 64
N_Q_HEADS = 8
N_KV_HEADS = 2
Q_PER_KV = N_Q_HEADS // N_KV_HEADS
ATTN_WIDTH = N_Q_HEADS * HEAD_DIM
KV_WIDTH = N_KV_HEADS * HEAD_DIM
LRU_WIDTH = D_MODEL - ATTN_WIDTH
LRU_BLOCK_DIM = 64
IN_WIDTH = ATTN_WIDTH + 2 * KV_WIDTH + 2 * LRU_WIDTH
WINDOW = 128
BLOCK = 128
GRID_W = 64
ROPE_BASE = 10000.0
LRU_C = 8.0
N_EXPERTS = 16
CAPACITY_FACTOR = 2
EPS = 1e-6
NEG_INF = -1e30
LOG2_E = 1.4426950408889634
SCORE_SCALE = HEAD_DIM ** -0.5 * LOG2_E
N_MOD = 6
FEATURE_CHUNKS = D_MODEL // LANES
CAST_ROWS = 128
SCATTER_UNROLL = 16


def _dot(a, b):
    return jnp.dot(a, b, preferred_element_type=F32)


def _dot_nt(a, b):
    return lax.dot_general(a, b, (((1,), (1,)), ((), ())), preferred_element_type=F32)


def _dot_tn(a, b):
    return lax.dot_general(a, b, (((0,), (0,)), ((), ())), preferred_element_type=F32)


def _split_bf16(x, parts):
    out = []
    r = x
    for _ in range(parts):
        p = r.astype(BF16)
        out.append(p)
        r = r - p.astype(F32)
    return out


def _rmsnorm_rows(x, g):
    ms = jnp.mean(x * x, axis=-1, keepdims=True)
    return x * lax.rsqrt(ms + EPS) * g


def _ada_kernel(c_ref, w_ref, b_ref, o_ref):
    s = jax.nn.silu(c_ref[...])
    s_hi, s_lo = _split_bf16(s, 2)
    w_hi, w_lo = _split_bf16(w_ref[...], 2)
    o_ref[...] = _dot(s_hi, w_hi) + _dot(s_hi, w_lo) + _dot(s_lo, w_hi) + b_ref[...]


def _ada(cc, w_ada, b_ada):
    rows = cc.shape[0]
    width = w_ada.shape[1]
    return pl.pallas_call(
        _ada_kernel,
        out_shape=jax.ShapeDtypeStruct((rows, width), F32),
        grid=(width // D_MODEL,),
        in_specs=[
            pl.BlockSpec((rows, D_MODEL), lambda j: (0, 0)),
            pl.BlockSpec((D_MODEL, D_MODEL), lambda j: (0, j)),
            pl.BlockSpec((1, D_MODEL), lambda j: (0, j)),
        ],
        out_specs=pl.BlockSpec((rows, D_MODEL), lambda j: (0, j)),
        compiler_params=pltpu.CompilerParams(
            dimension_semantics=("arbitrary",), vmem_limit_bytes=VMEM_LIMIT_BYTES),
        name="ada",
    )(cc, w_ada, b_ada)


def _head_norm(p, g):
    lane = lax.broadcasted_iota(jnp.int32, p.shape, 1)
    low = lane < HEAD_DIM
    sq = p * p
    s_low = jnp.sum(jnp.where(low, sq, 0.0), axis=-1, keepdims=True)
    s_high = jnp.sum(jnp.where(low, 0.0, sq), axis=-1, keepdims=True)
    ms = jnp.where(low, s_low, s_high) * (1.0 / HEAD_DIM)
    return p * lax.rsqrt(ms + EPS) * g


def _rope(p, cos, sin_signed):
    lane = lax.broadcasted_iota(jnp.int32, p.shape, 1)
    partner = jnp.where((lane & 16) == 0,
                        pltpu.roll(p, LANES - 16, 1), pltpu.roll(p, 16, 1))
    return p * cos + partner * sin_signed


def _inproj_latent_kernel(x_ref, mod_ref, g1_ref, w_ref, qg_ref, kg_ref, cos_ref, sin_ref,
                          q_ref, k_ref, v_ref, xr_ref, gg_ref):
    x = x_ref[...]
    h = _rmsnorm_rows(x, g1_ref[...]) * (1.0 + mod_ref[0, 1:2, :]) + mod_ref[0, 0:1, :]
    y = _dot(h.astype(BF16), w_ref[...])
    cos = cos_ref[...]
    sin = sin_ref[...]
    pieces = []
    for j in range(ATTN_WIDTH // LANES):
        p = _head_norm(y[:, j * LANES:(j + 1) * LANES], qg_ref[...])
        pieces.append(_rope(p, cos, sin))
    lane = lax.broadcasted_iota(jnp.int32, pieces[0].shape, 1)
    low = lane < HEAD_DIM
    n_blocks = y.shape[0] // BLOCK
    for g in range(Q_PER_KV):
        src0 = pieces[g // 2]
        src1 = pieces[2 + g // 2]
        if g % 2 == 0:
            out = jnp.where(low, src0, pltpu.roll(src1, HEAD_DIM, 1))
        else:
            out = jnp.where(low, pltpu.roll(src0, HEAD_DIM, 1), src1)
        out = (out * SCORE_SCALE).astype(BF16)
        for j in range(n_blocks):
            q_ref[0, j, g * BLOCK:(g + 1) * BLOCK, :] = out[j * BLOCK:(j + 1) * BLOCK, :]
    k = _head_norm(y[:, ATTN_WIDTH:ATTN_WIDTH + KV_WIDTH], kg_ref[...])
    k_ref[...] = _rope(k, cos, sin).astype(BF16)
    o = ATTN_WIDTH + KV_WIDTH
    v_ref[...] = y[:, o:o + KV_WIDTH].astype(BF16)
    o += KV_WIDTH
    xr_ref[...] = y[:, o:o + LRU_WIDTH]
    o += LRU_WIDTH
    gg_ref[...] = jax.nn.gelu(y[:, o:o + LRU_WIDTH])


def _inproj_ctx_kernel(x_ref, mod_ref, g1_ref, w_ref, kg_ref, k_ref, v_ref, xr_ref):
    x = x_ref[...]
    h = _rmsnorm_rows(x, g1_ref[...]) * (1.0 + mod_ref[0, 1:2, :]) + mod_ref[0, 0:1, :]
    y = _dot(h.astype(BF16), w_ref[...])
    k_ref[...] = _head_norm(y[:, 0:KV_WIDTH], kg_ref[...]).astype(BF16)
    v_ref[...] = y[:, KV_WIDTH:2 * KV_WIDTH].astype(BF16)
    xr_ref[...] = y[:, 2 * KV_WIDTH:2 * KV_WIDTH + LRU_WIDTH]


OUTPROJ_ROW_TILE = 1024


def _row_tile(n):
    return min(512, n)


def _inproj_latent(x2, mod3, g1, w_in, qg, kg, cos, sin, bsz, n):
    tm = _row_tile(n)
    tps = n // tm
    rows = bsz * n
    const = lambda i: (0, 0)
    return pl.pallas_call(
        _inproj_latent_kernel,
        out_shape=(
            jax.ShapeDtypeStruct((bsz, n // BLOCK, Q_PER_KV * BLOCK, LANES), BF16),
            jax.ShapeDtypeStruct((rows, KV_WIDTH), BF16),
            jax.ShapeDtypeStruct((rows, KV_WIDTH), BF16),
            jax.ShapeDtypeStruct((rows, LRU_WIDTH), F32),
            jax.ShapeDtypeStruct((rows, LRU_WIDTH), F32),
        ),
        grid=(rows // tm,),
        in_specs=[
            pl.BlockSpec((tm, D_MODEL), lambda i: (i, 0)),
            pl.BlockSpec((1, N_MOD, D_MODEL), lambda i: (i // tps, 0, 0)),
            pl.BlockSpec((1, D_MODEL), const),
            pl.BlockSpec((D_MODEL, IN_WIDTH), const),
            pl.BlockSpec((1, LANES), const),
            pl.BlockSpec((1, LANES), const),
            pl.BlockSpec((tm, LANES), lambda i: (i % tps, 0)),
            pl.BlockSpec((tm, LANES), lambda i: (i % tps, 0)),
        ],
        out_specs=(
            pl.BlockSpec((1, tm // BLOCK, Q_PER_KV * BLOCK, LANES),
                         lambda i: (i // tps, i % tps, 0, 0)),
            pl.BlockSpec((tm, KV_WIDTH), lambda i: (i, 0)),
            pl.BlockSpec((tm, KV_WIDTH), lambda i: (i, 0)),
            pl.BlockSpec((tm, LRU_WIDTH), lambda i: (i, 0)),
            pl.BlockSpec((tm, LRU_WIDTH), lambda i: (i, 0)),
        ),
        compiler_params=pltpu.CompilerParams(
            dimension_semantics=("arbitrary",), vmem_limit_bytes=VMEM_LIMIT_BYTES),
        name="inproj_latent",
    )(x2, mod3, g1, w_in, qg, kg, cos, sin)


def _inproj_ctx(c2, mod3, g1, w_ctx, kg, bsz, n_ctx):
    tm = _row_tile(n_ctx)
    rows = bsz * n_ctx
    ctx_row = mod3.shape[0] - 1
    width = w_ctx.shape[1]
    const = lambda i: (0, 0)
    return pl.pallas_call(
        _inproj_ctx_kernel,
        out_shape=(
            jax.ShapeDtypeStruct((rows, KV_WIDTH), BF16),
            jax.ShapeDtypeStruct((rows, KV_WIDTH), BF16),
            jax.ShapeDtypeStruct((rows, LRU_WIDTH), F32),
        ),
        grid=(rows // tm,),
        in_specs=[
            pl.BlockSpec((tm, D_MODEL), lambda i: (i, 0)),
            pl.BlockSpec((1, N_MOD, D_MODEL), lambda i: (ctx_row, 0, 0)),
            pl.BlockSpec((1, D_MODEL), const),
            pl.BlockSpec((D_MODEL, width), const),
            pl.BlockSpec((1, LANES), const),
        ],
        out_specs=(
            pl.BlockSpec((tm, KV_WIDTH), lambda i: (i, 0)),
            pl.BlockSpec((tm, KV_WIDTH), lambda i: (i, 0)),
            pl.BlockSpec((tm, LRU_WIDTH), lambda i: (i, 0)),
        ),
        compiler_params=pltpu.CompilerParams(
            dimension_semantics=("arbitrary",), vmem_limit_bytes=VMEM_LIMIT_BYTES),
        name="inproj_ctx",
    )(c2, mod3, g1, w_ctx, kg)


def _attn_kernel(sink_ref, q_ref, k_ref, v_ref, kc_ref, vc_ref, bias_ref, o_ref, s0_s, s1_s,
                 *, n, nb):
    t = pl.program_id(0)
    last = pl.num_programs(0) - 2
    j_scored = jnp.minimum(t, last) % nb
    j_finished = jnp.maximum(t - 1, 0) % nb
    span = 3 * BLOCK

    def window_start(i):
        return pl.multiple_of(jnp.clip((i - 1) * BLOCK, 0, n - span), BLOCK)

    def head_only(t, h):
        lane = lax.broadcasted_iota(jnp.int32, t.shape, 1)
        keep = (lane < HEAD_DIM) if h == 0 else (lane >= HEAD_DIM)
        return jnp.where(keep, t, jnp.zeros_like(t))

    stages = functools.partial(_attn_stages, sink_ref, q_ref, k_ref, v_ref, kc_ref, vc_ref,
                               bias_ref, o_ref, start_scored=window_start(j_scored),
                               start_finished=window_start(j_finished), head_only=head_only)

    @pl.when(t == 0)
    def _():
        s1_s[...] = jnp.zeros_like(s1_s)

    @pl.when(t % 2 == 0)
    def _():
        stages(s0_s, s1_s)

    @pl.when(t % 2 == 1)
    def _():
        stages(s1_s, s0_s)


def _attn_stages(sink_ref, q_ref, k_ref, v_ref, kc_ref, vc_ref, bias_ref, o_ref, s_new, s_old,
                 start_scored, start_finished, head_only):
    span = 3 * BLOCK

    def with_ones(t, lane_index):
        lane = lax.broadcasted_iota(jnp.int32, t.shape, 1)
        return jnp.where(lane == lane_index, jnp.ones_like(t), t)

    q = q_ref[0, 0]
    kw = k_ref[0, pl.ds(start_scored, span), :]
    kc = kc_ref[0]
    bias = bias_ref[0]
    for h in range(N_KV_HEADS):
        s_new[h, :, 0:span] = _dot_nt(q, head_only(kw, h)) + bias
        s_new[h, :, span:] = _dot_nt(q, head_only(kc, h))

    vw = v_ref[0, pl.ds(start_finished, span), :]
    vc = vc_ref[0]
    rows = q.shape[0]
    group = lax.broadcasted_iota(jnp.int32, (rows, 1), 0) // BLOCK
    acc = jnp.zeros((rows, LANES), F32)
    for h in range(N_KV_HEADS):
        s_loc = s_old[h, :, 0:span]
        s_ctx = s_old[h, :, span:]
        sink = jnp.zeros((rows, 1), F32)
        for g in range(Q_PER_KV):
            sink = jnp.where(group == g, sink_ref[h * Q_PER_KV + g] * LOG2_E, sink)
        m = jnp.maximum(jnp.maximum(jnp.max(s_loc, axis=-1, keepdims=True),
                                    jnp.max(s_ctx, axis=-1, keepdims=True)), sink)
        p_loc = jnp.exp2(s_loc - m).astype(BF16)
        p_ctx = jnp.exp2(s_ctx - m).astype(BF16)
        sum_lane = HEAD_DIM if h == 0 else 0
        o = (_dot(p_loc, with_ones(head_only(vw, h), sum_lane))
             + _dot(p_ctx, with_ones(head_only(vc, h), sum_lane)))
        denom = o[:, sum_lane:sum_lane + 1] + jnp.exp2(sink - m)
        lane = lax.broadcasted_iota(jnp.int32, o.shape, 1)
        own = (lane < HEAD_DIM) if h == 0 else (lane >= HEAD_DIM)
        acc = acc + jnp.where(own, o, 0.0) / denom
    o_ref[0, 0] = acc.astype(BF16)


def _band_bias(n):
    span = 3 * BLOCK
    r = jnp.arange(Q_PER_KV * BLOCK)[:, None] % BLOCK
    c = jnp.arange(span)[None, :]
    offsets = jnp.arange(span // BLOCK)[:, None, None] * BLOCK
    valid = jnp.abs(r - c + offsets) <= WINDOW
    return jnp.where(valid, 0.0, NEG_INF).astype(F32)


def _attn(sink, q_s, k3, v3, kc3, vc3):
    bsz, nb = q_s.shape[0], q_s.shape[1]
    n = k3.shape[1]
    n_ctx = kc3.shape[1]
    assert nb >= 3
    bias = _band_bias(n)

    n_blocks = bsz * nb

    def scored(t):
        return jnp.divmod(jnp.minimum(t, n_blocks - 1), nb)

    def finished(t):
        return jnp.divmod(jnp.maximum(t - 1, 0), nb)

    def placement(t):
        i = scored(t)[1]
        return (jnp.minimum(i, 1) + (i == nb - 1).astype(jnp.int32), 0, 0)

    block = (1, 1, Q_PER_KV * BLOCK, LANES)

    return pl.pallas_call(
        functools.partial(_attn_kernel, n=n, nb=nb),
        out_shape=jax.ShapeDtypeStruct(q_s.shape, BF16),
        grid=(n_blocks + 1,),
        in_specs=[
            pl.BlockSpec(memory_space=pltpu.SMEM),
            pl.BlockSpec(block, lambda t: (*scored(t), 0, 0)),
            pl.BlockSpec((1, n, KV_WIDTH), lambda t: (scored(t)[0], 0, 0)),
            pl.BlockSpec((1, n, KV_WIDTH), lambda t: (finished(t)[0], 0, 0)),
            pl.BlockSpec((1, n_ctx, KV_WIDTH), lambda t: (scored(t)[0], 0, 0)),
            pl.BlockSpec((1, n_ctx, KV_WIDTH), lambda t: (finished(t)[0], 0, 0)),
            pl.BlockSpec((1, Q_PER_KV * BLOCK, 3 * BLOCK), placement),
        ],
        out_specs=pl.BlockSpec(block, lambda t: (*finished(t), 0, 0)),
        scratch_shapes=[
            pltpu.VMEM((N_KV_HEADS, Q_PER_KV * BLOCK, 3 * BLOCK + n_ctx), F32),
            pltpu.VMEM((N_KV_HEADS, Q_PER_KV * BLOCK, 3 * BLOCK + n_ctx), F32),
        ],
        compiler_params=pltpu.CompilerParams(
            dimension_semantics=("arbitrary",), vmem_limit_bytes=VMEM_LIMIT_BYTES),
        name="attn",
    )(sink, q_s, k3, v3, kc3, vc3, bias)


def _softplus(z):
    return jnp.maximum(z, 0.0) + jnp.log1p(jnp.exp(-jnp.abs(z)))


LRU_MAIN_SEG = 252


def _lru_plan(length):
    main, rest = divmod(length // SUBLANES, LRU_MAIN_SEG)
    if rest == 0:
        tails = ()
    elif rest % SUBLANES == 0 and rest > SUBLANES:
        tails = (rest - 5, 5)
    else:
        tails = (rest,)
    assert length % SUBLANES == 0 and all(t >= 2 for t in tails)
    return main, tails


def _lru_kernel(xr_ref, xrc_ref, gg_ref, cw_ref, cb_ref, wg_ref, bg_ref, lam_ref, o_ref,
                xc_s, xcc_s, hf_s, hb_s, *, n, n_ctx):
    cw = cw_ref[...]
    cb = cb_ref[...]
    sub = lax.broadcasted_iota(jnp.int32, (SUBLANES, LANES), 0)
    main_rows = SUBLANES * LRU_MAIN_SEG
    n_main, tails = _lru_plan(n)
    c_main, c_tails = _lru_plan(n_ctx)

    def static_tiles(n_main_tiles, tail_segs, with_main):
        tiles = [(k * main_rows, LRU_MAIN_SEG) for k in range(n_main_tiles)] if with_main else []
        t0 = n_main_tiles * main_rows
        for seg in tail_segs:
            tiles.append((t0, seg))
            t0 += SUBLANES * seg
        return tiles

    def aligned(v):
        return v if isinstance(v, int) else pl.multiple_of(v, SUBLANES)

    def conv_tile(src_ref, dst_ref, t0, seg, length):
        x = [src_ref[0, pl.ds(t0 + j, SUBLANES, stride=seg), :] for j in range(seg)]
        lo = aligned(jnp.maximum(t0 - SUBLANES, 0))
        hi = aligned(jnp.minimum(t0 + SUBLANES * seg, length - SUBLANES))
        prev = jnp.where(t0 > 0, src_ref[0, pl.ds(lo, SUBLANES), :], 0.0)
        nxt = jnp.where(t0 + SUBLANES * seg < length, src_ref[0, pl.ds(hi, SUBLANES), :], 0.0)

        def from_prev_segment(v, row):
            return jnp.where(sub == 0, row, pltpu.roll(v, 1, 0))

        def from_next_segment(v, row):
            return jnp.where(sub == SUBLANES - 1, row, pltpu.roll(v, SUBLANES - 1, 0))

        m1 = [from_prev_segment(x[seg - 1], prev[7:8])] + x[:seg - 1]
        m2 = [from_prev_segment(x[seg - 2], prev[6:7])] + m1[:seg - 1]
        p1 = x[1:] + [from_next_segment(x[0], nxt[0:1])]
        for j in range(seg):
            y = cb + cw[0:1] * m2[j]
            y = y + cw[1:2] * m1[j]
            y = y + cw[2:3] * x[j]
            y = y + cw[3:4] * p1[j]
            dst_ref[pl.ds(t0 + j * SUBLANES, SUBLANES), :] = y

    half_rate = [-0.5 * LRU_C * _softplus(-lam_ref[d:d + 1, :]) for d in range(2)]

    def coeffs(xc, d):
        w = wg_ref[0, :, 2 * LANES * d:2 * LANES * (d + 1)]
        g = _dot(xc.astype(BF16), w) + bg_ref[0, :, 2 * LANES * d:2 * LANES * (d + 1)]
        t_r = jnp.tanh(g[:, :LANES])
        t_i = jnp.tanh(g[:, LANES:])
        log_a = t_r * half_rate[d] + half_rate[d]
        a = jnp.exp(log_a)
        mult = jnp.sqrt(-jnp.tanh(log_a) * (a * a + 1.0))
        return a, mult * (t_i + 1.0) * xc

    def scan_tile(a, b, h_in, seg, reverse):
        order = list(range(seg - 1, -1, -1) if reverse else range(seg))
        local = [None] * seg
        prod = [None] * seg
        h = p = None
        for j in order:
            aj = a[j * SUBLANES:(j + 1) * SUBLANES, :]
            bj = b[j * SUBLANES:(j + 1) * SUBLANES, :]
            h = bj if h is None else aj * h + bj
            p = aj if p is None else aj * p
            local[j], prod[j] = h, p
        end = order[-1]
        carry = jnp.zeros((SUBLANES, LANES), F32)
        c = h_in
        for s in (range(SUBLANES - 1, -1, -1) if reverse else range(SUBLANES)):
            carry = jnp.where(sub == s, c, carry)
            c = prod[end][s:s + 1, :] * c + local[end][s:s + 1, :]
        return [local[j] + prod[j] * carry for j in range(seg)], c

    def run_tile(xc_ref, t0, seg, d, h, emit=None):
        a, b = coeffs(xc_ref[pl.ds(t0, SUBLANES * seg), :], d)
        states, h = scan_tile(a, b, h, seg, d == 1)
        if emit is not None:
            emit(t0, seg, states)
        return h

    def keep(dst_ref):
        def emit(t0, seg, states):
            for j in range(seg):
                dst_ref[pl.ds(t0 + j * SUBLANES, SUBLANES), :] = states[j]
        return emit

    def write_output(other_ref):
        def emit(t0, seg, states):
            for j in range(seg):
                rows = pl.ds(t0 + j, SUBLANES, stride=seg)
                other = other_ref[pl.ds(t0 + j * SUBLANES, SUBLANES), :]
                o_ref[0, rows, :] = (other + states[j]) * gg_ref[0, rows, :]
        return emit

    def main_t0(k):
        return pl.multiple_of(k * main_rows, SUBLANES)

    for t0, seg in static_tiles(c_main, c_tails, True):
        conv_tile(xrc_ref, xcc_s, t0, seg, n_ctx)

    def conv_body(k, carry):
        conv_tile(xr_ref, xc_s, main_t0(k), LRU_MAIN_SEG, n)
        return carry
    lax.fori_loop(0, n_main, conv_body, 0)
    for t0, seg in static_tiles(n_main, tails, False):
        conv_tile(xr_ref, xc_s, t0, seg, n)

    hf = jnp.zeros((1, LANES), F32)
    hb = jnp.zeros((1, LANES), F32)
    for t0, seg in static_tiles(c_main, c_tails, True):
        hf = run_tile(xcc_s, t0, seg, 0, hf)
    for t0, seg in reversed(static_tiles(c_main, c_tails, True)):
        hb = run_tile(xcc_s, t0, seg, 1, hb)
    for t0, seg in reversed(static_tiles(n_main, tails, False)):
        hb = run_tile(xc_s, t0, seg, 1, hb, keep(hb_s))

    def pair_body(first_visit):
        def body(k, carry):
            hf, hb = carry
            fwd_emit = keep(hf_s) if first_visit else write_output(hb_s)
            rev_emit = keep(hb_s) if first_visit else write_output(hf_s)
            hf = run_tile(xc_s, main_t0(k), LRU_MAIN_SEG, 0, hf, fwd_emit)
            hb = run_tile(xc_s, main_t0(n_main - 1 - k), LRU_MAIN_SEG, 1, hb, rev_emit)
            return hf, hb
        return body

    half = n_main // 2
    hf, hb = lax.fori_loop(0, half, pair_body(True), (hf, hb))
    if n_main % 2:
        hf = run_tile(xc_s, half * main_rows, LRU_MAIN_SEG, 0, hf, keep(hf_s))
        hb = run_tile(xc_s, half * main_rows, LRU_MAIN_SEG, 1, hb, write_output(hf_s))
    hf, hb = lax.fori_loop(n_main - half, n_main, pair_body(False), (hf, hb))
    for t0, seg in static_tiles(n_main, tails, False):
        hf = run_tile(xc_s, t0, seg, 0, hf, write_output(hb_s))


def _lru(xr3, xrc3, gg3, conv_w, conv_b, wg, bg, lam):
    bsz, n, _ = xr3.shape
    n_ctx = xrc3.shape[1]
    n_p = LRU_WIDTH // LANES
    return pl.pallas_call(
        functools.partial(_lru_kernel, n=n, n_ctx=n_ctx),
        out_shape=jax.ShapeDtypeStruct((bsz, n, LRU_WIDTH), F32),
        grid=(bsz, n_p),
        in_specs=[
            pl.BlockSpec((1, n, LANES), lambda b, p: (b, 0, p)),
            pl.BlockSpec((1, n_ctx, LANES), lambda b, p: (b, 0, p)),
            pl.BlockSpec((1, n, LANES), lambda b, p: (b, 0, p)),
            pl.BlockSpec((conv_w.shape[0], LANES), lambda b, p: (0, p)),
            pl.BlockSpec((1, LANES), lambda b, p: (0, p)),
            pl.BlockSpec((1, LANES, 4 * LANES), lambda b, p: (p, 0, 0)),
            pl.BlockSpec((1, 1, 4 * LANES), lambda b, p: (p, 0, 0)),
            pl.BlockSpec((2, LANES), lambda b, p: (0, p)),
        ],
        out_specs=pl.BlockSpec((1, n, LANES), lambda b, p: (b, 0, p)),
        scratch_shapes=[
            pltpu.VMEM((n, LANES), F32),
            pltpu.VMEM((n_ctx, LANES), F32),
            pltpu.VMEM((n, LANES), F32),
            pltpu.VMEM((n, LANES), F32),
        ],
        compiler_params=pltpu.CompilerParams(
            dimension_semantics=("arbitrary", "arbitrary"), vmem_limit_bytes=VMEM_LIMIT_BYTES),
        name="lru",
    )(xr3, xrc3, gg3, conv_w, conv_b, wg, bg, lam)


def _outproj_kernel(o_ref, rg_ref, x_ref, mod_ref, watt_ref, wrnn_ref, g2_ref, wr_ref,
                    x1t_ref, h2t_ref, lg_ref):
    tm = x_ref.shape[0]
    n_blocks = tm // BLOCK
    att = jnp.concatenate(
        [jnp.concatenate([o_ref[0, j, g * BLOCK:(g + 1) * BLOCK, :] for g in range(Q_PER_KV)],
                         axis=1) for j in range(n_blocks)], axis=0)
    y = _dot(att, watt_ref[...]) + _dot(rg_ref[...].astype(BF16), wrnn_ref[...])
    x1 = x_ref[...] + mod_ref[0, 2:3, :] * y
    h2 = _rmsnorm_rows(x1, g2_ref[...]) * (1.0 + mod_ref[0, 4:5, :]) + mod_ref[0, 3:4, :]
    for s in range(FEATURE_CHUNKS):
        rows = pl.ds(s, tm, stride=FEATURE_CHUNKS)
        x1t_ref[rows, :] = x1[:, s * LANES:(s + 1) * LANES]
        h2t_ref[rows, :] = h2[:, s * LANES:(s + 1) * LANES]
    lt = _dot_nt(wr_ref[...], h2.astype(BF16))
    for j in range(n_blocks):
        lg_ref[0, j] = lt[:, j * LANES:(j + 1) * LANES]


def _outproj(o_s, rg2, x2, mod3, w_att, w_rnn, g2, wr_t, bsz, n):
    tm = min(OUTPROJ_ROW_TILE, n)
    tps = n // tm
    rows = bsz * n
    const = lambda i: (0, 0)
    return pl.pallas_call(
        _outproj_kernel,
        out_shape=(
            jax.ShapeDtypeStruct((rows * FEATURE_CHUNKS, LANES), F32),
            jax.ShapeDtypeStruct((rows * FEATURE_CHUNKS, LANES), F32),
            jax.ShapeDtypeStruct((bsz, n // LANES, N_EXPERTS, LANES), F32),
        ),
        grid=(rows // tm,),
        in_specs=[
            pl.BlockSpec((1, tm // BLOCK, Q_PER_KV * BLOCK, LANES),
                         lambda i: (i // tps, i % tps, 0, 0)),
            pl.BlockSpec((tm, LRU_WIDTH), lambda i: (i, 0)),
            pl.BlockSpec((tm, D_MODEL), lambda i: (i, 0)),
            pl.BlockSpec((1, N_MOD, D_MODEL), lambda i: (i // tps, 0, 0)),
            pl.BlockSpec((ATTN_WIDTH, D_MODEL), const),
            pl.BlockSpec((LRU_WIDTH, D_MODEL), const),
            pl.BlockSpec((1, D_MODEL), const),
            pl.BlockSpec((N_EXPERTS, D_MODEL), const),
        ],
        out_specs=(
            pl.BlockSpec((tm * FEATURE_CHUNKS, LANES), lambda i: (i, 0)),
            pl.BlockSpec((tm * FEATURE_CHUNKS, LANES), lambda i: (i, 0)),
            pl.BlockSpec((1, tm // LANES, N_EXPERTS, LANES), lambda i: (i // tps, i % tps, 0, 0)),
        ),
        compiler_params=pltpu.CompilerParams(
            dimension_semantics=("arbitrary",), vmem_limit_bytes=VMEM_LIMIT_BYTES),
        name="outproj",
    )(o_s, rg2, x2, mod3, w_att, w_rnn, g2, wr_t)


def _topk_kernel(lg_ref, *refs, cap, n_cast):
    w_refs = refs[:n_cast]
    idx_ref, gate_ref = refs[n_cast:n_cast + 2]
    wb_refs = refs[n_cast + 2:2 * n_cast + 2]
    aff_s, cum_s, before_s = refs[2 * n_cast + 2:]
    for w_ref, wb_ref in zip(w_refs, wb_refs):
        def cast_rows(i, carry):
            rows = pl.ds(pl.multiple_of(i * CAST_ROWS, CAST_ROWS), CAST_ROWS)
            for k in range(w_ref.shape[0]):
                wb_ref[k, rows, :] = w_ref[k, rows, :].astype(BF16)
            return carry
        lax.fori_loop(0, w_ref.shape[1] // CAST_ROWS, cast_rows, 0)

    lg = lg_ref[0]
    n_chunks = lg.shape[0]
    m = jnp.max(lg, axis=1, keepdims=True)
    ex = jnp.exp(lg - m)
    aff = ex / jnp.sum(ex, axis=1, keepdims=True)
    aff_s[...] = aff

    def count(mask):
        c = jnp.sum(jnp.where(mask, 1.0, 0.0), axis=0, keepdims=True)
        return jnp.sum(c, axis=2, keepdims=True)

    def bit_body(it, thr):
        cand = thr | jnp.left_shift(jnp.int32(1), 30 - it)
        return jnp.where(count(aff >= pltpu.bitcast(cand, F32)) >= cap, cand, thr)
    thr = lax.fori_loop(0, 31, bit_body, jnp.zeros((1, N_EXPERTS, 1), jnp.int32))
    thr = pltpu.bitcast(thr, F32)

    tri = (lax.broadcasted_iota(jnp.int32, (LANES, LANES), 0)
           <= lax.broadcasted_iota(jnp.int32, (LANES, LANES), 1))
    tri = jnp.where(tri, 1.0, 0.0).astype(BF16)

    def chunk_cumsum(mask):
        flat = jnp.where(mask, 1.0, 0.0).astype(BF16).reshape(n_chunks * N_EXPERTS, LANES)
        inc = _dot(flat, tri).reshape(n_chunks, N_EXPERTS, LANES)
        run = jnp.zeros((N_EXPERTS, 1), F32)
        before = []
        for c in range(n_chunks):
            before.append(run)
            run = run + inc[c][:, LANES - 1:LANES]
        return inc, jnp.stack(before, axis=0)

    above = aff > thr
    tied = aff == thr
    need = cap - count(above)
    tie_inc, tie_before = chunk_cumsum(tied)
    tie_rank = tie_before + tie_inc - jnp.where(tied, 1.0, 0.0)
    sel = above | (tied & (tie_rank < need))
    sel_inc, sel_before = chunk_cumsum(sel)
    cum_s[...] = sel_inc
    before_s[...] = jnp.broadcast_to(sel_before, sel_inc.shape)

    slot = lax.broadcasted_iota(jnp.int32, (1, cap), 1).astype(F32)
    chunk_id = lax.broadcasted_iota(jnp.int32, (n_chunks, cap), 0).astype(F32)
    lane_id = lax.broadcasted_iota(jnp.int32, (LANES, cap), 0).astype(F32)
    for e in range(N_EXPERTS):
        cin = cum_s[:, e, :]
        cc_exc = before_s[:, e, :][:, 0:1]
        cc_inc = cc_exc + cin[:, LANES - 1:LANES]
        kc = jnp.sum(jnp.where(cc_inc <= slot, 1.0, 0.0), axis=0, keepdims=True)
        onehot = chunk_id == kc
        onehot_b = jnp.where(onehot, 1.0, 0.0).astype(BF16)
        counts_t = _dot_tn(cin.astype(BF16), onehot_b)
        local = slot - jnp.sum(jnp.where(onehot, cc_exc, 0.0), axis=0, keepdims=True)
        il = jnp.sum(jnp.where(counts_t <= local, 1.0, 0.0), axis=0, keepdims=True)
        idx_ref[0, e:e + 1, :] = ((kc * LANES + il) * FEATURE_CHUNKS).astype(jnp.int32)
        aff_t = jnp.zeros((LANES, cap), F32)
        for part in _split_bf16(aff_s[:, e, :], 3):
            aff_t = aff_t + _dot_tn(part, onehot_b)
        gate_ref[0, e:e + 1, :] = jnp.sum(jnp.where(lane_id == il, aff_t, 0.0),
                                          axis=0, keepdims=True)


def _topk(lg4, cap, expert_weights):
    bsz, n_chunks = lg4.shape[0], lg4.shape[1]
    if N_EXPERTS % bsz:
        cast_in, cast_out = (), tuple(w.astype(BF16) for w in expert_weights)
    else:
        cast_in, cast_out = tuple(expert_weights), ()
    per_step = N_EXPERTS // bsz if cast_in else 0
    w_specs = [pl.BlockSpec((per_step,) + w.shape[1:], lambda b: (b, 0, 0)) for w in cast_in]
    outs = pl.pallas_call(
        functools.partial(_topk_kernel, cap=cap, n_cast=len(cast_in)),
        out_shape=(
            jax.ShapeDtypeStruct((bsz, N_EXPERTS, cap), jnp.int32),
            jax.ShapeDtypeStruct((bsz, N_EXPERTS, cap), F32),
        ) + tuple(jax.ShapeDtypeStruct(w.shape, BF16) for w in cast_in),
        grid=(bsz,),
        in_specs=[pl.BlockSpec((1, n_chunks, N_EXPERTS, LANES), lambda b: (b, 0, 0, 0))] + w_specs,
        out_specs=(
            pl.BlockSpec((1, N_EXPERTS, cap), lambda b: (b, 0, 0)),
            pl.BlockSpec((1, N_EXPERTS, cap), lambda b: (b, 0, 0)),
        ) + tuple(w_specs),
        scratch_shapes=[
            pltpu.VMEM((n_chunks, N_EXPERTS, LANES), F32),
            pltpu.VMEM((n_chunks, N_EXPERTS, LANES), F32),
            pltpu.VMEM((n_chunks, N_EXPERTS, LANES), F32),
        ],
        compiler_params=pltpu.CompilerParams(
            dimension_semantics=("arbitrary",), vmem_limit_bytes=VMEM_LIMIT_BYTES),
        name="topk",
    )(lg4, *cast_in)
    return outs[0], outs[1], tuple(outs[2:]) + cast_out


def _moe_kernel(idx_ref, gate_ref, h2t_hbm, x1t_hbm, mod_ref, wg_ref, wu_ref, wd_ref, out_hbm,
                h2_s, acc_s, xs_s, ys_s, stage_s, sems, *, n, cap, sub):
    b = pl.program_id(0)
    e = pl.program_id(1)
    n_b = pl.num_programs(0)
    n_e = pl.num_programs(1)
    tok_rows = n * FEATURE_CHUNKS
    n_parts = cap // sub

    def load_h2(sample):
        return pltpu.make_async_copy(
            h2t_hbm.at[pl.ds(sample * tok_rows, tok_rows), :], h2_s, sems.at[0])

    def load_x1(sample):
        return pltpu.make_async_copy(
            x1t_hbm.at[pl.ds(sample * tok_rows, tok_rows), :], acc_s, sems.at[1])

    def load_x1_chunk(sample, c):
        rows = pl.ds(pl.multiple_of(c * (sub * FEATURE_CHUNKS), sub * FEATURE_CHUNKS),
                     sub * FEATURE_CHUNKS)
        return pltpu.make_async_copy(
            x1t_hbm.at[pl.ds(sample * tok_rows + c * (sub * FEATURE_CHUNKS),
                             sub * FEATURE_CHUNKS), :], acc_s.at[rows, :], sems.at[1])

    gates = gate_ref[0, 0]
    gl = gates.shape[1]
    gates = jnp.pad(gates, ((0, SUBLANES - gates.shape[0]), (0, LANES - gl)))
    gate_cols = gates.T
    gate2 = mod_ref[0, N_MOD - 1:N_MOD, :]

    def token_tile(first_row):
        return pl.ds(pl.multiple_of(first_row, FEATURE_CHUNKS), FEATURE_CHUNKS)

    def gather(part):
        for r in range(sub):
            t = idx_ref[0, 0, 0, part * sub + r]
            xs_s[part, r * FEATURE_CHUNKS:(r + 1) * FEATURE_CHUNKS, :] = h2_s[token_tile(t), :]

    def expert(part):
        x = jnp.concatenate(
            [xs_s[part, pl.ds(s, sub, stride=FEATURE_CHUNKS), :] for s in range(FEATURE_CHUNKS)],
            axis=1).astype(BF16)
        hid = (jax.nn.silu(_dot(x, wg_ref[0])) * _dot(x, wu_ref[0])).astype(BF16)
        y = _dot(hid, wd_ref[0]) * gate2
        first_col = part * sub // gl
        y = jnp.concatenate(
            [y[k * gl:(k + 1) * gl, :] * gate_cols[:gl, first_col + k:first_col + k + 1]
             for k in range(sub // gl)], axis=0)
        for s in range(FEATURE_CHUNKS):
            ys_s[part, pl.ds(s, sub, stride=FEATURE_CHUNKS), :] = y[:, s * LANES:(s + 1) * LANES]

    def scatter(part):
        for r0 in range(0, sub, SCATTER_UNROLL):
            updates = []
            for r in range(r0, r0 + SCATTER_UNROLL):
                dst = token_tile(idx_ref[0, 0, 0, part * sub + r])
                row = ys_s[part, r * FEATURE_CHUNKS:(r + 1) * FEATURE_CHUNKS, :]
                updates.append((dst, acc_s[dst, :] + row))
            for dst, val in updates:
                acc_s[dst, :] = val

    @pl.when((e == 0) & (b == 0))
    def _():
        load_h2(b).start()
        load_x1(b).start()

    @pl.when(e == 0)
    def _():
        load_h2(b).wait()

    for part in range(n_parts):
        gather(part)

    @pl.when((e == n_e - 1) & (b + 1 < n_b))
    def _():
        load_h2(b + 1).start()

    expert(0)

    @pl.when(e == 0)
    def _():
        load_x1(b).wait()

    for part in range(1, n_parts):
        expert(part)
    for part in range(n_parts):
        scatter(part)

    @pl.when(e == n_e - 1)
    def _():
        n_out = n // sub

        def store(c, slot):
            return pltpu.make_async_copy(
                stage_s.at[slot], out_hbm.at[pl.ds(b * n + c * sub, sub), :], sems.at[2 + slot])

        def out_body(c, carry):
            slot = c % 2

            @pl.when(c >= 2)
            def _():
                store(c - 2, slot).wait()
            row0 = pl.multiple_of(c * (sub * FEATURE_CHUNKS), sub * FEATURE_CHUNKS)
            for s in range(FEATURE_CHUNKS):
                stage_s[slot, :, s * LANES:(s + 1) * LANES] = (
                    acc_s[pl.ds(row0 + s, sub, stride=FEATURE_CHUNKS), :])
            store(c, slot).start()

            @pl.when(b + 1 < n_b)
            def _():
                load_x1_chunk(b + 1, c).start()
            return carry
        lax.fori_loop(0, n_out, out_body, 0)
        for c in range(n_out - 2, n_out):
            store(c, c % 2).wait()


def _moe(idx, gate, h2t, x1t, mod3, wg, wu, wd, bsz, n):
    cap = idx.shape[-1]
    sub = min(256, cap)
    gl = min(LANES, cap)
    assert n // sub >= 2 and cap % sub == 0 and sub % SCATTER_UNROLL == 0 and sub % gl == 0
    assert cap // gl <= SUBLANES
    tok_rows = n * FEATURE_CHUNKS
    weight_spec = pl.BlockSpec((1, D_MODEL, D_MODEL), lambda b, e: (e, 0, 0))
    smem_spec = pl.BlockSpec((1, 1, 1, cap), lambda b, e: (b, e, 0, 0),
                             memory_space=pltpu.SMEM)
    return pl.pallas_call(
        functools.partial(_moe_kernel, n=n, cap=cap, sub=sub),
        out_shape=jax.ShapeDtypeStruct((bsz * n, D_MODEL), F32),
        grid=(bsz, N_EXPERTS),
        in_specs=[
            smem_spec,
            pl.BlockSpec((1, 1, cap // gl, gl), lambda b, e: (b, e, 0, 0)),
            pl.BlockSpec(memory_space=pl.ANY),
            pl.BlockSpec(memory_space=pl.ANY),
            pl.BlockSpec((1, N_MOD, D_MODEL), lambda b, e: (b, 0, 0)),
            weight_spec,
            weight_spec,
            weight_spec,
        ],
        out_specs=pl.BlockSpec(memory_space=pl.ANY),
        scratch_shapes=[
            pltpu.VMEM((tok_rows, LANES), F32),
            pltpu.VMEM((tok_rows, LANES), F32),
            pltpu.VMEM((cap // sub, sub * FEATURE_CHUNKS, LANES), F32),
            pltpu.VMEM((cap // sub, sub * FEATURE_CHUNKS, LANES), F32),
            pltpu.VMEM((2, sub, D_MODEL), F32),
            pltpu.SemaphoreType.DMA((4,)),
        ],
        compiler_params=pltpu.CompilerParams(
            dimension_semantics=("arbitrary", "arbitrary"), vmem_limit_bytes=VMEM_LIMIT_BYTES),
        name="moe",
    )(idx[:, :, None, :], gate.reshape(bsz, N_EXPERTS, cap // gl, gl), h2t, x1t, mod3, wg, wu, wd)


def _rope_tables(n):
    t = jnp.arange(n)
    row = (t // GRID_W).astype(F32)
    col = (t % GRID_W).astype(F32)
    n_freq = HEAD_DIM // 4
    inv_freq = ROPE_BASE ** (-jnp.arange(n_freq, dtype=F32) / n_freq)
    ang_r = row[:, None] * inv_freq
    ang_c = col[:, None] * inv_freq
    cos = jnp.concatenate([jnp.cos(ang_r)] * 2 + [jnp.cos(ang_c)] * 2, axis=1)
    sin = jnp.concatenate([-jnp.sin(ang_r), jnp.sin(ang_r), -jnp.sin(ang_c), jnp.sin(ang_c)],
                          axis=1)
    return jnp.tile(cos, (1, 2)), jnp.tile(sin, (1, 2))


def _pack_gate_weights(w_r, b_r, w_i, b_i):
    n_p = LRU_WIDTH // LANES
    zeros = jnp.zeros((LRU_BLOCK_DIM, LRU_BLOCK_DIM), F32)

    def pair(w, p):
        top = jnp.concatenate([w[2 * p], zeros], axis=1)
        bot = jnp.concatenate([zeros, w[2 * p + 1]], axis=1)
        return jnp.concatenate([top, bot], axis=0)

    ws, bs = [], []
    for p in range(n_p):
        ws.append(jnp.concatenate(
            [pair(w_r[0], p), pair(w_i[0], p), pair(w_r[1], p), pair(w_i[1], p)], axis=1))
        sl = slice(p * LANES, (p + 1) * LANES)
        bs.append(jnp.concatenate([b_r[0, sl], b_i[0, sl], b_r[1, sl], b_i[1, sl]])[None, :])
    return jnp.stack(ws).astype(BF16), 0.5 * jnp.stack(bs)


def _layer(x, ctx, mod3, norm1_g, norm2_g, w_in, q_norm_g, k_norm_g, attn_sink, conv_w, conv_b,
           lru_w_r, lru_b_r, lru_w_i, lru_b_i, lru_lambda, w_out, w_router, w_gate, w_up, w_down):
    bsz, n, _ = x.shape
    n_ctx = ctx.shape[1]
    x2 = x.reshape(bsz * n, D_MODEL)
    c2 = ctx.reshape(bsz * n_ctx, D_MODEL)
    g1 = norm1_g[None, :]
    g2 = norm2_g[None, :]
    qg = jnp.tile(q_norm_g, 2)[None, :]
    kg = jnp.tile(k_norm_g, 2)[None, :]
    cos, sin = _rope_tables(n)
    w_in_b = w_in.astype(BF16)
    w_ctx_b = w_in_b[:, ATTN_WIDTH:ATTN_WIDTH + 2 * KV_WIDTH + LRU_WIDTH]

    q_s, k2, v2, xr2, gg2 = _inproj_latent(x2, mod3, g1, w_in_b, qg, kg, cos, sin, bsz, n)
    kc2, vc2, xrc2 = _inproj_ctx(c2, mod3, g1, w_ctx_b, kg, bsz, n_ctx)

    o_s = _attn(attn_sink, q_s, k2.reshape(bsz, n, KV_WIDTH), v2.reshape(bsz, n, KV_WIDTH),
                kc2.reshape(bsz, n_ctx, KV_WIDTH), vc2.reshape(bsz, n_ctx, KV_WIDTH))

    wg_lru, bg_lru = _pack_gate_weights(lru_w_r, lru_b_r, lru_w_i, lru_b_i)
    rg3 = _lru(xr2.reshape(bsz, n, LRU_WIDTH), xrc2.reshape(bsz, n_ctx, LRU_WIDTH),
               gg2.reshape(bsz, n, LRU_WIDTH), 0.5 * conv_w, 0.5 * conv_b[None, :], wg_lru,
               bg_lru, lru_lambda)

    w_att = (w_out[:ATTN_WIDTH].reshape(N_KV_HEADS, Q_PER_KV, HEAD_DIM, D_MODEL)
             .transpose(1, 0, 2, 3).reshape(ATTN_WIDTH, D_MODEL).astype(BF16))
    w_rnn = w_out[ATTN_WIDTH:].astype(BF16)
    x1t, h2t, lg4 = _outproj(o_s, rg3.reshape(bsz * n, LRU_WIDTH), x2, mod3, w_att, w_rnn, g2,
                             w_router.T.astype(BF16), bsz, n)

    cap = CAPACITY_FACTOR * n // N_EXPERTS
    idx, gate, (wg_b, wu_b, wd_b) = _topk(lg4, cap, (w_gate, w_up, w_down))
    out = _moe(idx, gate, h2t, x1t, mod3, wg_b, wu_b, wd_b, bsz, n)
    return out.reshape(bsz, n, D_MODEL)


def kernel(x, c, ctx, c_ctx, w_ada, b_ada, norm1_g, norm2_g, w_in, q_norm_g, k_norm_g, attn_sink, conv_w, conv_b, lru_w_r, lru_b_r, lru_w_i, lru_b_i, lru_lambda, w_out, w_router, w_gate, w_up, w_down):
    bsz = x.shape[0]
    depth = w_ada.shape[0]
    assert depth == 1, "context-stream update between layers is not implemented"
    pad_rows = -(bsz + 1) % SUBLANES
    cc = jnp.concatenate([c, jnp.zeros((pad_rows, D_MODEL), F32), c_ctx[None, :]], axis=0)
    for layer in range(depth):
        mod = _ada(cc, w_ada[layer], b_ada[layer][None, :])
        mod3 = mod.reshape(cc.shape[0], N_MOD, D_MODEL)
        x = _layer(x, ctx, mod3, norm1_g[layer], norm2_g[layer], w_in[layer], q_norm_g[layer],
                   k_norm_g[layer], attn_sink[layer], conv_w[layer], conv_b[layer],
                   lru_w_r[layer], lru_b_r[layer], lru_w_i[layer], lru_b_i[layer],
                   lru_lambda[layer], w_out[layer], w_router[layer], w_gate[layer],
                   w_up[layer], w_down[layer])
    return x
```

```python
import functools

import jax
import jax.numpy as jnp
from jax import lax
from jax.experimental import pallas as pl
from jax.experimental.pallas import tpu as pltpu

F32 = jnp.float32
BF16 = jnp.bfloat16

LANES = 128
SUBLANES = 8
VMEM_LIMIT_BYTES = 56 * 1024 * 1024

D_MODEL = 1024
HEAD_DIM = 64
N_Q_HEADS = 8
N_KV_HEADS = 2
Q_PER_KV = N_Q_HEADS // N_KV_HEADS
ATTN_WIDTH = N_Q_HEADS * HEAD_DIM
KV_WIDTH = N_KV_HEADS * HEAD_DIM
LRU_WIDTH = D_MODEL - ATTN_WIDTH
LRU_BLOCK_DIM = 64
IN_WIDTH = ATTN_WIDTH + 2 * KV_WIDTH + 2 * LRU_WIDTH
WINDOW = 128
BLOCK = 128
GRID_W = 64
ROPE_BASE = 10000.0
LRU_C = 8.0
N_EXPERTS = 16
CAPACITY_FACTOR = 2
EPS = 1e-6
NEG_INF = -1e30
LOG2_E = 1.4426950408889634
SCORE_SCALE = HEAD_DIM ** -0.5 * LOG2_E
N_MOD = 6
FEATURE_CHUNKS = D_MODEL // LANES
ATTN_BLOCKS_PER_STEP = 2
CAST_ROWS = 128
SCATTER_UNROLL = 16


def _dot(a, b):
    return jnp.dot(a, b, preferred_element_type=F32)


def _dot_nt(a, b):
    return lax.dot_general(a, b, (((1,), (1,)), ((), ())), preferred_element_type=F32)


def _dot_tn(a, b):
    return lax.dot_general(a, b, (((0,), (0,)), ((), ())), preferred_element_type=F32)


def _split_bf16(x, parts):
    out = []
    r = x
    for _ in range(parts):
        p = r.astype(BF16)
        out.append(p)
        r = r - p.astype(F32)
    return out


def _rmsnorm_rows(x, g):
    ms = jnp.mean(x * x, axis=-1, keepdims=True)
    return x * lax.rsqrt(ms + EPS) * g


def _ada_kernel(c_ref, w_ref, b_ref, o_ref):
    s = jax.nn.silu(c_ref[...])
    s_hi, s_lo = _split_bf16(s, 2)
    w_hi, w_lo = _split_bf16(w_ref[...], 2)
    o_ref[...] = _dot(s_hi, w_hi) + _dot(s_hi, w_lo) + _dot(s_lo, w_hi) + b_ref[...]


def _ada(cc, w_ada, b_ada):
    rows = cc.shape[0]
    width = w_ada.shape[1]
    return pl.pallas_call(
        _ada_kernel,
        out_shape=jax.ShapeDtypeStruct((rows, width), F32),
        grid=(width // D_MODEL,),
        in_specs=[
            pl.BlockSpec((rows, D_MODEL), lambda j: (0, 0)),
            pl.BlockSpec((D_MODEL, D_MODEL), lambda j: (0, j)),
            pl.BlockSpec((1, D_MODEL), lambda j: (0, j)),
        ],
        out_specs=pl.BlockSpec((rows, D_MODEL), lambda j: (0, j)),
        compiler_params=pltpu.CompilerParams(
            dimension_semantics=("arbitrary",), vmem_limit_bytes=VMEM_LIMIT_BYTES),
        name="ada",
    )(cc, w_ada, b_ada)


def _head_norm(p, g):
    lane = lax.broadcasted_iota(jnp.int32, p.shape, 1)
    low = lane < HEAD_DIM
    sq = p * p
    s_low = jnp.sum(jnp.where(low, sq, 0.0), axis=-1, keepdims=True)
    s_high = jnp.sum(jnp.where(low, 0.0, sq), axis=-1, keepdims=True)
    ms = jnp.where(low, s_low, s_high) * (1.0 / HEAD_DIM)
    return p * lax.rsqrt(ms + EPS) * g


def _rope(p, cos, sin_signed):
    lane = lax.broadcasted_iota(jnp.int32, p.shape, 1)
    partner = jnp.where((lane & 16) == 0,
                        pltpu.roll(p, LANES - 16, 1), pltpu.roll(p, 16, 1))
    return p * cos + partner * sin_signed


def _inproj_latent_kernel(x_ref, mod_ref, g1_ref, w_ref, qg_ref, kg_ref, cos_ref, sin_ref,
                          q_ref, k_ref, v_ref, xr_ref, gg_ref):
    x = x_ref[...]
    h = _rmsnorm_rows(x, g1_ref[...]) * (1.0 + mod_ref[0, 1:2, :]) + mod_ref[0, 0:1, :]
    y = _dot(h.astype(BF16), w_ref[...])
    cos = cos_ref[...]
    sin = sin_ref[...]
    pieces = []
    for j in range(ATTN_WIDTH // LANES):
        p = _head_norm(y[:, j * LANES:(j + 1) * LANES], qg_ref[...])
        pieces.append(_rope(p, cos, sin))
    lane = lax.broadcasted_iota(jnp.int32, pieces[0].shape, 1)
    low = lane < HEAD_DIM
    n_blocks = y.shape[0] // BLOCK
    for g in range(Q_PER_KV):
        src0 = pieces[g // 2]
        src1 = pieces[2 + g // 2]
        if g % 2 == 0:
            out = jnp.where(low, src0, pltpu.roll(src1, HEAD_DIM, 1))
        else:
            out = jnp.where(low, pltpu.roll(src0, HEAD_DIM, 1), src1)
        out = (out * SCORE_SCALE).astype(BF16)
        for j in range(n_blocks):
            q_ref[0, j, g * BLOCK:(g + 1) * BLOCK, :] = out[j * BLOCK:(j + 1) * BLOCK, :]
    k = _head_norm(y[:, ATTN_WIDTH:ATTN_WIDTH + KV_WIDTH], kg_ref[...])
    k_ref[...] = _rope(k, cos, sin).astype(BF16)
    o = ATTN_WIDTH + KV_WIDTH
    v_ref[...] = y[:, o:o + KV_WIDTH].astype(BF16)
    o += KV_WIDTH
    xr_ref[...] = y[:, o:o + LRU_WIDTH]
    o += LRU_WIDTH
    gg_ref[...] = jax.nn.gelu(y[:, o:o + LRU_WIDTH])


def _inproj_ctx_kernel(x_ref, mod_ref, g1_ref, w_ref, kg_ref, k_ref, v_ref, xr_ref):
    x = x_ref[...]
    h = _rmsnorm_rows(x, g1_ref[...]) * (1.0 + mod_ref[0, 1:2, :]) + mod_ref[0, 0:1, :]
    y = _dot(h.astype(BF16), w_ref[...])
    k_ref[...] = _head_norm(y[:, 0:KV_WIDTH], kg_ref[...]).astype(BF16)
    v_ref[...] = y[:, KV_WIDTH:2 * KV_WIDTH].astype(BF16)
    xr_ref[...] = y[:, 2 * KV_WIDTH:2 * KV_WIDTH + LRU_WIDTH]


OUTPROJ_ROW_TILE = 1024


def _row_tile(n):
    return min(512, n)


def _inproj_latent(x2, mod3, g1, w_in, qg, kg, cos, sin, bsz, n):
    tm = _row_tile(n)
    tps = n // tm
    rows = bsz * n
    const = lambda i: (0, 0)
    return pl.pallas_call(
        _inproj_latent_kernel,
        out_shape=(
            jax.ShapeDtypeStruct((bsz, n // BLOCK, Q_PER_KV * BLOCK, LANES), BF16),
            jax.ShapeDtypeStruct((rows, KV_WIDTH), BF16),
            jax.ShapeDtypeStruct((rows, KV_WIDTH), BF16),
            jax.ShapeDtypeStruct((rows, LRU_WIDTH), F32),
            jax.ShapeDtypeStruct((rows, LRU_WIDTH), F32),
        ),
        grid=(rows // tm,),
        in_specs=[
            pl.BlockSpec((tm, D_MODEL), lambda i: (i, 0)),
            pl.BlockSpec((1, N_MOD, D_MODEL), lambda i: (i // tps, 0, 0)),
            pl.BlockSpec((1, D_MODEL), const),
            pl.BlockSpec((D_MODEL, IN_WIDTH), const),
            pl.BlockSpec((1, LANES), const),
            pl.BlockSpec((1, LANES), const),
            pl.BlockSpec((tm, LANES), lambda i: (i % tps, 0)),
            pl.BlockSpec((tm, LANES), lambda i: (i % tps, 0)),
        ],
        out_specs=(
            pl.BlockSpec((1, tm // BLOCK, Q_PER_KV * BLOCK, LANES),
                         lambda i: (i // tps, i % tps, 0, 0)),
            pl.BlockSpec((tm, KV_WIDTH), lambda i: (i, 0)),
            pl.BlockSpec((tm, KV_WIDTH), lambda i: (i, 0)),
            pl.BlockSpec((tm, LRU_WIDTH), lambda i: (i, 0)),
            pl.BlockSpec((tm, LRU_WIDTH), lambda i: (i, 0)),
        ),
        compiler_params=pltpu.CompilerParams(
            dimension_semantics=("arbitrary",), vmem_limit_bytes=VMEM_LIMIT_BYTES),
        name="inproj_latent",
    )(x2, mod3, g1, w_in, qg, kg, cos, sin)


def _inproj_ctx(c2, mod3, g1, w_ctx, kg, bsz, n_ctx):
    tm = _row_tile(n_ctx)
    rows = bsz * n_ctx
    ctx_row = mod3.shape[0] - 1
    width = w_ctx.shape[1]
    const = lambda i: (0, 0)
    return pl.pallas_call(
        _inproj_ctx_kernel,
        out_shape=(
            jax.ShapeDtypeStruct((rows, KV_WIDTH), BF16),
            jax.ShapeDtypeStruct((rows, KV_WIDTH), BF16),
            jax.ShapeDtypeStruct((rows, LRU_WIDTH), F32),
        ),
        grid=(rows // tm,),
        in_specs=[
            pl.BlockSpec((tm, D_MODEL), lambda i: (i, 0)),
            pl.BlockSpec((1, N_MOD, D_MODEL), lambda i: (ctx_row, 0, 0)),
            pl.BlockSpec((1, D_MODEL), const),
            pl.BlockSpec((D_MODEL, width), const),
            pl.BlockSpec((1, LANES), const),
        ],
        out_specs=(
            pl.BlockSpec((tm, KV_WIDTH), lambda i: (i, 0)),
            pl.BlockSpec((tm, KV_WIDTH), lambda i: (i, 0)),
            pl.BlockSpec((tm, LRU_WIDTH), lambda i: (i, 0)),
        ),
        compiler_params=pltpu.CompilerParams(
            dimension_semantics=("arbitrary",), vmem_limit_bytes=VMEM_LIMIT_BYTES),
        name="inproj_ctx",
    )(c2, mod3, g1, w_ctx, kg)


def _attn_kernel(sink_ref, q_ref, k_ref, v_ref, kc_ref, vc_ref, *refs, n, nb):
    bias_refs = refs[:ATTN_BLOCKS_PER_STEP]
    o_ref, s0_s, s1_s = refs[ATTN_BLOCKS_PER_STEP:]
    t = pl.program_id(0)
    last = pl.num_programs(0) - 2
    j_scored = (jnp.minimum(t, last) * ATTN_BLOCKS_PER_STEP) % nb
    j_finished = (jnp.maximum(t - 1, 0) * ATTN_BLOCKS_PER_STEP) % nb
    span = 3 * BLOCK

    def window_start(i):
        return pl.multiple_of(jnp.clip((i - 1) * BLOCK, 0, n - span), BLOCK)

    def head_only(t, h):
        lane = lax.broadcasted_iota(jnp.int32, t.shape, 1)
        keep = (lane < HEAD_DIM) if h == 0 else (lane >= HEAD_DIM)
        return jnp.where(keep, t, jnp.zeros_like(t))

    group = range(ATTN_BLOCKS_PER_STEP)
    stages = functools.partial(_attn_stages, sink_ref, q_ref, k_ref, v_ref, kc_ref, vc_ref,
                               bias_refs, o_ref,
                               starts_scored=[window_start(j_scored + u) for u in group],
                               starts_finished=[window_start(j_finished + u) for u in group],
                               head_only=head_only)

    @pl.when(t == 0)
    def _():
        s1_s[...] = jnp.zeros_like(s1_s)

    @pl.when(t % 2 == 0)
    def _():
        stages(s0_s, s1_s)

    @pl.when(t % 2 == 1)
    def _():
        stages(s1_s, s0_s)


def _attn_stages(sink_ref, q_ref, k_ref, v_ref, kc_ref, vc_ref, bias_refs, o_ref, s_new, s_old,
                 starts_scored, starts_finished, head_only):
    for u in range(ATTN_BLOCKS_PER_STEP):
        _attn_block_stages(sink_ref, q_ref.at[0, u], k_ref, v_ref, kc_ref, vc_ref, bias_refs[u],
                           o_ref.at[0, u], s_new.at[u], s_old.at[u], starts_scored[u],
                           starts_finished[u], head_only)


def _attn_block_stages(sink_ref, q_ref, k_ref, v_ref, kc_ref, vc_ref, bias_ref, o_ref, s_new,
                       s_old, start_scored, start_finished, head_only):
    span = 3 * BLOCK

    def with_ones(t, lane_index):
        lane = lax.broadcasted_iota(jnp.int32, t.shape, 1)
        return jnp.where(lane == lane_index, jnp.ones_like(t), t)

    q = q_ref[...]
    kw = k_ref[0, pl.ds(start_scored, span), :]
    kc = kc_ref[0]
    bias = bias_ref[0]
    for h in range(N_KV_HEADS):
        s_new[h, :, 0:span] = _dot_nt(q, head_only(kw, h)) + bias
        s_new[h, :, span:] = _dot_nt(q, head_only(kc, h))

    vw = v_ref[0, pl.ds(start_finished, span), :]
    vc = vc_ref[0]
    rows = q.shape[0]
    group = lax.broadcasted_iota(jnp.int32, (rows, 1), 0) // BLOCK
    acc = jnp.zeros((rows, LANES), F32)
    for h in range(N_KV_HEADS):
        s_loc = s_old[h, :, 0:span]
        s_ctx = s_old[h, :, span:]
        sink = jnp.zeros((rows, 1), F32)
        for g in range(Q_PER_KV):
            sink = jnp.where(group == g, sink_ref[h * Q_PER_KV + g] * LOG2_E, sink)
        m = jnp.maximum(jnp.maximum(jnp.max(s_loc, axis=-1, keepdims=True),
                                    jnp.max(s_ctx, axis=-1, keepdims=True)), sink)
        p_loc = jnp.exp2(s_loc - m).astype(BF16)
        p_ctx = jnp.exp2(s_ctx - m).astype(BF16)
        sum_lane = HEAD_DIM if h == 0 else 0
        o = (_dot(p_loc, with_ones(head_only(vw, h), sum_lane))
             + _dot(p_ctx, with_ones(head_only(vc, h), sum_lane)))
        denom = o[:, sum_lane:sum_lane + 1] + jnp.exp2(sink - m)
        lane = lax.broadcasted_iota(jnp.int32, o.shape, 1)
        own = (lane < HEAD_DIM) if h == 0 else (lane >= HEAD_DIM)
        acc = acc + jnp.where(own, o, 0.0) / denom
    o_ref[...] = acc.astype(BF16)


def _band_bias(n):
    span = 3 * BLOCK
    r = jnp.arange(Q_PER_KV * BLOCK)[:, None] % BLOCK
    c = jnp.arange(span)[None, :]
    offsets = jnp.arange(span // BLOCK)[:, None, None] * BLOCK
    valid = jnp.abs(r - c + offsets) <= WINDOW
    return jnp.where(valid, 0.0, NEG_INF).astype(F32)


def _attn(sink, q_s, k3, v3, kc3, vc3):
    bsz, nb = q_s.shape[0], q_s.shape[1]
    n = k3.shape[1]
    n_ctx = kc3.shape[1]
    per = ATTN_BLOCKS_PER_STEP
    assert nb >= 3 and nb % per == 0
    bias = _band_bias(n)
    n_groups = bsz * nb // per
    grouped = (n_groups, per) + q_s.shape[2:]

    def scored(t):
        return jnp.minimum(t, n_groups - 1)

    def finished(t):
        return jnp.maximum(t - 1, 0)

    def sample(g):
        return g * per // nb

    def placement(u):
        def index(t):
            i = (scored(t) * per + u) % nb
            return (jnp.minimum(i, 1) + (i == nb - 1).astype(jnp.int32), 0, 0)
        return index

    block = (1, per, Q_PER_KV * BLOCK, LANES)
    scores = pltpu.VMEM((per, N_KV_HEADS, Q_PER_KV * BLOCK, 3 * BLOCK + n_ctx), F32)

    return pl.pallas_call(
        functools.partial(_attn_kernel, n=n, nb=nb),
        out_shape=jax.ShapeDtypeStruct(grouped, BF16),
        grid=(n_groups + 1,),
        in_specs=[
            pl.BlockSpec(memory_space=pltpu.SMEM),
            pl.BlockSpec(block, lambda t: (scored(t), 0, 0, 0)),
            pl.BlockSpec((1, n, KV_WIDTH), lambda t: (sample(scored(t)), 0, 0)),
            pl.BlockSpec((1, n, KV_WIDTH), lambda t: (sample(finished(t)), 0, 0)),
            pl.BlockSpec((1, n_ctx, KV_WIDTH), lambda t: (sample(scored(t)), 0, 0)),
            pl.BlockSpec((1, n_ctx, KV_WIDTH), lambda t: (sample(finished(t)), 0, 0)),
        ] + [pl.BlockSpec((1, Q_PER_KV * BLOCK, 3 * BLOCK), placement(u)) for u in range(per)],
        out_specs=pl.BlockSpec(block, lambda t: (finished(t), 0, 0, 0)),
        scratch_shapes=[scores, scores],
        compiler_params=pltpu.CompilerParams(
            dimension_semantics=("arbitrary",), vmem_limit_bytes=VMEM_LIMIT_BYTES),
        name="attn",
    )(sink, q_s.reshape(grouped), k3, v3, kc3, vc3, *([bias] * per)).reshape(q_s.shape)


def _softplus(z):
    return jnp.maximum(z, 0.0) + jnp.log1p(jnp.exp(-jnp.abs(z)))


LRU_MAIN_SEG = 252


def _lru_plan(length):
    main, rest = divmod(length // SUBLANES, LRU_MAIN_SEG)
    if rest == 0:
        tails = ()
    elif rest % SUBLANES == 0 and rest > SUBLANES:
        tails = (rest - 5, 5)
    else:
        tails = (rest,)
    assert length % SUBLANES == 0 and all(t >= 2 for t in tails)
    return main, tails


def _lru_kernel(xr_ref, xrc_ref, gg_ref, cw_ref, cb_ref, wg_ref, bg_ref, lam_ref, o_ref,
                xc_s, xcc_s, hf_s, hb_s, *, n, n_ctx):
    cw = cw_ref[...]
    cb = cb_ref[...]
    sub = lax.broadcasted_iota(jnp.int32, (SUBLANES, LANES), 0)
    main_rows = SUBLANES * LRU_MAIN_SEG
    n_main, tails = _lru_plan(n)
    c_main, c_tails = _lru_plan(n_ctx)

    def static_tiles(n_main_tiles, tail_segs, with_main):
        tiles = [(k * main_rows, LRU_MAIN_SEG) for k in range(n_main_tiles)] if with_main else []
        t0 = n_main_tiles * main_rows
        for seg in tail_segs:
            tiles.append((t0, seg))
            t0 += SUBLANES * seg
        return tiles

    def aligned(v):
        return v if isinstance(v, int) else pl.multiple_of(v, SUBLANES)

    def conv_tile(src_ref, dst_ref, t0, seg, length):
        x = [src_ref[0, pl.ds(t0 + j, SUBLANES, stride=seg), :] for j in range(seg)]
        lo = aligned(jnp.maximum(t0 - SUBLANES, 0))
        hi = aligned(jnp.minimum(t0 + SUBLANES * seg, length - SUBLANES))
        prev = jnp.where(t0 > 0, src_ref[0, pl.ds(lo, SUBLANES), :], 0.0)
        nxt = jnp.where(t0 + SUBLANES * seg < length, src_ref[0, pl.ds(hi, SUBLANES), :], 0.0)

        def from_prev_segment(v, row):
            return jnp.where(sub == 0, row, pltpu.roll(v, 1, 0))

        def from_next_segment(v, row):
            return jnp.where(sub == SUBLANES - 1, row, pltpu.roll(v, SUBLANES - 1, 0))

        m1 = [from_prev_segment(x[seg - 1], prev[7:8])] + x[:seg - 1]
        m2 = [from_prev_segment(x[seg - 2], prev[6:7])] + m1[:seg - 1]
        p1 = x[1:] + [from_next_segment(x[0], nxt[0:1])]
        for j in range(seg):
            y = cb + cw[0:1] * m2[j]
            y = y + cw[1:2] * m1[j]
            y = y + cw[2:3] * x[j]
            y = y + cw[3:4] * p1[j]
            dst_ref[pl.ds(t0 + j * SUBLANES, SUBLANES), :] = y

    half_rate = [-0.5 * LRU_C * _softplus(-lam_ref[d:d + 1, :]) for d in range(2)]

    def coeffs(xc, d):
        w = wg_ref[0, :, 2 * LANES * d:2 * LANES * (d + 1)]
        g = _dot(xc.astype(BF16), w) + bg_ref[0, :, 2 * LANES * d:2 * LANES * (d + 1)]
        t_r = jnp.tanh(g[:, :LANES])
        t_i = jnp.tanh(g[:, LANES:])
        log_a = t_r * half_rate[d] + half_rate[d]
        a = jnp.exp(log_a)
        mult = jnp.sqrt(-jnp.tanh(log_a) * (a * a + 1.0))
        return a, mult * (t_i + 1.0) * xc

    def scan_tile(a, b, h_in, seg, reverse):
        order = list(range(seg - 1, -1, -1) if reverse else range(seg))
        local = [None] * seg
        prod = [None] * seg
        h = p = None
        for j in order:
            aj = a[j * SUBLANES:(j + 1) * SUBLANES, :]
            bj = b[j * SUBLANES:(j + 1) * SUBLANES, :]
            h = bj if h is None else aj * h + bj
            p = aj if p is None else aj * p
            local[j], prod[j] = h, p
        end = order[-1]
        carry = jnp.zeros((SUBLANES, LANES), F32)
        c = h_in
        for s in (range(SUBLANES - 1, -1, -1) if reverse else range(SUBLANES)):
            carry = jnp.where(sub == s, c, carry)
            c = prod[end][s:s + 1, :] * c + local[end][s:s + 1, :]
        return [local[j] + prod[j] * carry for j in range(seg)], c

    def run_tile(xc_ref, t0, seg, d, h, emit=None):
        a, b = coeffs(xc_ref[pl.ds(t0, SUBLANES * seg), :], d)
        states, h = scan_tile(a, b, h, seg, d == 1)
        if emit is not None:
            emit(t0, seg, states)
        return h

    def keep(dst_ref):
        def emit(t0, seg, states):
            for j in range(seg):
                dst_ref[pl.ds(t0 + j * SUBLANES, SUBLANES), :] = states[j]
        return emit

    def write_output(other_ref):
        def emit(t0, seg, states):
            for j in range(seg):
                rows = pl.ds(t0 + j, SUBLANES, stride=seg)
                other = other_ref[pl.ds(t0 + j * SUBLANES, SUBLANES), :]
                o_ref[0, rows, :] = (other + states[j]) * gg_ref[0, rows, :]
        return emit

    def main_t0(k):
        return pl.multiple_of(k * main_rows, SUBLANES)

    for t0, seg in static_tiles(c_main, c_tails, True):
        conv_tile(xrc_ref, xcc_s, t0, seg, n_ctx)

    def conv_body(k, carry):
        conv_tile(xr_ref, xc_s, main_t0(k), LRU_MAIN_SEG, n)
        return carry
    lax.fori_loop(0, n_main, conv_body, 0)
    for t0, seg in static_tiles(n_main, tails, False):
        conv_tile(xr_ref, xc_s, t0, seg, n)

    hf = jnp.zeros((1, LANES), F32)
    hb = jnp.zeros((1, LANES), F32)
    for t0, seg in static_tiles(c_main, c_tails, True):
        hf = run_tile(xcc_s, t0, seg, 0, hf)
    for t0, seg in reversed(static_tiles(c_main, c_tails, True)):
        hb = run_tile(xcc_s, t0, seg, 1, hb)
    for t0, seg in reversed(static_tiles(n_main, tails, False)):
        hb = run_tile(xc_s, t0, seg, 1, hb, keep(hb_s))

    def pair_body(first_visit):
        def body(k, carry):
            hf, hb = carry
            fwd_emit = keep(hf_s) if first_visit else write_output(hb_s)
            rev_emit = keep(hb_s) if first_visit else write_output(hf_s)
            hf = run_tile(xc_s, main_t0(k), LRU_MAIN_SEG, 0, hf, fwd_emit)
            hb = run_tile(xc_s, main_t0(n_main - 1 - k), LRU_MAIN_SEG, 1, hb, rev_emit)
            return hf, hb
        return body

    half = n_main // 2
    hf, hb = lax.fori_loop(0, half, pair_body(True), (hf, hb))
    if n_main % 2:
        hf = run_tile(xc_s, half * main_rows, LRU_MAIN_SEG, 0, hf, keep(hf_s))
        hb = run_tile(xc_s, half * main_rows, LRU_MAIN_SEG, 1, hb, write_output(hf_s))
    hf, hb = lax.fori_loop(n_main - half, n_main, pair_body(False), (hf, hb))
    for t0, seg in static_tiles(n_main, tails, False):
        hf = run_tile(xc_s, t0, seg, 0, hf, write_output(hb_s))


def _lru(xr3, xrc3, gg3, conv_w, conv_b, wg, bg, lam):
    bsz, n, _ = xr3.shape
    n_ctx = xrc3.shape[1]
    n_p = LRU_WIDTH // LANES
    return pl.pallas_call(
        functools.partial(_lru_kernel, n=n, n_ctx=n_ctx),
        out_shape=jax.ShapeDtypeStruct((bsz, n, LRU_WIDTH), F32),
        grid=(bsz, n_p),
        in_specs=[
            pl.BlockSpec((1, n, LANES), lambda b, p: (b, 0, p)),
            pl.BlockSpec((1, n_ctx, LANES), lambda b, p: (b, 0, p)),
            pl.BlockSpec((1, n, LANES), lambda b, p: (b, 0, p)),
            pl.BlockSpec((conv_w.shape[0], LANES), lambda b, p: (0, p)),
            pl.BlockSpec((1, LANES), lambda b, p: (0, p)),
            pl.BlockSpec((1, LANES, 4 * LANES), lambda b, p: (p, 0, 0)),
            pl.BlockSpec((1, 1, 4 * LANES), lambda b, p: (p, 0, 0)),
            pl.BlockSpec((2, LANES), lambda b, p: (0, p)),
        ],
        out_specs=pl.BlockSpec((1, n, LANES), lambda b, p: (b, 0, p)),
        scratch_shapes=[
            pltpu.VMEM((n, LANES), F32),
            pltpu.VMEM((n_ctx, LANES), F32),
            pltpu.VMEM((n, LANES), F32),
            pltpu.VMEM((n, LANES), F32),
        ],
        compiler_params=pltpu.CompilerParams(
            dimension_semantics=("arbitrary", "arbitrary"), vmem_limit_bytes=VMEM_LIMIT_BYTES),
        name="lru",
    )(xr3, xrc3, gg3, conv_w, conv_b, wg, bg, lam)


def _outproj_kernel(o_ref, rg_ref, x_ref, mod_ref, watt_ref, wrnn_ref, g2_ref, wr_ref,
                    x1t_ref, h2t_ref, lg_ref):
    tm = x_ref.shape[0]
    n_blocks = tm // BLOCK
    att = jnp.concatenate(
        [jnp.concatenate([o_ref[0, j, g * BLOCK:(g + 1) * BLOCK, :] for g in range(Q_PER_KV)],
                         axis=1) for j in range(n_blocks)], axis=0)
    y = _dot(att, watt_ref[...]) + _dot(rg_ref[...].astype(BF16), wrnn_ref[...])
    x1 = x_ref[...] + mod_ref[0, 2:3, :] * y
    h2 = _rmsnorm_rows(x1, g2_ref[...]) * (1.0 + mod_ref[0, 4:5, :]) + mod_ref[0, 3:4, :]
    for s in range(FEATURE_CHUNKS):
        rows = pl.ds(s, tm, stride=FEATURE_CHUNKS)
        x1t_ref[rows, :] = x1[:, s * LANES:(s + 1) * LANES]
        h2t_ref[rows, :] = h2[:, s * LANES:(s + 1) * LANES]
    lt = _dot_nt(wr_ref[...], h2.astype(BF16))
    for j in range(n_blocks):
        lg_ref[0, j] = lt[:, j * LANES:(j + 1) * LANES]


def _outproj(o_s, rg2, x2, mod3, w_att, w_rnn, g2, wr_t, bsz, n):
    tm = min(OUTPROJ_ROW_TILE, n)
    tps = n // tm
    rows = bsz * n
    const = lambda i: (0, 0)
    return pl.pallas_call(
        _outproj_kernel,
        out_shape=(
            jax.ShapeDtypeStruct((rows * FEATURE_CHUNKS, LANES), F32),
            jax.ShapeDtypeStruct((rows * FEATURE_CHUNKS, LANES), F32),
            jax.ShapeDtypeStruct((bsz, n // LANES, N_EXPERTS, LANES), F32),
        ),
        grid=(rows // tm,),
        in_specs=[
            pl.BlockSpec((1, tm // BLOCK, Q_PER_KV * BLOCK, LANES),
                         lambda i: (i // tps, i % tps, 0, 0)),
            pl.BlockSpec((tm, LRU_WIDTH), lambda i: (i, 0)),
            pl.BlockSpec((tm, D_MODEL), lambda i: (i, 0)),
            pl.BlockSpec((1, N_MOD, D_MODEL), lambda i: (i // tps, 0, 0)),
            pl.BlockSpec((ATTN_WIDTH, D_MODEL), const),
            pl.BlockSpec((LRU_WIDTH, D_MODEL), const),
            pl.BlockSpec((1, D_MODEL), const),
            pl.BlockSpec((N_EXPERTS, D_MODEL), const),
        ],
        out_specs=(
            pl.BlockSpec((tm * FEATURE_CHUNKS, LANES), lambda i: (i, 0)),
            pl.BlockSpec((tm * FEATURE_CHUNKS, LANES), lambda i: (i, 0)),
            pl.BlockSpec((1, tm // LANES, N_EXPERTS, LANES), lambda i: (i // tps, i % tps, 0, 0)),
        ),
        compiler_params=pltpu.CompilerParams(
            dimension_semantics=("arbitrary",), vmem_limit_bytes=VMEM_LIMIT_BYTES),
        name="outproj",
    )(o_s, rg2, x2, mod3, w_att, w_rnn, g2, wr_t)


def _topk_kernel(lg_ref, *refs, cap, n_cast):
    w_refs = refs[:n_cast]
    idx_ref, gate_ref = refs[n_cast:n_cast + 2]
    wb_refs = refs[n_cast + 2:2 * n_cast + 2]
    aff_s, cum_s, before_s = refs[2 * n_cast + 2:]
    for w_ref, wb_ref in zip(w_refs, wb_refs):
        def cast_rows(i, carry):
            rows = pl.ds(pl.multiple_of(i * CAST_ROWS, CAST_ROWS), CAST_ROWS)
            for k in range(w_ref.shape[0]):
                wb_ref[k, rows, :] = w_ref[k, rows, :].astype(BF16)
            return carry
        lax.fori_loop(0, w_ref.shape[1] // CAST_ROWS, cast_rows, 0)

    lg = lg_ref[0]
    n_chunks = lg.shape[0]
    m = jnp.max(lg, axis=1, keepdims=True)
    ex = jnp.exp(lg - m)
    aff = ex / jnp.sum(ex, axis=1, keepdims=True)
    aff_s[...] = aff

    def count(mask):
        c = jnp.sum(jnp.where(mask, 1.0, 0.0), axis=0, keepdims=True)
        return jnp.sum(c, axis=2, keepdims=True)

    def bit_body(it, thr):
        cand = thr | jnp.left_shift(jnp.int32(1), 30 - it)
        return jnp.where(count(aff >= pltpu.bitcast(cand, F32)) >= cap, cand, thr)
    thr = lax.fori_loop(0, 31, bit_body, jnp.zeros((1, N_EXPERTS, 1), jnp.int32))
    thr = pltpu.bitcast(thr, F32)

    tri = (lax.broadcasted_iota(jnp.int32, (LANES, LANES), 0)
           <= lax.broadcasted_iota(jnp.int32, (LANES, LANES), 1))
    tri = jnp.where(tri, 1.0, 0.0).astype(BF16)

    def chunk_cumsum(mask):
        flat = jnp.where(mask, 1.0, 0.0).astype(BF16).reshape(n_chunks * N_EXPERTS, LANES)
        inc = _dot(flat, tri).reshape(n_chunks, N_EXPERTS, LANES)
        run = jnp.zeros((N_EXPERTS, 1), F32)
        before = []
        for c in range(n_chunks):
            before.append(run)
            run = run + inc[c][:, LANES - 1:LANES]
        return inc, jnp.stack(before, axis=0)

    above = aff > thr
    tied = aff == thr
    need = cap - count(above)
    tie_inc, tie_before = chunk_cumsum(tied)
    tie_rank = tie_before + tie_inc - jnp.where(tied, 1.0, 0.0)
    sel = above | (tied & (tie_rank < need))
    sel_inc, sel_before = chunk_cumsum(sel)
    cum_s[...] = sel_inc
    before_s[...] = jnp.broadcast_to(sel_before, sel_inc.shape)

    slot = lax.broadcasted_iota(jnp.int32, (1, cap), 1).astype(F32)
    chunk_id = lax.broadcasted_iota(jnp.int32, (n_chunks, cap), 0).astype(F32)
    lane_id = lax.broadcasted_iota(jnp.int32, (LANES, cap), 0).astype(F32)
    for e in range(N_EXPERTS):
        cin = cum_s[:, e, :]
        cc_exc = before_s[:, e, :][:, 0:1]
        cc_inc = cc_exc + cin[:, LANES - 1:LANES]
        kc = jnp.sum(jnp.where(cc_inc <= slot, 1.0, 0.0), axis=0, keepdims=True)
        onehot = chunk_id == kc
        onehot_b = jnp.where(onehot, 1.0, 0.0).astype(BF16)
        counts_t = _dot_tn(cin.astype(BF16), onehot_b)
        local = slot - jnp.sum(jnp.where(onehot, cc_exc, 0.0), axis=0, keepdims=True)
        il = jnp.sum(jnp.where(counts_t <= local, 1.0, 0.0), axis=0, keepdims=True)
        idx_ref[0, e:e + 1, :] = ((kc * LANES + il) * FEATURE_CHUNKS).astype(jnp.int32)
        aff_t = jnp.zeros((LANES, cap), F32)
        for part in _split_bf16(aff_s[:, e, :], 3):
            aff_t = aff_t + _dot_tn(part, onehot_b)
        gate_ref[0, e:e + 1, :] = jnp.sum(jnp.where(lane_id == il, aff_t, 0.0),
                                          axis=0, keepdims=True)


def _topk(lg4, cap, expert_weights):
    bsz, n_chunks = lg4.shape[0], lg4.shape[1]
    if N_EXPERTS % bsz:
        cast_in, cast_out = (), tuple(w.astype(BF16) for w in expert_weights)
    else:
        cast_in, cast_out = tuple(expert_weights), ()
    per_step = N_EXPERTS // bsz if cast_in else 0
    w_specs = [pl.BlockSpec((per_step,) + w.shape[1:], lambda b: (b, 0, 0)) for w in cast_in]
    outs = pl.pallas_call(
        functools.partial(_topk_kernel, cap=cap, n_cast=len(cast_in)),
        out_shape=(
            jax.ShapeDtypeStruct((bsz, N_EXPERTS, cap), jnp.int32),
            jax.ShapeDtypeStruct((bsz, N_EXPERTS, cap), F32),
        ) + tuple(jax.ShapeDtypeStruct(w.shape, BF16) for w in cast_in),
        grid=(bsz,),
        in_specs=[pl.BlockSpec((1, n_chunks, N_EXPERTS, LANES), lambda b: (b, 0, 0, 0))] + w_specs,
        out_specs=(
            pl.BlockSpec((1, N_EXPERTS, cap), lambda b: (b, 0, 0)),
            pl.BlockSpec((1, N_EXPERTS, cap), lambda b: (b, 0, 0)),
        ) + tuple(w_specs),
        scratch_shapes=[
            pltpu.VMEM((n_chunks, N_EXPERTS, LANES), F32),
            pltpu.VMEM((n_chunks, N_EXPERTS, LANES), F32),
            pltpu.VMEM((n_chunks, N_EXPERTS, LANES), F32),
        ],
        compiler_params=pltpu.CompilerParams(
            dimension_semantics=("arbitrary",), vmem_limit_bytes=VMEM_LIMIT_BYTES),
        name="topk",
    )(lg4, *cast_in)
    return outs[0], outs[1], tuple(outs[2:]) + cast_out


def _moe_kernel(idx_ref, gate_ref, h2t_hbm, x1t_hbm, mod_ref, wg_ref, wu_ref, wd_ref, out_hbm,
                h2_s, acc_s, xs_s, ys_s, stage_s, sems, *, n, cap, sub):
    b = pl.program_id(0)
    e = pl.program_id(1)
    n_b = pl.num_programs(0)
    n_e = pl.num_programs(1)
    tok_rows = n * FEATURE_CHUNKS
    n_parts = cap // sub

    def load_h2(sample):
        return pltpu.make_async_copy(
            h2t_hbm.at[pl.ds(sample * tok_rows, tok_rows), :], h2_s, sems.at[0])

    def load_x1(sample):
        return pltpu.make_async_copy(
            x1t_hbm.at[pl.ds(sample * tok_rows, tok_rows), :], acc_s, sems.at[1])

    def load_x1_chunk(sample, c):
        rows = pl.ds(pl.multiple_of(c * (sub * FEATURE_CHUNKS), sub * FEATURE_CHUNKS),
                     sub * FEATURE_CHUNKS)
        return pltpu.make_async_copy(
            x1t_hbm.at[pl.ds(sample * tok_rows + c * (sub * FEATURE_CHUNKS),
                             sub * FEATURE_CHUNKS), :], acc_s.at[rows, :], sems.at[1])

    gates = gate_ref[0, 0]
    gl = gates.shape[1]
    gates = jnp.pad(gates, ((0, SUBLANES - gates.shape[0]), (0, LANES - gl)))
    gate_cols = gates.T
    gate2 = mod_ref[0, N_MOD - 1:N_MOD, :]

    def token_tile(first_row):
        return pl.ds(pl.multiple_of(first_row, FEATURE_CHUNKS), FEATURE_CHUNKS)

    def gather(part):
        for r in range(sub):
            t = idx_ref[0, 0, 0, part * sub + r]
            xs_s[part, r * FEATURE_CHUNKS:(r + 1) * FEATURE_CHUNKS, :] = h2_s[token_tile(t), :]

    def expert(part):
        x = jnp.concatenate(
            [xs_s[part, pl.ds(s, sub, stride=FEATURE_CHUNKS), :] for s in range(FEATURE_CHUNKS)],
            axis=1).astype(BF16)
        hid = (jax.nn.silu(_dot(x, wg_ref[0])) * _dot(x, wu_ref[0])).astype(BF16)
        y = _dot(hid, wd_ref[0]) * gate2
        first_col = part * sub // gl
        y = jnp.concatenate(
            [y[k * gl:(k + 1) * gl, :] * gate_cols[:gl, first_col + k:first_col + k + 1]
             for k in range(sub // gl)], axis=0)
        for s in range(FEATURE_CHUNKS):
            ys_s[part, pl.ds(s, sub, stride=FEATURE_CHUNKS), :] = y[:, s * LANES:(s + 1) * LANES]

    def scatter(part):
        for r0 in range(0, sub, SCATTER_UNROLL):
            updates = []
            for r in range(r0, r0 + SCATTER_UNROLL):
                dst = token_tile(idx_ref[0, 0, 0, part * sub + r])
                row = ys_s[part, r * FEATURE_CHUNKS:(r + 1) * FEATURE_CHUNKS, :]
                updates.append((dst, acc_s[dst, :] + row))
            for dst, val in updates:
                acc_s[dst, :] = val

    @pl.when((e == 0) & (b == 0))
    def _():
        load_h2(b).start()
        load_x1(b).start()

    @pl.when(e == 0)
    def _():
        load_h2(b).wait()

    for part in range(n_parts):
        gather(part)

    @pl.when((e == n_e - 1) & (b + 1 < n_b))
    def _():
        load_h2(b + 1).start()

    expert(0)

    @pl.when(e == 0)
    def _():
        load_x1(b).wait()

    for part in range(1, n_parts):
        expert(part)
    for part in range(n_parts):
        scatter(part)

    @pl.when(e == n_e - 1)
    def _():
        n_out = n // sub

        def store(c, slot):
            return pltpu.make_async_copy(
                stage_s.at[slot], out_hbm.at[pl.ds(b * n + c * sub, sub), :], sems.at[2 + slot])

        def out_body(c, carry):
            slot = c % 2

            @pl.when(c >= 2)
            def _():
                store(c - 2, slot).wait()
            row0 = pl.multiple_of(c * (sub * FEATURE_CHUNKS), sub * FEATURE_CHUNKS)
            for s in range(FEATURE_CHUNKS):
                stage_s[slot, :, s * LANES:(s + 1) * LANES] = (
                    acc_s[pl.ds(row0 + s, sub, stride=FEATURE_CHUNKS), :])
            store(c, slot).start()

            @pl.when(b + 1 < n_b)
            def _():
                load_x1_chunk(b + 1, c).start()
            return carry
        lax.fori_loop(0, n_out, out_body, 0)
        for c in range(n_out - 2, n_out):
            store(c, c % 2).wait()


def _moe(idx, gate, h2t, x1t, mod3, wg, wu, wd, bsz, n):
    cap = idx.shape[-1]
    sub = min(256, cap)
    gl = min(LANES, cap)
    assert n // sub >= 2 and cap % sub == 0 and sub % SCATTER_UNROLL == 0 and sub % gl == 0
    assert cap // gl <= SUBLANES
    tok_rows = n * FEATURE_CHUNKS
    weight_spec = pl.BlockSpec((1, D_MODEL, D_MODEL), lambda b, e: (e, 0, 0))
    smem_spec = pl.BlockSpec((1, 1, 1, cap), lambda b, e: (b, e, 0, 0),
                             memory_space=pltpu.SMEM)
    return pl.pallas_call(
        functools.partial(_moe_kernel, n=n, cap=cap, sub=sub),
        out_shape=jax.ShapeDtypeStruct((bsz * n, D_MODEL), F32),
        grid=(bsz, N_EXPERTS),
        in_specs=[
            smem_spec,
            pl.BlockSpec((1, 1, cap // gl, gl), lambda b, e: (b, e, 0, 0)),
            pl.BlockSpec(memory_space=pl.ANY),
            pl.BlockSpec(memory_space=pl.ANY),
            pl.BlockSpec((1, N_MOD, D_MODEL), lambda b, e: (b, 0, 0)),
            weight_spec,
            weight_spec,
            weight_spec,
        ],
        out_specs=pl.BlockSpec(memory_space=pl.ANY),
        scratch_shapes=[
            pltpu.VMEM((tok_rows, LANES), F32),
            pltpu.VMEM((tok_rows, LANES), F32),
            pltpu.VMEM((cap // sub, sub * FEATURE_CHUNKS, LANES), F32),
            pltpu.VMEM((cap // sub, sub * FEATURE_CHUNKS, LANES), F32),
            pltpu.VMEM((2, sub, D_MODEL), F32),
            pltpu.SemaphoreType.DMA((4,)),
        ],
        compiler_params=pltpu.CompilerParams(
            dimension_semantics=("arbitrary", "arbitrary"), vmem_limit_bytes=VMEM_LIMIT_BYTES),
        name="moe",
    )(idx[:, :, None, :], gate.reshape(bsz, N_EXPERTS, cap // gl, gl), h2t, x1t, mod3, wg, wu, wd)


def _rope_tables(n):
    t = jnp.arange(n)
    row = (t // GRID_W).astype(F32)
    col = (t % GRID_W).astype(F32)
    n_freq = HEAD_DIM // 4
    inv_freq = ROPE_BASE ** (-jnp.arange(n_freq, dtype=F32) / n_freq)
    ang_r = row[:, None] * inv_freq
    ang_c = col[:, None] * inv_freq
    cos = jnp.concatenate([jnp.cos(ang_r)] * 2 + [jnp.cos(ang_c)] * 2, axis=1)
    sin = jnp.concatenate([-jnp.sin(ang_r), jnp.sin(ang_r), -jnp.sin(ang_c), jnp.sin(ang_c)],
                          axis=1)
    return jnp.tile(cos, (1, 2)), jnp.tile(sin, (1, 2))


def _pack_gate_weights(w_r, b_r, w_i, b_i):
    n_p = LRU_WIDTH // LANES
    zeros = jnp.zeros((LRU_BLOCK_DIM, LRU_BLOCK_DIM), F32)

    def pair(w, p):
        top = jnp.concatenate([w[2 * p], zeros], axis=1)
        bot = jnp.concatenate([zeros, w[2 * p + 1]], axis=1)
        return jnp.concatenate([top, bot], axis=0)

    ws, bs = [], []
    for p in range(n_p):
        ws.append(jnp.concatenate(
            [pair(w_r[0], p), pair(w_i[0], p), pair(w_r[1], p), pair(w_i[1], p)], axis=1))
        sl = slice(p * LANES, (p + 1) * LANES)
        bs.append(jnp.concatenate([b_r[0, sl], b_i[0, sl], b_r[1, sl], b_i[1, sl]])[None, :])
    return jnp.stack(ws).astype(BF16), 0.5 * jnp.stack(bs)


def _layer(x, ctx, mod3, norm1_g, norm2_g, w_in, q_norm_g, k_norm_g, attn_sink, conv_w, conv_b,
           lru_w_r, lru_b_r, lru_w_i, lru_b_i, lru_lambda, w_out, w_router, w_gate, w_up, w_down):
    bsz, n, _ = x.shape
    n_ctx = ctx.shape[1]
    x2 = x.reshape(bsz * n, D_MODEL)
    c2 = ctx.reshape(bsz * n_ctx, D_MODEL)
    g1 = norm1_g[None, :]
    g2 = norm2_g[None, :]
    qg = jnp.tile(q_norm_g, 2)[None, :]
    kg = jnp.tile(k_norm_g, 2)[None, :]
    cos, sin = _rope_tables(n)
    w_in_b = w_in.astype(BF16)
    w_ctx_b = w_in_b[:, ATTN_WIDTH:ATTN_WIDTH + 2 * KV_WIDTH + LRU_WIDTH]

    q_s, k2, v2, xr2, gg2 = _inproj_latent(x2, mod3, g1, w_in_b, qg, kg, cos, sin, bsz, n)
    kc2, vc2, xrc2 = _inproj_ctx(c2, mod3, g1, w_ctx_b, kg, bsz, n_ctx)

    o_s = _attn(attn_sink, q_s, k2.reshape(bsz, n, KV_WIDTH), v2.reshape(bsz, n, KV_WIDTH),
                kc2.reshape(bsz, n_ctx, KV_WIDTH), vc2.reshape(bsz, n_ctx, KV_WIDTH))

    wg_lru, bg_lru = _pack_gate_weights(lru_w_r, lru_b_r, lru_w_i, lru_b_i)
    rg3 = _lru(xr2.reshape(bsz, n, LRU_WIDTH), xrc2.reshape(bsz, n_ctx, LRU_WIDTH),
               gg2.reshape(bsz, n, LRU_WIDTH), 0.5 * conv_w, 0.5 * conv_b[None, :], wg_lru,
               bg_lru, lru_lambda)

    w_att = (w_out[:ATTN_WIDTH].reshape(N_KV_HEADS, Q_PER_KV, HEAD_DIM, D_MODEL)
             .transpose(1, 0, 2, 3).reshape(ATTN_WIDTH, D_MODEL).astype(BF16))
    w_rnn = w_out[ATTN_WIDTH:].astype(BF16)
    x1t, h2t, lg4 = _outproj(o_s, rg3.reshape(bsz * n, LRU_WIDTH), x2, mod3, w_att, w_rnn, g2,
                             w_router.T.astype(BF16), bsz, n)

    cap = CAPACITY_FACTOR * n // N_EXPERTS
    idx, gate, (wg_b, wu_b, wd_b) = _topk(lg4, cap, (w_gate, w_up, w_down))
    out = _moe(idx, gate, h2t, x1t, mod3, wg_b, wu_b, wd_b, bsz, n)
    return out.reshape(bsz, n, D_MODEL)


def kernel(x, c, ctx, c_ctx, w_ada, b_ada, norm1_g, norm2_g, w_in, q_norm_g, k_norm_g, attn_sink, conv_w, conv_b, lru_w_r, lru_b_r, lru_w_i, lru_b_i, lru_lambda, w_out, w_router, w_gate, w_up, w_down):
    bsz = x.shape[0]
    depth = w_ada.shape[0]
    assert depth == 1, "context-stream update between layers is not implemented"
    pad_rows = -(bsz + 1) % SUBLANES
    cc = jnp.concatenate([c, jnp.zeros((pad_rows, D_MODEL), F32), c_ctx[None, :]], axis=0)
    for layer in range(depth):
        mod = _ada(cc, w_ada[layer], b_ada[layer][None, :])
        mod3 = mod.reshape(cc.shape[0], N_MOD, D_MODEL)
        x = _layer(x, ctx, mod3, norm1_g[layer], norm2_g[layer], w_in[layer], q_norm_g[layer],
                   k_norm_g[layer], attn_sink[layer], conv_w[layer], conv_b[layer],
                   lru_w_r[layer], lru_b_r[layer], lru_w_i[layer], lru_b_i[layer],
                   lru_lambda[layer], w_out[layer], w_router[layer], w_gate[layer],
                   w_up[layer], w_down[layer])
    return x
```

```python
import functools

import jax
import jax.numpy as jnp
from jax import lax
from jax.experimental import pallas as pl
from jax.experimental.pallas import tpu as pltpu

F32 = jnp.float32
BF16 = jnp.bfloat16

LANES = 128
SUBLANES = 8
VMEM_LIMIT_BYTES = 56 * 1024 * 1024

D_MODEL = 1024
HEAD_DIM = 64
N_Q_HEADS = 8
N_KV_HEADS = 2
Q_PER_KV = N_Q_HEADS // N_KV_HEADS
ATTN_WIDTH = N_Q_HEADS * HEAD_DIM
KV_WIDTH = N_KV_HEADS * HEAD_DIM
LRU_WIDTH = D_MODEL - ATTN_WIDTH
LRU_BLOCK_DIM = 64
IN_WIDTH = ATTN_WIDTH + 2 * KV_WIDTH + 2 * LRU_WIDTH
WINDOW = 128
BLOCK = 128
GRID_W = 64
ROPE_BASE = 10000.0
LRU_C = 8.0
N_EXPERTS = 16
CAPACITY_FACTOR = 2
EPS = 1e-6
NEG_INF = -1e30
LOG2_E = 1.4426950408889634
SCORE_SCALE = HEAD_DIM ** -0.5 * LOG2_E
N_MOD = 6
FEATURE_CHUNKS = D_MODEL // LANES
ATTN_BLOCKS_PER_STEP = 4
CAST_ROWS = 128
SCATTER_UNROLL = 16


def _dot(a, b):
    return jnp.dot(a, b, preferred_element_type=F32)


def _dot_nt(a, b):
    return lax.dot_general(a, b, (((1,), (1,)), ((), ())), preferred_element_type=F32)


def _dot_tn(a, b):
    return lax.dot_general(a, b, (((0,), (0,)), ((), ())), preferred_element_type=F32)


def _split_bf16(x, parts):
    out = []
    r = x
    for _ in range(parts):
        p = r.astype(BF16)
        out.append(p)
        r = r - p.astype(F32)
    return out


def _rmsnorm_rows(x, g):
    ms = jnp.mean(x * x, axis=-1, keepdims=True)
    return x * lax.rsqrt(ms + EPS) * g


def _ada_kernel(c_ref, w_ref, b_ref, o_ref):
    s = jax.nn.silu(c_ref[...])
    s_hi, s_lo = _split_bf16(s, 2)
    w_hi, w_lo = _split_bf16(w_ref[...], 2)
    o_ref[...] = _dot(s_hi, w_hi) + _dot(s_hi, w_lo) + _dot(s_lo, w_hi) + b_ref[...]


def _ada(cc, w_ada, b_ada):
    rows = cc.shape[0]
    width = w_ada.shape[1]
    return pl.pallas_call(
        _ada_kernel,
        out_shape=jax.ShapeDtypeStruct((rows, width), F32),
        grid=(width // D_MODEL,),
        in_specs=[
            pl.BlockSpec((rows, D_MODEL), lambda j: (0, 0)),
            pl.BlockSpec((D_MODEL, D_MODEL), lambda j: (0, j)),
            pl.BlockSpec((1, D_MODEL), lambda j: (0, j)),
        ],
        out_specs=pl.BlockSpec((rows, D_MODEL), lambda j: (0, j)),
        compiler_params=pltpu.CompilerParams(
            dimension_semantics=("arbitrary",), vmem_limit_bytes=VMEM_LIMIT_BYTES),
        name="ada",
    )(cc, w_ada, b_ada)


def _head_norm(p, g):
    lane = lax.broadcasted_iota(jnp.int32, p.shape, 1)
    low = lane < HEAD_DIM
    sq = p * p
    s_low = jnp.sum(jnp.where(low, sq, 0.0), axis=-1, keepdims=True)
    s_high = jnp.sum(jnp.where(low, 0.0, sq), axis=-1, keepdims=True)
    ms = jnp.where(low, s_low, s_high) * (1.0 / HEAD_DIM)
    return p * lax.rsqrt(ms + EPS) * g


def _rope(p, cos, sin_signed):
    lane = lax.broadcasted_iota(jnp.int32, p.shape, 1)
    partner = jnp.where((lane & 16) == 0,
                        pltpu.roll(p, LANES - 16, 1), pltpu.roll(p, 16, 1))
    return p * cos + partner * sin_signed


def _inproj_latent_kernel(x_ref, mod_ref, g1_ref, w_ref, qg_ref, kg_ref, cos_ref, sin_ref,
                          q_ref, k_ref, v_ref, xr_ref, gg_ref):
    x = x_ref[...]
    h = _rmsnorm_rows(x, g1_ref[...]) * (1.0 + mod_ref[0, 1:2, :]) + mod_ref[0, 0:1, :]
    y = _dot(h.astype(BF16), w_ref[...])
    cos = cos_ref[...]
    sin = sin_ref[...]
    pieces = []
    for j in range(ATTN_WIDTH // LANES):
        p = _head_norm(y[:, j * LANES:(j + 1) * LANES], qg_ref[...])
        pieces.append(_rope(p, cos, sin))
    lane = lax.broadcasted_iota(jnp.int32, pieces[0].shape, 1)
    low = lane < HEAD_DIM
    n_blocks = y.shape[0] // BLOCK
    for g in range(Q_PER_KV):
        src0 = pieces[g // 2]
        src1 = pieces[2 + g // 2]
        if g % 2 == 0:
            out = jnp.where(low, src0, pltpu.roll(src1, HEAD_DIM, 1))
        else:
            out = jnp.where(low, pltpu.roll(src0, HEAD_DIM, 1), src1)
        out = (out * SCORE_SCALE).astype(BF16)
        for j in range(n_blocks):
            q_ref[0, j, g * BLOCK:(g + 1) * BLOCK, :] = out[j * BLOCK:(j + 1) * BLOCK, :]
    k = _head_norm(y[:, ATTN_WIDTH:ATTN_WIDTH + KV_WIDTH], kg_ref[...])
    k_ref[...] = _rope(k, cos, sin).astype(BF16)
    o = ATTN_WIDTH + KV_WIDTH
    v_ref[...] = y[:, o:o + KV_WIDTH].astype(BF16)
    o += KV_WIDTH
    xr_ref[...] = y[:, o:o + LRU_WIDTH]
    o += LRU_WIDTH
    gg_ref[...] = jax.nn.gelu(y[:, o:o + LRU_WIDTH])


def _inproj_ctx_kernel(x_ref, mod_ref, g1_ref, w_ref, kg_ref, k_ref, v_ref, xr_ref):
    x = x_ref[...]
    h = _rmsnorm_rows(x, g1_ref[...]) * (1.0 + mod_ref[0, 1:2, :]) + mod_ref[0, 0:1, :]
    y = _dot(h.astype(BF16), w_ref[...])
    k_ref[...] = _head_norm(y[:, 0:KV_WIDTH], kg_ref[...]).astype(BF16)
    v_ref[...] = y[:, KV_WIDTH:2 * KV_WIDTH].astype(BF16)
    xr_ref[...] = y[:, 2 * KV_WIDTH:2 * KV_WIDTH + LRU_WIDTH]


OUTPROJ_ROW_TILE = 1024


def _row_tile(n):
    return min(512, n)


def _inproj_latent(x2, mod3, g1, w_in, qg, kg, cos, sin, bsz, n):
    tm = _row_tile(n)
    tps = n // tm
    rows = bsz * n
    const = lambda i: (0, 0)
    return pl.pallas_call(
        _inproj_latent_kernel,
        out_shape=(
            jax.ShapeDtypeStruct((bsz, n // BLOCK, Q_PER_KV * BLOCK, LANES), BF16),
            jax.ShapeDtypeStruct((rows, KV_WIDTH), BF16),
            jax.ShapeDtypeStruct((rows, KV_WIDTH), BF16),
            jax.ShapeDtypeStruct((rows, LRU_WIDTH), F32),
            jax.ShapeDtypeStruct((rows, LRU_WIDTH), F32),
        ),
        grid=(rows // tm,),
        in_specs=[
            pl.BlockSpec((tm, D_MODEL), lambda i: (i, 0)),
            pl.BlockSpec((1, N_MOD, D_MODEL), lambda i: (i // tps, 0, 0)),
            pl.BlockSpec((1, D_MODEL), const),
            pl.BlockSpec((D_MODEL, IN_WIDTH), const),
            pl.BlockSpec((1, LANES), const),
            pl.BlockSpec((1, LANES), const),
            pl.BlockSpec((tm, LANES), lambda i: (i % tps, 0)),
            pl.BlockSpec((tm, LANES), lambda i: (i % tps, 0)),
        ],
        out_specs=(
            pl.BlockSpec((1, tm // BLOCK, Q_PER_KV * BLOCK, LANES),
                         lambda i: (i // tps, i % tps, 0, 0)),
            pl.BlockSpec((tm, KV_WIDTH), lambda i: (i, 0)),
            pl.BlockSpec((tm, KV_WIDTH), lambda i: (i, 0)),
            pl.BlockSpec((tm, LRU_WIDTH), lambda i: (i, 0)),
            pl.BlockSpec((tm, LRU_WIDTH), lambda i: (i, 0)),
        ),
        compiler_params=pltpu.CompilerParams(
            dimension_semantics=("arbitrary",), vmem_limit_bytes=VMEM_LIMIT_BYTES),
        name="inproj_latent",
    )(x2, mod3, g1, w_in, qg, kg, cos, sin)


def _inproj_ctx(c2, mod3, g1, w_ctx, kg, bsz, n_ctx):
    tm = _row_tile(n_ctx)
    rows = bsz * n_ctx
    ctx_row = mod3.shape[0] - 1
    width = w_ctx.shape[1]
    const = lambda i: (0, 0)
    return pl.pallas_call(
        _inproj_ctx_kernel,
        out_shape=(
            jax.ShapeDtypeStruct((rows, KV_WIDTH), BF16),
            jax.ShapeDtypeStruct((rows, KV_WIDTH), BF16),
            jax.ShapeDtypeStruct((rows, LRU_WIDTH), F32),
        ),
        grid=(rows // tm,),
        in_specs=[
            pl.BlockSpec((tm, D_MODEL), lambda i: (i, 0)),
            pl.BlockSpec((1, N_MOD, D_MODEL), lambda i: (ctx_row, 0, 0)),
            pl.BlockSpec((1, D_MODEL), const),
            pl.BlockSpec((D_MODEL, width), const),
            pl.BlockSpec((1, LANES), const),
        ],
        out_specs=(
            pl.BlockSpec((tm, KV_WIDTH), lambda i: (i, 0)),
            pl.BlockSpec((tm, KV_WIDTH), lambda i: (i, 0)),
            pl.BlockSpec((tm, LRU_WIDTH), lambda i: (i, 0)),
        ),
        compiler_params=pltpu.CompilerParams(
            dimension_semantics=("arbitrary",), vmem_limit_bytes=VMEM_LIMIT_BYTES),
        name="inproj_ctx",
    )(c2, mod3, g1, w_ctx, kg)


def _attn_kernel(sink_ref, q_ref, k_ref, v_ref, kc_ref, vc_ref, *refs, n, nb):
    bias_refs = refs[:ATTN_BLOCKS_PER_STEP]
    o_ref, s0_s, s1_s = refs[ATTN_BLOCKS_PER_STEP:]
    t = pl.program_id(0)
    last = pl.num_programs(0) - 2
    j_scored = (jnp.minimum(t, last) * ATTN_BLOCKS_PER_STEP) % nb
    j_finished = (jnp.maximum(t - 1, 0) * ATTN_BLOCKS_PER_STEP) % nb
    span = 3 * BLOCK

    def window_start(i):
        return pl.multiple_of(jnp.clip((i - 1) * BLOCK, 0, n - span), BLOCK)

    def head_only(t, h):
        lane = lax.broadcasted_iota(jnp.int32, t.shape, 1)
        keep = (lane < HEAD_DIM) if h == 0 else (lane >= HEAD_DIM)
        return jnp.where(keep, t, jnp.zeros_like(t))

    group = range(ATTN_BLOCKS_PER_STEP)
    stages = functools.partial(_attn_stages, sink_ref, q_ref, k_ref, v_ref, kc_ref, vc_ref,
                               bias_refs, o_ref,
                               starts_scored=[window_start(j_scored + u) for u in group],
                               starts_finished=[window_start(j_finished + u) for u in group],
                               head_only=head_only)

    @pl.when(t == 0)
    def _():
        s1_s[...] = jnp.zeros_like(s1_s)

    @pl.when(t % 2 == 0)
    def _():
        stages(s0_s, s1_s)

    @pl.when(t % 2 == 1)
    def _():
        stages(s1_s, s0_s)


def _attn_stages(sink_ref, q_ref, k_ref, v_ref, kc_ref, vc_ref, bias_refs, o_ref, s_new, s_old,
                 starts_scored, starts_finished, head_only):
    for u in range(ATTN_BLOCKS_PER_STEP):
        _attn_block_stages(sink_ref, q_ref.at[0, u], k_ref, v_ref, kc_ref, vc_ref, bias_refs[u],
                           o_ref.at[0, u], s_new.at[u], s_old.at[u], starts_scored[u],
                           starts_finished[u], head_only)


def _attn_block_stages(sink_ref, q_ref, k_ref, v_ref, kc_ref, vc_ref, bias_ref, o_ref, s_new,
                       s_old, start_scored, start_finished, head_only):
    span = 3 * BLOCK

    def with_ones(t, lane_index):
        lane = lax.broadcasted_iota(jnp.int32, t.shape, 1)
        return jnp.where(lane == lane_index, jnp.ones_like(t), t)

    q = q_ref[...]
    kw = k_ref[0, pl.ds(start_scored, span), :]
    kc = kc_ref[0]
    bias = bias_ref[0]
    for h in range(N_KV_HEADS):
        s_new[h, :, 0:span] = _dot_nt(q, head_only(kw, h)) + bias
        s_new[h, :, span:] = _dot_nt(q, head_only(kc, h))

    vw = v_ref[0, pl.ds(start_finished, span), :]
    vc = vc_ref[0]
    rows = q.shape[0]
    group = lax.broadcasted_iota(jnp.int32, (rows, 1), 0) // BLOCK
    acc = jnp.zeros((rows, LANES), F32)
    for h in range(N_KV_HEADS):
        s_loc = s_old[h, :, 0:span]
        s_ctx = s_old[h, :, span:]
        sink = jnp.zeros((rows, 1), F32)
        for g in range(Q_PER_KV):
            sink = jnp.where(group == g, sink_ref[h * Q_PER_KV + g] * LOG2_E, sink)
        m = jnp.maximum(jnp.maximum(jnp.max(s_loc, axis=-1, keepdims=True),
                                    jnp.max(s_ctx, axis=-1, keepdims=True)), sink)
        p_loc = jnp.exp2(s_loc - m).astype(BF16)
        p_ctx = jnp.exp2(s_ctx - m).astype(BF16)
        sum_lane = HEAD_DIM if h == 0 else 0
        o = (_dot(p_loc, with_ones(head_only(vw, h), sum_lane))
             + _dot(p_ctx, with_ones(head_only(vc, h), sum_lane)))
        denom = o[:, sum_lane:sum_lane + 1] + jnp.exp2(sink - m)
        lane = lax.broadcasted_iota(jnp.int32, o.shape, 1)
        own = (lane < HEAD_DIM) if h == 0 else (lane >= HEAD_DIM)
        acc = acc + jnp.where(own, o, 0.0) / denom
    o_ref[...] = acc.astype(BF16)


def _band_bias(n):
    span = 3 * BLOCK
    r = jnp.arange(Q_PER_KV * BLOCK)[:, None] % BLOCK
    c = jnp.arange(span)[None, :]
    offsets = jnp.arange(span // BLOCK)[:, None, None] * BLOCK
    valid = jnp.abs(r - c + offsets) <= WINDOW
    return jnp.where(valid, 0.0, NEG_INF).astype(F32)


def _attn(sink, q_s, k3, v3, kc3, vc3):
    bsz, nb = q_s.shape[0], q_s.shape[1]
    n = k3.shape[1]
    n_ctx = kc3.shape[1]
    per = ATTN_BLOCKS_PER_STEP
    assert nb >= 3 and nb % per == 0
    bias = _band_bias(n)
    n_groups = bsz * nb // per
    grouped = (n_groups, per) + q_s.shape[2:]

    def scored(t):
        return jnp.minimum(t, n_groups - 1)

    def finished(t):
        return jnp.maximum(t - 1, 0)

    def sample(g):
        return g * per // nb

    def placement(u):
        def index(t):
            i = (scored(t) * per + u) % nb
            return (jnp.minimum(i, 1) + (i == nb - 1).astype(jnp.int32), 0, 0)
        return index

    block = (1, per, Q_PER_KV * BLOCK, LANES)
    scores = pltpu.VMEM((per, N_KV_HEADS, Q_PER_KV * BLOCK, 3 * BLOCK + n_ctx), F32)

    return pl.pallas_call(
        functools.partial(_attn_kernel, n=n, nb=nb),
        out_shape=jax.ShapeDtypeStruct(grouped, BF16),
        grid=(n_groups + 1,),
        in_specs=[
            pl.BlockSpec(memory_space=pltpu.SMEM),
            pl.BlockSpec(block, lambda t: (scored(t), 0, 0, 0)),
            pl.BlockSpec((1, n, KV_WIDTH), lambda t: (sample(scored(t)), 0, 0)),
            pl.BlockSpec((1, n, KV_WIDTH), lambda t: (sample(finished(t)), 0, 0)),
            pl.BlockSpec((1, n_ctx, KV_WIDTH), lambda t: (sample(scored(t)), 0, 0)),
            pl.BlockSpec((1, n_ctx, KV_WIDTH), lambda t: (sample(finished(t)), 0, 0)),
        ] + [pl.BlockSpec((1, Q_PER_KV * BLOCK, 3 * BLOCK), placement(u)) for u in range(per)],
        out_specs=pl.BlockSpec(block, lambda t: (finished(t), 0, 0, 0)),
        scratch_shapes=[scores, scores],
        compiler_params=pltpu.CompilerParams(
            dimension_semantics=("arbitrary",), vmem_limit_bytes=VMEM_LIMIT_BYTES),
        name="attn",
    )(sink, q_s.reshape(grouped), k3, v3, kc3, vc3, *([bias] * per)).reshape(q_s.shape)


def _softplus(z):
    return jnp.maximum(z, 0.0) + jnp.log1p(jnp.exp(-jnp.abs(z)))


LRU_MAIN_SEG = 252


def _lru_plan(length):
    main, rest = divmod(length // SUBLANES, LRU_MAIN_SEG)
    if rest == 0:
        tails = ()
    elif rest % SUBLANES == 0 and rest > SUBLANES:
        tails = (rest - 5, 5)
    else:
        tails = (rest,)
    assert length % SUBLANES == 0 and all(t >= 2 for t in tails)
    return main, tails


def _lru_kernel(xr_ref, xrc_ref, gg_ref, cw_ref, cb_ref, wg_ref, bg_ref, lam_ref, o_ref,
                xc_s, xcc_s, hf_s, hb_s, *, n, n_ctx):
    cw = cw_ref[...]
    cb = cb_ref[...]
    sub = lax.broadcasted_iota(jnp.int32, (SUBLANES, LANES), 0)
    main_rows = SUBLANES * LRU_MAIN_SEG
    n_main, tails = _lru_plan(n)
    c_main, c_tails = _lru_plan(n_ctx)

    def static_tiles(n_main_tiles, tail_segs, with_main):
        tiles = [(k * main_rows, LRU_MAIN_SEG) for k in range(n_main_tiles)] if with_main else []
        t0 = n_main_tiles * main_rows
        for seg in tail_segs:
            tiles.append((t0, seg))
            t0 += SUBLANES * seg
        return tiles

    def aligned(v):
        return v if isinstance(v, int) else pl.multiple_of(v, SUBLANES)

    def conv_tile(src_ref, dst_ref, t0, seg, length):
        x = [src_ref[0, pl.ds(t0 + j, SUBLANES, stride=seg), :] for j in range(seg)]
        lo = aligned(jnp.maximum(t0 - SUBLANES, 0))
        hi = aligned(jnp.minimum(t0 + SUBLANES * seg, length - SUBLANES))
        prev = jnp.where(t0 > 0, src_ref[0, pl.ds(lo, SUBLANES), :], 0.0)
        nxt = jnp.where(t0 + SUBLANES * seg < length, src_ref[0, pl.ds(hi, SUBLANES), :], 0.0)

        def from_prev_segment(v, row):
            return jnp.where(sub == 0, row, pltpu.roll(v, 1, 0))

        def from_next_segment(v, row):
            return jnp.where(sub == SUBLANES - 1, row, pltpu.roll(v, SUBLANES - 1, 0))

        m1 = [from_prev_segment(x[seg - 1], prev[7:8])] + x[:seg - 1]
        m2 = [from_prev_segment(x[seg - 2], prev[6:7])] + m1[:seg - 1]
        p1 = x[1:] + [from_next_segment(x[0], nxt[0:1])]
        for j in range(seg):
            y = cb + cw[0:1] * m2[j]
            y = y + cw[1:2] * m1[j]
            y = y + cw[2:3] * x[j]
            y = y + cw[3:4] * p1[j]
            dst_ref[pl.ds(t0 + j * SUBLANES, SUBLANES), :] = y

    half_rate = [-0.5 * LRU_C * _softplus(-lam_ref[d:d + 1, :]) for d in range(2)]

    def coeffs(xc, d):
        w = wg_ref[0, :, 2 * LANES * d:2 * LANES * (d + 1)]
        g = _dot(xc.astype(BF16), w) + bg_ref[0, :, 2 * LANES * d:2 * LANES * (d + 1)]
        t_r = jnp.tanh(g[:, :LANES])
        t_i = jnp.tanh(g[:, LANES:])
        log_a = t_r * half_rate[d] + half_rate[d]
        a = jnp.exp(log_a)
        mult = jnp.sqrt(-jnp.tanh(log_a) * (a * a + 1.0))
        return a, mult * (t_i + 1.0) * xc

    def scan_tile(a, b, h_in, seg, reverse):
        order = list(range(seg - 1, -1, -1) if reverse else range(seg))
        local = [None] * seg
        prod = [None] * seg
        h = p = None
        for j in order:
            aj = a[j * SUBLANES:(j + 1) * SUBLANES, :]
            bj = b[j * SUBLANES:(j + 1) * SUBLANES, :]
            h = bj if h is None else aj * h + bj
            p = aj if p is None else aj * p
            local[j], prod[j] = h, p
        end = order[-1]
        carry = jnp.zeros((SUBLANES, LANES), F32)
        c = h_in
        for s in (range(SUBLANES - 1, -1, -1) if reverse else range(SUBLANES)):
            carry = jnp.where(sub == s, c, carry)
            c = prod[end][s:s + 1, :] * c + local[end][s:s + 1, :]
        return [local[j] + prod[j] * carry for j in range(seg)], c

    def run_tile(xc_ref, t0, seg, d, h, emit=None):
        a, b = coeffs(xc_ref[pl.ds(t0, SUBLANES * seg), :], d)
        states, h = scan_tile(a, b, h, seg, d == 1)
        if emit is not None:
            emit(t0, seg, states)
        return h

    def keep(dst_ref):
        def emit(t0, seg, states):
            for j in range(seg):
                dst_ref[pl.ds(t0 + j * SUBLANES, SUBLANES), :] = states[j]
        return emit

    def write_output(other_ref):
        def emit(t0, seg, states):
            for j in range(seg):
                rows = pl.ds(t0 + j, SUBLANES, stride=seg)
                other = other_ref[pl.ds(t0 + j * SUBLANES, SUBLANES), :]
                o_ref[0, rows, :] = (other + states[j]) * gg_ref[0, rows, :]
        return emit

    def main_t0(k):
        return pl.multiple_of(k * main_rows, SUBLANES)

    for t0, seg in static_tiles(c_main, c_tails, True):
        conv_tile(xrc_ref, xcc_s, t0, seg, n_ctx)

    def conv_body(k, carry):
        conv_tile(xr_ref, xc_s, main_t0(k), LRU_MAIN_SEG, n)
        return carry
    lax.fori_loop(0, n_main, conv_body, 0)
    for t0, seg in static_tiles(n_main, tails, False):
        conv_tile(xr_ref, xc_s, t0, seg, n)

    hf = jnp.zeros((1, LANES), F32)
    hb = jnp.zeros((1, LANES), F32)
    for t0, seg in static_tiles(c_main, c_tails, True):
        hf = run_tile(xcc_s, t0, seg, 0, hf)
    for t0, seg in reversed(static_tiles(c_main, c_tails, True)):
        hb = run_tile(xcc_s, t0, seg, 1, hb)
    for t0, seg in reversed(static_tiles(n_main, tails, False)):
        hb = run_tile(xc_s, t0, seg, 1, hb, keep(hb_s))

    def pair_body(first_visit):
        def body(k, carry):
            hf, hb = carry
            fwd_emit = keep(hf_s) if first_visit else write_output(hb_s)
            rev_emit = keep(hb_s) if first_visit else write_output(hf_s)
            hf = run_tile(xc_s, main_t0(k), LRU_MAIN_SEG, 0, hf, fwd_emit)
            hb = run_tile(xc_s, main_t0(n_main - 1 - k), LRU_MAIN_SEG, 1, hb, rev_emit)
            return hf, hb
        return body

    half = n_main // 2
    hf, hb = lax.fori_loop(0, half, pair_body(True), (hf, hb))
    if n_main % 2:
        hf = run_tile(xc_s, half * main_rows, LRU_MAIN_SEG, 0, hf, keep(hf_s))
        hb = run_tile(xc_s, half * main_rows, LRU_MAIN_SEG, 1, hb, write_output(hf_s))
    hf, hb = lax.fori_loop(n_main - half, n_main, pair_body(False), (hf, hb))
    for t0, seg in static_tiles(n_main, tails, False):
        hf = run_tile(xc_s, t0, seg, 0, hf, write_output(hb_s))


def _lru(xr3, xrc3, gg3, conv_w, conv_b, wg, bg, lam):
    bsz, n, _ = xr3.shape
    n_ctx = xrc3.shape[1]
    n_p = LRU_WIDTH // LANES
    return pl.pallas_call(
        functools.partial(_lru_kernel, n=n, n_ctx=n_ctx),
        out_shape=jax.ShapeDtypeStruct((bsz, n, LRU_WIDTH), F32),
        grid=(bsz, n_p),
        in_specs=[
            pl.BlockSpec((1, n, LANES), lambda b, p: (b, 0, p)),
            pl.BlockSpec((1, n_ctx, LANES), lambda b, p: (b, 0, p)),
            pl.BlockSpec((1, n, LANES), lambda b, p: (b, 0, p)),
            pl.BlockSpec((conv_w.shape[0], LANES), lambda b, p: (0, p)),
            pl.BlockSpec((1, LANES), lambda b, p: (0, p)),
            pl.BlockSpec((1, LANES, 4 * LANES), lambda b, p: (p, 0, 0)),
            pl.BlockSpec((1, 1, 4 * LANES), lambda b, p: (p, 0, 0)),
            pl.BlockSpec((2, LANES), lambda b, p: (0, p)),
        ],
        out_specs=pl.BlockSpec((1, n, LANES), lambda b, p: (b, 0, p)),
        scratch_shapes=[
            pltpu.VMEM((n, LANES), F32),
            pltpu.VMEM((n_ctx, LANES), F32),
            pltpu.VMEM((n, LANES), F32),
            pltpu.VMEM((n, LANES), F32),
        ],
        compiler_params=pltpu.CompilerParams(
            dimension_semantics=("arbitrary", "arbitrary"), vmem_limit_bytes=VMEM_LIMIT_BYTES),
        name="lru",
    )(xr3, xrc3, gg3, conv_w, conv_b, wg, bg, lam)


def _outproj_kernel(o_ref, rg_ref, x_ref, mod_ref, watt_ref, wrnn_ref, g2_ref, wr_ref,
                    x1t_ref, h2t_ref, lg_ref):
    tm = x_ref.shape[0]
    n_blocks = tm // BLOCK
    att = jnp.concatenate(
        [jnp.concatenate([o_ref[0, j, g * BLOCK:(g + 1) * BLOCK, :] for g in range(Q_PER_KV)],
                         axis=1) for j in range(n_blocks)], axis=0)
    y = _dot(att, watt_ref[...]) + _dot(rg_ref[...].astype(BF16), wrnn_ref[...])
    x1 = x_ref[...] + mod_ref[0, 2:3, :] * y
    h2 = _rmsnorm_rows(x1, g2_ref[...]) * (1.0 + mod_ref[0, 4:5, :]) + mod_ref[0, 3:4, :]
    for s in range(FEATURE_CHUNKS):
        rows = pl.ds(s, tm, stride=FEATURE_CHUNKS)
        x1t_ref[rows, :] = x1[:, s * LANES:(s + 1) * LANES]
        h2t_ref[rows, :] = h2[:, s * LANES:(s + 1) * LANES]
    lt = _dot_nt(wr_ref[...], h2.astype(BF16))
    for j in range(n_blocks):
        lg_ref[0, j] = lt[:, j * LANES:(j + 1) * LANES]


def _outproj(o_s, rg2, x2, mod3, w_att, w_rnn, g2, wr_t, bsz, n):
    tm = min(OUTPROJ_ROW_TILE, n)
    tps = n // tm
    rows = bsz * n
    const = lambda i: (0, 0)
    return pl.pallas_call(
        _outproj_kernel,
        out_shape=(
            jax.ShapeDtypeStruct((rows * FEATURE_CHUNKS, LANES), F32),
            jax.ShapeDtypeStruct((rows * FEATURE_CHUNKS, LANES), F32),
            jax.ShapeDtypeStruct((bsz, n // LANES, N_EXPERTS, LANES), F32),
        ),
        grid=(rows // tm,),
        in_specs=[
            pl.BlockSpec((1, tm // BLOCK, Q_PER_KV * BLOCK, LANES),
                         lambda i: (i // tps, i % tps, 0, 0)),
            pl.BlockSpec((tm, LRU_WIDTH), lambda i: (i, 0)),
            pl.BlockSpec((tm, D_MODEL), lambda i: (i, 0)),
            pl.BlockSpec((1, N_MOD, D_MODEL), lambda i: (i // tps, 0, 0)),
            pl.BlockSpec((ATTN_WIDTH, D_MODEL), const),
            pl.BlockSpec((LRU_WIDTH, D_MODEL), const),
            pl.BlockSpec((1, D_MODEL), const),
            pl.BlockSpec((N_EXPERTS, D_MODEL), const),
        ],
        out_specs=(
            pl.BlockSpec((tm * FEATURE_CHUNKS, LANES), lambda i: (i, 0)),
            pl.BlockSpec((tm * FEATURE_CHUNKS, LANES), lambda i: (i, 0)),
            pl.BlockSpec((1, tm // LANES, N_EXPERTS, LANES), lambda i: (i // tps, i % tps, 0, 0)),
        ),
        compiler_params=pltpu.CompilerParams(
            dimension_semantics=("arbitrary",), vmem_limit_bytes=VMEM_LIMIT_BYTES),
        name="outproj",
    )(o_s, rg2, x2, mod3, w_att, w_rnn, g2, wr_t)


def _topk_kernel(lg_ref, *refs, cap, n_cast):
    w_refs = refs[:n_cast]
    idx_ref, gate_ref = refs[n_cast:n_cast + 2]
    wb_refs = refs[n_cast + 2:2 * n_cast + 2]
    aff_s, cum_s, before_s = refs[2 * n_cast + 2:]
    for w_ref, wb_ref in zip(w_refs, wb_refs):
        def cast_rows(i, carry):
            rows = pl.ds(pl.multiple_of(i * CAST_ROWS, CAST_ROWS), CAST_ROWS)
            for k in range(w_ref.shape[0]):
                wb_ref[k, rows, :] = w_ref[k, rows, :].astype(BF16)
            return carry
        lax.fori_loop(0, w_ref.shape[1] // CAST_ROWS, cast_rows, 0)

    lg = lg_ref[0]
    n_chunks = lg.shape[0]
    m = jnp.max(lg, axis=1, keepdims=True)
    ex = jnp.exp(lg - m)
    aff = ex / jnp.sum(ex, axis=1, keepdims=True)
    aff_s[...] = aff

    def count(mask):
        c = jnp.sum(jnp.where(mask, 1.0, 0.0), axis=0, keepdims=True)
        return jnp.sum(c, axis=2, keepdims=True)

    def bit_body(it, thr):
        cand = thr | jnp.left_shift(jnp.int32(1), 30 - it)
        return jnp.where(count(aff >= pltpu.bitcast(cand, F32)) >= cap, cand, thr)
    thr = lax.fori_loop(0, 31, bit_body, jnp.zeros((1, N_EXPERTS, 1), jnp.int32))
    thr = pltpu.bitcast(thr, F32)

    tri = (lax.broadcasted_iota(jnp.int32, (LANES, LANES), 0)
           <= lax.broadcasted_iota(jnp.int32, (LANES, LANES), 1))
    tri = jnp.where(tri, 1.0, 0.0).astype(BF16)

    def chunk_cumsum(mask):
        flat = jnp.where(mask, 1.0, 0.0).astype(BF16).reshape(n_chunks * N_EXPERTS, LANES)
        inc = _dot(flat, tri).reshape(n_chunks, N_EXPERTS, LANES)
        run = jnp.zeros((N_EXPERTS, 1), F32)
        before = []
        for c in range(n_chunks):
            before.append(run)
            run = run + inc[c][:, LANES - 1:LANES]
        return inc, jnp.stack(before, axis=0)

    above = aff > thr
    tied = aff == thr
    need = cap - count(above)
    tie_inc, tie_before = chunk_cumsum(tied)
    tie_rank = tie_before + tie_inc - jnp.where(tied, 1.0, 0.0)
    sel = above | (tied & (tie_rank < need))
    sel_inc, sel_before = chunk_cumsum(sel)
    cum_s[...] = sel_inc
    before_s[...] = jnp.broadcast_to(sel_before, sel_inc.shape)

    slot = lax.broadcasted_iota(jnp.int32, (1, cap), 1).astype(F32)
    chunk_id = lax.broadcasted_iota(jnp.int32, (n_chunks, cap), 0).astype(F32)
    lane_id = lax.broadcasted_iota(jnp.int32, (LANES, cap), 0).astype(F32)
    for e in range(N_EXPERTS):
        cin = cum_s[:, e, :]
        cc_exc = before_s[:, e, :][:, 0:1]
        cc_inc = cc_exc + cin[:, LANES - 1:LANES]
        kc = jnp.sum(jnp.where(cc_inc <= slot, 1.0, 0.0), axis=0, keepdims=True)
        onehot = chunk_id == kc
        onehot_b = jnp.where(onehot, 1.0, 0.0).astype(BF16)
        counts_t = _dot_tn(cin.astype(BF16), onehot_b)
        local = slot - jnp.sum(jnp.where(onehot, cc_exc, 0.0), axis=0, keepdims=True)
        il = jnp.sum(jnp.where(counts_t <= local, 1.0, 0.0), axis=0, keepdims=True)
        idx_ref[0, e:e + 1, :] = ((kc * LANES + il) * FEATURE_CHUNKS).astype(jnp.int32)
        aff_t = jnp.zeros((LANES, cap), F32)
        for part in _split_bf16(aff_s[:, e, :], 3):
            aff_t = aff_t + _dot_tn(part, onehot_b)
        gate_ref[0, e:e + 1, :] = jnp.sum(jnp.where(lane_id == il, aff_t, 0.0),
                                          axis=0, keepdims=True)


def _topk(lg4, cap, expert_weights):
    bsz, n_chunks = lg4.shape[0], lg4.shape[1]
    if N_EXPERTS % bsz:
        cast_in, cast_out = (), tuple(w.astype(BF16) for w in expert_weights)
    else:
        cast_in, cast_out = tuple(expert_weights), ()
    per_step = N_EXPERTS // bsz if cast_in else 0
    w_specs = [pl.BlockSpec((per_step,) + w.shape[1:], lambda b: (b, 0, 0)) for w in cast_in]
    outs = pl.pallas_call(
        functools.partial(_topk_kernel, cap=cap, n_cast=len(cast_in)),
        out_shape=(
            jax.ShapeDtypeStruct((bsz, N_EXPERTS, cap), jnp.int32),
            jax.ShapeDtypeStruct((bsz, N_EXPERTS, cap), F32),
        ) + tuple(jax.ShapeDtypeStruct(w.shape, BF16) for w in cast_in),
        grid=(bsz,),
        in_specs=[pl.BlockSpec((1, n_chunks, N_EXPERTS, LANES), lambda b: (b, 0, 0, 0))] + w_specs,
        out_specs=(
            pl.BlockSpec((1, N_EXPERTS, cap), lambda b: (b, 0, 0)),
            pl.BlockSpec((1, N_EXPERTS, cap), lambda b: (b, 0, 0)),
        ) + tuple(w_specs),
        scratch_shapes=[
            pltpu.VMEM((n_chunks, N_EXPERTS, LANES), F32),
            pltpu.VMEM((n_chunks, N_EXPERTS, LANES), F32),
            pltpu.VMEM((n_chunks, N_EXPERTS, LANES), F32),
        ],
        compiler_params=pltpu.CompilerParams(
            dimension_semantics=("arbitrary",), vmem_limit_bytes=VMEM_LIMIT_BYTES),
        name="topk",
    )(lg4, *cast_in)
    return outs[0], outs[1], tuple(outs[2:]) + cast_out


def _moe_kernel(idx_ref, gate_ref, h2t_hbm, x1t_hbm, mod_ref, wg_ref, wu_ref, wd_ref, out_hbm,
                h2_s, acc_s, xs_s, ys_s, stage_s, sems, *, n, cap, sub):
    b = pl.program_id(0)
    e = pl.program_id(1)
    n_b = pl.num_programs(0)
    n_e = pl.num_programs(1)
    tok_rows = n * FEATURE_CHUNKS
    n_parts = cap // sub

    def load_h2(sample):
        return pltpu.make_async_copy(
            h2t_hbm.at[pl.ds(sample * tok_rows, tok_rows), :], h2_s, sems.at[0])

    def load_x1(sample):
        return pltpu.make_async_copy(
            x1t_hbm.at[pl.ds(sample * tok_rows, tok_rows), :], acc_s, sems.at[1])

    def load_x1_chunk(sample, c):
        rows = pl.ds(pl.multiple_of(c * (sub * FEATURE_CHUNKS), sub * FEATURE_CHUNKS),
                     sub * FEATURE_CHUNKS)
        return pltpu.make_async_copy(
            x1t_hbm.at[pl.ds(sample * tok_rows + c * (sub * FEATURE_CHUNKS),
                             sub * FEATURE_CHUNKS), :], acc_s.at[rows, :], sems.at[1])

    gates = gate_ref[0, 0]
    gl = gates.shape[1]
    gates = jnp.pad(gates, ((0, SUBLANES - gates.shape[0]), (0, LANES - gl)))
    gate_cols = gates.T
    gate2 = mod_ref[0, N_MOD - 1:N_MOD, :]

    def token_tile(first_row):
        return pl.ds(pl.multiple_of(first_row, FEATURE_CHUNKS), FEATURE_CHUNKS)

    def gather(part):
        for r in range(sub):
            t = idx_ref[0, 0, 0, part * sub + r]
            xs_s[part, r * FEATURE_CHUNKS:(r + 1) * FEATURE_CHUNKS, :] = h2_s[token_tile(t), :]

    def expert(part):
        x = jnp.concatenate(
            [xs_s[part, pl.ds(s, sub, stride=FEATURE_CHUNKS), :] for s in range(FEATURE_CHUNKS)],
            axis=1).astype(BF16)
        hid = (jax.nn.silu(_dot(x, wg_ref[0])) * _dot(x, wu_ref[0])).astype(BF16)
        y = _dot(hid, wd_ref[0]) * gate2
        first_col = part * sub // gl
        y = jnp.concatenate(
            [y[k * gl:(k + 1) * gl, :] * gate_cols[:gl, first_col + k:first_col + k + 1]
             for k in range(sub // gl)], axis=0)
        for s in range(FEATURE_CHUNKS):
            ys_s[part, pl.ds(s, sub, stride=FEATURE_CHUNKS), :] = y[:, s * LANES:(s + 1) * LANES]

    def scatter(part):
        for r0 in range(0, sub, SCATTER_UNROLL):
            updates = []
            for r in range(r0, r0 + SCATTER_UNROLL):
                dst = token_tile(idx_ref[0, 0, 0, part * sub + r])
                row = ys_s[part, r * FEATURE_CHUNKS:(r + 1) * FEATURE_CHUNKS, :]
                updates.append((dst, acc_s[dst, :] + row))
            for dst, val in updates:
                acc_s[dst, :] = val

    @pl.when((e == 0) & (b == 0))
    def _():
        load_h2(b).start()
        load_x1(b).start()

    @pl.when(e == 0)
    def _():
        load_h2(b).wait()

    for part in range(n_parts):
        gather(part)

    @pl.when((e == n_e - 1) & (b + 1 < n_b))
    def _():
        load_h2(b + 1).start()

    expert(0)

    @pl.when(e == 0)
    def _():
        load_x1(b).wait()

    for part in range(1, n_parts):
        expert(part)
    for part in range(n_parts):
        scatter(part)

    @pl.when(e == n_e - 1)
    def _():
        n_out = n // sub

        def store(c, slot):
            return pltpu.make_async_copy(
                stage_s.at[slot], out_hbm.at[pl.ds(b * n + c * sub, sub), :], sems.at[2 + slot])

        def out_body(c, carry):
            slot = c % 2

            @pl.when(c >= 2)
            def _():
                store(c - 2, slot).wait()
            row0 = pl.multiple_of(c * (sub * FEATURE_CHUNKS), sub * FEATURE_CHUNKS)
            for s in range(FEATURE_CHUNKS):
                stage_s[slot, :, s * LANES:(s + 1) * LANES] = (
                    acc_s[pl.ds(row0 + s, sub, stride=FEATURE_CHUNKS), :])
            store(c, slot).start()

            @pl.when(b + 1 < n_b)
            def _():
                load_x1_chunk(b + 1, c).start()
            return carry
        lax.fori_loop(0, n_out, out_body, 0)
        for c in range(n_out - 2, n_out):
            store(c, c % 2).wait()


def _moe(idx, gate, h2t, x1t, mod3, wg, wu, wd, bsz, n):
    cap = idx.shape[-1]
    sub = min(256, cap)
    gl = min(LANES, cap)
    assert n // sub >= 2 and cap % sub == 0 and sub % SCATTER_UNROLL == 0 and sub % gl == 0
    assert cap // gl <= SUBLANES
    tok_rows = n * FEATURE_CHUNKS
    weight_spec = pl.BlockSpec((1, D_MODEL, D_MODEL), lambda b, e: (e, 0, 0))
    smem_spec = pl.BlockSpec((1, 1, 1, cap), lambda b, e: (b, e, 0, 0),
                             memory_space=pltpu.SMEM)
    return pl.pallas_call(
        functools.partial(_moe_kernel, n=n, cap=cap, sub=sub),
        out_shape=jax.ShapeDtypeStruct((bsz * n, D_MODEL), F32),
        grid=(bsz, N_EXPERTS),
        in_specs=[
            smem_spec,
            pl.BlockSpec((1, 1, cap // gl, gl), lambda b, e: (b, e, 0, 0)),
            pl.BlockSpec(memory_space=pl.ANY),
            pl.BlockSpec(memory_space=pl.ANY),
            pl.BlockSpec((1, N_MOD, D_MODEL), lambda b, e: (b, 0, 0)),
            weight_spec,
            weight_spec,
            weight_spec,
        ],
        out_specs=pl.BlockSpec(memory_space=pl.ANY),
        scratch_shapes=[
            pltpu.VMEM((tok_rows, LANES), F32),
            pltpu.VMEM((tok_rows, LANES), F32),
            pltpu.VMEM((cap // sub, sub * FEATURE_CHUNKS, LANES), F32),
            pltpu.VMEM((cap // sub, sub * FEATURE_CHUNKS, LANES), F32),
            pltpu.VMEM((2, sub, D_MODEL), F32),
            pltpu.SemaphoreType.DMA((4,)),
        ],
        compiler_params=pltpu.CompilerParams(
            dimension_semantics=("arbitrary", "arbitrary"), vmem_limit_bytes=VMEM_LIMIT_BYTES),
        name="moe",
    )(idx[:, :, None, :], gate.reshape(bsz, N_EXPERTS, cap // gl, gl), h2t, x1t, mod3, wg, wu, wd)


def _rope_tables(n):
    t = jnp.arange(n)
    row = (t // GRID_W).astype(F32)
    col = (t % GRID_W).astype(F32)
    n_freq = HEAD_DIM // 4
    inv_freq = ROPE_BASE ** (-jnp.arange(n_freq, dtype=F32) / n_freq)
    ang_r = row[:, None] * inv_freq
    ang_c = col[:, None] * inv_freq
    cos = jnp.concatenate([jnp.cos(ang_r)] * 2 + [jnp.cos(ang_c)] * 2, axis=1)
    sin = jnp.concatenate([-jnp.sin(ang_r), jnp.sin(ang_r), -jnp.sin(ang_c), jnp.sin(ang_c)],
                          axis=1)
    return jnp.tile(cos, (1, 2)), jnp.tile(sin, (1, 2))


def _pack_gate_weights(w_r, b_r, w_i, b_i):
    n_p = LRU_WIDTH // LANES
    zeros = jnp.zeros((LRU_BLOCK_DIM, LRU_BLOCK_DIM), F32)

    def pair(w, p):
        top = jnp.concatenate([w[2 * p], zeros], axis=1)
        bot = jnp.concatenate([zeros, w[2 * p + 1]], axis=1)
        return jnp.concatenate([top, bot], axis=0)

    ws, bs = [], []
    for p in range(n_p):
        ws.append(jnp.concatenate(
            [pair(w_r[0], p), pair(w_i[0], p), pair(w_r[1], p), pair(w_i[1], p)], axis=1))
        sl = slice(p * LANES, (p + 1) * LANES)
        bs.append(jnp.concatenate([b_r[0, sl], b_i[0, sl], b_r[1, sl], b_i[1, sl]])[None, :])
    return jnp.stack(ws).astype(BF16), 0.5 * jnp.stack(bs)


def _layer(x, ctx, mod3, norm1_g, norm2_g, w_in, q_norm_g, k_norm_g, attn_sink, conv_w, conv_b,
           lru_w_r, lru_b_r, lru_w_i, lru_b_i, lru_lambda, w_out, w_router, w_gate, w_up, w_down):
    bsz, n, _ = x.shape
    n_ctx = ctx.shape[1]
    x2 = x.reshape(bsz * n, D_MODEL)
    c2 = ctx.reshape(bsz * n_ctx, D_MODEL)
    g1 = norm1_g[None, :]
    g2 = norm2_g[None, :]
    qg = jnp.tile(q_norm_g, 2)[None, :]
    kg = jnp.tile(k_norm_g, 2)[None, :]
    cos, sin = _rope_tables(n)
    w_in_b = w_in.astype(BF16)
    w_ctx_b = w_in_b[:, ATTN_WIDTH:ATTN_WIDTH + 2 * KV_WIDTH + LRU_WIDTH]

    q_s, k2, v2, xr2, gg2 = _inproj_latent(x2, mod3, g1, w_in_b, qg, kg, cos, sin, bsz, n)
    kc2, vc2, xrc2 = _inproj_ctx(c2, mod3, g1, w_ctx_b, kg, bsz, n_ctx)

    o_s = _attn(attn_sink, q_s, k2.reshape(bsz, n, KV_WIDTH), v2.reshape(bsz, n, KV_WIDTH),
                kc2.reshape(bsz, n_ctx, KV_WIDTH), vc2.reshape(bsz, n_ctx, KV_WIDTH))

    wg_lru, bg_lru = _pack_gate_weights(lru_w_r, lru_b_r, lru_w_i, lru_b_i)
    rg3 = _lru(xr2.reshape(bsz, n, LRU_WIDTH), xrc2.reshape(bsz, n_ctx, LRU_WIDTH),
               gg2.reshape(bsz, n, LRU_WIDTH), 0.5 * conv_w, 0.5 * conv_b[None, :], wg_lru,
               bg_lru, lru_lambda)

    w_att = (w_out[:ATTN_WIDTH].reshape(N_KV_HEADS, Q_PER_KV, HEAD_DIM, D_MODEL)
             .transpose(1, 0, 2, 3).reshape(ATTN_WIDTH, D_MODEL).astype(BF16))
    w_rnn = w_out[ATTN_WIDTH:].astype(BF16)
    x1t, h2t, lg4 = _outproj(o_s, rg3.reshape(bsz * n, LRU_WIDTH), x2, mod3, w_att, w_rnn, g2,
                             w_router.T.astype(BF16), bsz, n)

    cap = CAPACITY_FACTOR * n // N_EXPERTS
    idx, gate, (wg_b, wu_b, wd_b) = _topk(lg4, cap, (w_gate, w_up, w_down))
    out = _moe(idx, gate, h2t, x1t, mod3, wg_b, wu_b, wd_b, bsz, n)
    return out.reshape(bsz, n, D_MODEL)


def kernel(x, c, ctx, c_ctx, w_ada, b_ada, norm1_g, norm2_g, w_in, q_norm_g, k_norm_g, attn_sink, conv_w, conv_b, lru_w_r, lru_b_r, lru_w_i, lru_b_i, lru_lambda, w_out, w_router, w_gate, w_up, w_down):
    bsz = x.shape[0]
    depth = w_ada.shape[0]
    assert depth == 1, "context-stream update between layers is not implemented"
    pad_rows = -(bsz + 1) % SUBLANES
    cc = jnp.concatenate([c, jnp.zeros((pad_rows, D_MODEL), F32), c_ctx[None, :]], axis=0)
    for layer in range(depth):
        mod = _ada(cc, w_ada[layer], b_ada[layer][None, :])
        mod3 = mod.reshape(cc.shape[0], N_MOD, D_MODEL)
        x = _layer(x, ctx, mod3, norm1_g[layer], norm2_g[layer], w_in[layer], q_norm_g[layer],
                   k_norm_g[layer], attn_sink[layer], conv_w[layer], conv_b[layer],
                   lru_w_r[layer], lru_b_r[layer], lru_w_i[layer], lru_b_i[layer],
                   lru_lambda[layer], w_out[layer], w_router[layer], w_gate[layer],
                   w_up[layer], w_down[layer])
    return x
```

```python
import functools

import jax
import jax.numpy as jnp
from jax import lax
from jax.experimental import pallas as pl
from jax.experimental.pallas import tpu as pltpu

F32 = jnp.float32
BF16 = jnp.bfloat16

LANES = 128
SUBLANES = 8
VMEM_LIMIT_BYTES = 56 * 1024 * 1024

D_MODEL = 1024
HEAD_DIM = 64
N_Q_HEADS = 8
N_KV_HEADS = 2
Q_PER_KV = N_Q_HEADS // N_KV_HEADS
ATTN_WIDTH = N_Q_HEADS * HEAD_DIM
KV_WIDTH = N_KV_HEADS * HEAD_DIM
LRU_WIDTH = D_MODEL - ATTN_WIDTH
LRU_BLOCK_DIM = 64
IN_WIDTH = ATTN_WIDTH + 2 * KV_WIDTH + 2 * LRU_WIDTH
WINDOW = 128
BLOCK = 128
GRID_W = 64
ROPE_BASE = 10000.0
LRU_C = 8.0
N_EXPERTS = 16
CAPACITY_FACTOR = 2
EPS = 1e-6
NEG_INF = -1e30
LOG2_E = 1.4426950408889634
SCORE_SCALE = HEAD_DIM ** -0.5 * LOG2_E
N_MOD = 6
FEATURE_CHUNKS = D_MODEL // LANES
ATTN_BLOCKS_PER_STEP = 4
CAST_ROWS = 128
SCATTER_UNROLL = 16


def _dot(a, b):
    return jnp.dot(a, b, preferred_element_type=F32)


def _dot_nt(a, b):
    return lax.dot_general(a, b, (((1,), (1,)), ((), ())), preferred_element_type=F32)


def _dot_tn(a, b):
    return lax.dot_general(a, b, (((0,), (0,)), ((), ())), preferred_element_type=F32)


def _split_bf16(x, parts):
    out = []
    r = x
    for _ in range(parts):
        p = r.astype(BF16)
        out.append(p)
        r = r - p.astype(F32)
    return out


def _rmsnorm_rows(x, g):
    ms = jnp.mean(x * x, axis=-1, keepdims=True)
    return x * lax.rsqrt(ms + EPS) * g


def _ada_kernel(c_ref, w_ref, b_ref, o_ref):
    s = jax.nn.silu(c_ref[...])
    s_hi, s_lo = _split_bf16(s, 2)
    w_hi, w_lo = _split_bf16(w_ref[...], 2)
    o_ref[...] = _dot(s_hi, w_hi) + _dot(s_hi, w_lo) + _dot(s_lo, w_hi) + b_ref[...]


def _ada(cc, w_ada, b_ada):
    rows = cc.shape[0]
    width = w_ada.shape[1]
    return pl.pallas_call(
        _ada_kernel,
        out_shape=jax.ShapeDtypeStruct((rows, width), F32),
        grid=(width // D_MODEL,),
        in_specs=[
            pl.BlockSpec((rows, D_MODEL), lambda j: (0, 0)),
            pl.BlockSpec((D_MODEL, D_MODEL), lambda j: (0, j)),
            pl.BlockSpec((1, D_MODEL), lambda j: (0, j)),
        ],
        out_specs=pl.BlockSpec((rows, D_MODEL), lambda j: (0, j)),
        compiler_params=pltpu.CompilerParams(
            dimension_semantics=("arbitrary",), vmem_limit_bytes=VMEM_LIMIT_BYTES),
        name="ada",
    )(cc, w_ada, b_ada)


def _head_norm(p, g):
    lane = lax.broadcasted_iota(jnp.int32, p.shape, 1)
    low = lane < HEAD_DIM
    sq = p * p
    s_low = jnp.sum(jnp.where(low, sq, 0.0), axis=-1, keepdims=True)
    s_high = jnp.sum(jnp.where(low, 0.0, sq), axis=-1, keepdims=True)
    ms = jnp.where(low, s_low, s_high) * (1.0 / HEAD_DIM)
    return p * lax.rsqrt(ms + EPS) * g


def _rope(p, cos, sin_signed):
    lane = lax.broadcasted_iota(jnp.int32, p.shape, 1)
    partner = jnp.where((lane & 16) == 0,
                        pltpu.roll(p, LANES - 16, 1), pltpu.roll(p, 16, 1))
    return p * cos + partner * sin_signed


def _inproj_latent_kernel(x_ref, mod_ref, g1_ref, w_ref, qg_ref, kg_ref, cos_ref, sin_ref,
                          q_ref, k_ref, v_ref, xr_ref, gg_ref):
    x = x_ref[...]
    h = _rmsnorm_rows(x, g1_ref[...]) * (1.0 + mod_ref[0, 1:2, :]) + mod_ref[0, 0:1, :]
    y = _dot(h.astype(BF16), w_ref[...])
    cos = cos_ref[...]
    sin = sin_ref[...]
    pieces = []
    for j in range(ATTN_WIDTH // LANES):
        p = _head_norm(y[:, j * LANES:(j + 1) * LANES], qg_ref[...])
        pieces.append(_rope(p, cos, sin))
    lane = lax.broadcasted_iota(jnp.int32, pieces[0].shape, 1)
    low = lane < HEAD_DIM
    n_blocks = y.shape[0] // BLOCK
    for g in range(Q_PER_KV):
        src0 = pieces[g // 2]
        src1 = pieces[2 + g // 2]
        if g % 2 == 0:
            out = jnp.where(low, src0, pltpu.roll(src1, HEAD_DIM, 1))
        else:
            out = jnp.where(low, pltpu.roll(src0, HEAD_DIM, 1), src1)
        out = (out * SCORE_SCALE).astype(BF16)
        for j in range(n_blocks):
            q_ref[0, j, g * BLOCK:(g + 1) * BLOCK, :] = out[j * BLOCK:(j + 1) * BLOCK, :]
    k = _head_norm(y[:, ATTN_WIDTH:ATTN_WIDTH + KV_WIDTH], kg_ref[...])
    k_ref[...] = _rope(k, cos, sin).astype(BF16)
    o = ATTN_WIDTH + KV_WIDTH
    v_ref[...] = y[:, o:o + KV_WIDTH].astype(BF16)
    o += KV_WIDTH
    xr_ref[...] = y[:, o:o + LRU_WIDTH]
    o += LRU_WIDTH
    gg_ref[...] = jax.nn.gelu(y[:, o:o + LRU_WIDTH])


def _inproj_ctx_kernel(x_ref, mod_ref, g1_ref, w_ref, kg_ref, k_ref, v_ref, xr_ref):
    x = x_ref[...]
    h = _rmsnorm_rows(x, g1_ref[...]) * (1.0 + mod_ref[0, 1:2, :]) + mod_ref[0, 0:1, :]
    y = _dot(h.astype(BF16), w_ref[...])
    k_ref[...] = _head_norm(y[:, 0:KV_WIDTH], kg_ref[...]).astype(BF16)
    v_ref[...] = y[:, KV_WIDTH:2 * KV_WIDTH].astype(BF16)
    xr_ref[...] = y[:, 2 * KV_WIDTH:2 * KV_WIDTH + LRU_WIDTH]


OUTPROJ_ROW_TILE = 1024


def _row_tile(n):
    return min(512, n)


def _inproj_latent(x2, mod3, g1, w_in, qg, kg, cos, sin, bsz, n):
    tm = _row_tile(n)
    tps = n // tm
    rows = bsz * n
    const = lambda i: (0, 0)
    return pl.pallas_call(
        _inproj_latent_kernel,
        out_shape=(
            jax.ShapeDtypeStruct((bsz, n // BLOCK, Q_PER_KV * BLOCK, LANES), BF16),
            jax.ShapeDtypeStruct((rows, KV_WIDTH), BF16),
            jax.ShapeDtypeStruct((rows, KV_WIDTH), BF16),
            jax.ShapeDtypeStruct((rows, LRU_WIDTH), F32),
            jax.ShapeDtypeStruct((rows, LRU_WIDTH), F32),
        ),
        grid=(rows // tm,),
        in_specs=[
            pl.BlockSpec((tm, D_MODEL), lambda i: (i, 0)),
            pl.BlockSpec((1, N_MOD, D_MODEL), lambda i: (i // tps, 0, 0)),
            pl.BlockSpec((1, D_MODEL), const),
            pl.BlockSpec((D_MODEL, IN_WIDTH), const),
            pl.BlockSpec((1, LANES), const),
            pl.BlockSpec((1, LANES), const),
            pl.BlockSpec((tm, LANES), lambda i: (i % tps, 0)),
            pl.BlockSpec((tm, LANES), lambda i: (i % tps, 0)),
        ],
        out_specs=(
            pl.BlockSpec((1, tm // BLOCK, Q_PER_KV * BLOCK, LANES),
                         lambda i: (i // tps, i % tps, 0, 0)),
            pl.BlockSpec((tm, KV_WIDTH), lambda i: (i, 0)),
            pl.BlockSpec((tm, KV_WIDTH), lambda i: (i, 0)),
            pl.BlockSpec((tm, LRU_WIDTH), lambda i: (i, 0)),
            pl.BlockSpec((tm, LRU_WIDTH), lambda i: (i, 0)),
        ),
        compiler_params=pltpu.CompilerParams(
            dimension_semantics=("arbitrary",), vmem_limit_bytes=VMEM_LIMIT_BYTES),
        name="inproj_latent",
    )(x2, mod3, g1, w_in, qg, kg, cos, sin)


def _inproj_ctx(c2, mod3, g1, w_ctx, kg, bsz, n_ctx):
    tm = _row_tile(n_ctx)
    rows = bsz * n_ctx
    ctx_row = mod3.shape[0] - 1
    width = w_ctx.shape[1]
    const = lambda i: (0, 0)
    return pl.pallas_call(
        _inproj_ctx_kernel,
        out_shape=(
            jax.ShapeDtypeStruct((rows, KV_WIDTH), BF16),
            jax.ShapeDtypeStruct((rows, KV_WIDTH), BF16),
            jax.ShapeDtypeStruct((rows, LRU_WIDTH), F32),
        ),
        grid=(rows // tm,),
        in_specs=[
            pl.BlockSpec((tm, D_MODEL), lambda i: (i, 0)),
            pl.BlockSpec((1, N_MOD, D_MODEL), lambda i: (ctx_row, 0, 0)),
            pl.BlockSpec((1, D_MODEL), const),
            pl.BlockSpec((D_MODEL, width), const),
            pl.BlockSpec((1, LANES), const),
        ],
        out_specs=(
            pl.BlockSpec((tm, KV_WIDTH), lambda i: (i, 0)),
            pl.BlockSpec((tm, KV_WIDTH), lambda i: (i, 0)),
            pl.BlockSpec((tm, LRU_WIDTH), lambda i: (i, 0)),
        ),
        compiler_params=pltpu.CompilerParams(
            dimension_semantics=("arbitrary",), vmem_limit_bytes=VMEM_LIMIT_BYTES),
        name="inproj_ctx",
    )(c2, mod3, g1, w_ctx, kg)


def _attn_kernel(sink_ref, q_ref, k_ref, v_ref, kc_ref, vc_ref, *refs, n, nb):
    bias_refs = refs[:ATTN_BLOCKS_PER_STEP]
    o_ref, s0_s, s1_s = refs[ATTN_BLOCKS_PER_STEP:]
    t = pl.program_id(0)
    last = pl.num_programs(0) - 2
    j_scored = (jnp.minimum(t, last) * ATTN_BLOCKS_PER_STEP) % nb
    j_finished = (jnp.maximum(t - 1, 0) * ATTN_BLOCKS_PER_STEP) % nb
    span = 3 * BLOCK

    def window_start(i):
        return pl.multiple_of(jnp.clip((i - 1) * BLOCK, 0, n - span), BLOCK)

    def head_only(t, h):
        lane = lax.broadcasted_iota(jnp.int32, t.shape, 1)
        keep = (lane < HEAD_DIM) if h == 0 else (lane >= HEAD_DIM)
        return jnp.where(keep, t, jnp.zeros_like(t))

    group = range(ATTN_BLOCKS_PER_STEP)
    stages = functools.partial(_attn_stages, sink_ref, q_ref, k_ref, v_ref, kc_ref, vc_ref,
                               bias_refs, o_ref,
                               starts_scored=[window_start(j_scored + u) for u in group],
                               starts_finished=[window_start(j_finished + u) for u in group],
                               head_only=head_only)

    @pl.when(t == 0)
    def _():
        s1_s[...] = jnp.zeros_like(s1_s)

    @pl.when(t % 2 == 0)
    def _():
        stages(s0_s, s1_s)

    @pl.when(t % 2 == 1)
    def _():
        stages(s1_s, s0_s)


def _attn_stages(sink_ref, q_ref, k_ref, v_ref, kc_ref, vc_ref, bias_refs, o_ref, s_new, s_old,
                 starts_scored, starts_finished, head_only):
    for u in range(ATTN_BLOCKS_PER_STEP):
        _attn_block_stages(sink_ref, q_ref.at[0, u], k_ref, v_ref, kc_ref, vc_ref, bias_refs[u],
                           o_ref.at[0, u], s_new.at[u], s_old.at[u], starts_scored[u],
                           starts_finished[u], head_only)


def _attn_block_stages(sink_ref, q_ref, k_ref, v_ref, kc_ref, vc_ref, bias_ref, o_ref, s_new,
                       s_old, start_scored, start_finished, head_only):
    span = 3 * BLOCK

    def with_ones(t, lane_index):
        lane = lax.broadcasted_iota(jnp.int32, t.shape, 1)
        return jnp.where(lane == lane_index, jnp.ones_like(t), t)

    q = q_ref[...]
    kw = k_ref[0, pl.ds(start_scored, span), :]
    kc = kc_ref[0]
    bias = bias_ref[0]
    for h in range(N_KV_HEADS):
        s_new[h, :, 0:span] = _dot_nt(q, head_only(kw, h)) + bias
        s_new[h, :, span:] = _dot_nt(q, head_only(kc, h))

    vw = v_ref[0, pl.ds(start_finished, span), :]
    vc = vc_ref[0]
    rows = q.shape[0]
    group = lax.broadcasted_iota(jnp.int32, (rows, 1), 0) // BLOCK
    acc = jnp.zeros((rows, LANES), F32)
    for h in range(N_KV_HEADS):
        s_loc = s_old[h, :, 0:span]
        s_ctx = s_old[h, :, span:]
        sink = jnp.zeros((rows, 1), F32)
        for g in range(Q_PER_KV):
            sink = jnp.where(group == g, sink_ref[h * Q_PER_KV + g] * LOG2_E, sink)
        m = jnp.maximum(jnp.maximum(jnp.max(s_loc, axis=-1, keepdims=True),
                                    jnp.max(s_ctx, axis=-1, keepdims=True)), sink)
        p_loc = jnp.exp2(s_loc - m).astype(BF16)
        p_ctx = jnp.exp2(s_ctx - m).astype(BF16)
        sum_lane = HEAD_DIM if h == 0 else 0
        o = (_dot(p_loc, with_ones(head_only(vw, h), sum_lane))
             + _dot(p_ctx, with_ones(head_only(vc, h), sum_lane)))
        denom = o[:, sum_lane:sum_lane + 1] + jnp.exp2(sink - m)
        lane = lax.broadcasted_iota(jnp.int32, o.shape, 1)
        own = (lane < HEAD_DIM) if h == 0 else (lane >= HEAD_DIM)
        acc = acc + jnp.where(own, o, 0.0) / denom
    o_ref[...] = acc.astype(BF16)


def _band_bias(n):
    span = 3 * BLOCK
    r = jnp.arange(Q_PER_KV * BLOCK)[:, None] % BLOCK
    c = jnp.arange(span)[None, :]
    offsets = jnp.arange(span // BLOCK)[:, None, None] * BLOCK
    valid = jnp.abs(r - c + offsets) <= WINDOW
    return jnp.where(valid, 0.0, NEG_INF).astype(F32)


def _attn(sink, q_s, k3, v3, kc3, vc3):
    bsz, nb = q_s.shape[0], q_s.shape[1]
    n = k3.shape[1]
    n_ctx = kc3.shape[1]
    per = ATTN_BLOCKS_PER_STEP
    assert nb >= 3 and nb % per == 0
    bias = _band_bias(n)
    n_groups = bsz * nb // per
    grouped = (n_groups, per) + q_s.shape[2:]

    def scored(t):
        return jnp.minimum(t, n_groups - 1)

    def finished(t):
        return jnp.maximum(t - 1, 0)

    def sample(g):
        return g * per // nb

    def placement(u):
        def index(t):
            i = (scored(t) * per + u) % nb
            return (jnp.minimum(i, 1) + (i == nb - 1).astype(jnp.int32), 0, 0)
        return index

    block = (1, per, Q_PER_KV * BLOCK, LANES)
    scores = pltpu.VMEM((per, N_KV_HEADS, Q_PER_KV * BLOCK, 3 * BLOCK + n_ctx), F32)

    return pl.pallas_call(
        functools.partial(_attn_kernel, n=n, nb=nb),
        out_shape=jax.ShapeDtypeStruct(grouped, BF16),
        grid=(n_groups + 1,),
        in_specs=[
            pl.BlockSpec(memory_space=pltpu.SMEM),
            pl.BlockSpec(block, lambda t: (scored(t), 0, 0, 0)),
            pl.BlockSpec((1, n, KV_WIDTH), lambda t: (sample(scored(t)), 0, 0)),
            pl.BlockSpec((1, n, KV_WIDTH), lambda t: (sample(finished(t)), 0, 0)),
            pl.BlockSpec((1, n_ctx, KV_WIDTH), lambda t: (sample(scored(t)), 0, 0)),
            pl.BlockSpec((1, n_ctx, KV_WIDTH), lambda t: (sample(finished(t)), 0, 0)),
        ] + [pl.BlockSpec((1, Q_PER_KV * BLOCK, 3 * BLOCK), placement(u)) for u in range(per)],
        out_specs=pl.BlockSpec(block, lambda t: (finished(t), 0, 0, 0)),
        scratch_shapes=[scores, scores],
        compiler_params=pltpu.CompilerParams(
            dimension_semantics=("arbitrary",), vmem_limit_bytes=VMEM_LIMIT_BYTES),
        name="attn",
    )(sink, q_s.reshape(grouped), k3, v3, kc3, vc3, *([bias] * per)).reshape(q_s.shape)


def _softplus(z):
    return jnp.maximum(z, 0.0) + jnp.log1p(jnp.exp(-jnp.abs(z)))


LRU_MAIN_SEG = 252


def _lru_plan(length):
    main, rest = divmod(length // SUBLANES, LRU_MAIN_SEG)
    if rest == 0:
        tails = ()
    elif rest % SUBLANES == 0 and rest > SUBLANES:
        tails = (rest - 5, 5)
    else:
        tails = (rest,)
    assert length % SUBLANES == 0 and all(t >= 2 for t in tails)
    return main, tails


def _lru_kernel(xr_ref, xrc_ref, gg_ref, cw_ref, cb_ref, wg_ref, bg_ref, lam_ref, o_ref,
                xc_s, xcc_s, hf_s, hb_s, *, n, n_ctx):
    cw = cw_ref[...]
    cb = cb_ref[...]
    sub = lax.broadcasted_iota(jnp.int32, (SUBLANES, LANES), 0)
    main_rows = SUBLANES * LRU_MAIN_SEG
    n_main, tails = _lru_plan(n)
    c_main, c_tails = _lru_plan(n_ctx)

    def static_tiles(n_main_tiles, tail_segs, with_main):
        tiles = [(k * main_rows, LRU_MAIN_SEG) for k in range(n_main_tiles)] if with_main else []
        t0 = n_main_tiles * main_rows
        for seg in tail_segs:
            tiles.append((t0, seg))
            t0 += SUBLANES * seg
        return tiles

    def aligned(v):
        return v if isinstance(v, int) else pl.multiple_of(v, SUBLANES)

    def conv_tile(src_ref, dst_ref, t0, seg, length):
        x = [src_ref[0, pl.ds(t0 + j, SUBLANES, stride=seg), :] for j in range(seg)]
        lo = aligned(jnp.maximum(t0 - SUBLANES, 0))
        hi = aligned(jnp.minimum(t0 + SUBLANES * seg, length - SUBLANES))
        prev = jnp.where(t0 > 0, src_ref[0, pl.ds(lo, SUBLANES), :], 0.0)
        nxt = jnp.where(t0 + SUBLANES * seg < length, src_ref[0, pl.ds(hi, SUBLANES), :], 0.0)

        def from_prev_segment(v, row):
            return jnp.where(sub == 0, row, pltpu.roll(v, 1, 0))

        def from_next_segment(v, row):
            return jnp.where(sub == SUBLANES - 1, row, pltpu.roll(v, SUBLANES - 1, 0))

        m1 = [from_prev_segment(x[seg - 1], prev[7:8])] + x[:seg - 1]
        m2 = [from_prev_segment(x[seg - 2], prev[6:7])] + m1[:seg - 1]
        p1 = x[1:] + [from_next_segment(x[0], nxt[0:1])]
        for j in range(seg):
            y = cb + cw[0:1] * m2[j]
            y = y + cw[1:2] * m1[j]
            y = y + cw[2:3] * x[j]
            y = y + cw[3:4] * p1[j]
            dst_ref[pl.ds(t0 + j * SUBLANES, SUBLANES), :] = y

    half_rate = [-0.5 * LRU_C * _softplus(-lam_ref[d:d + 1, :]) for d in range(2)]

    def coeffs(xc, d):
        w = wg_ref[0, :, 2 * LANES * d:2 * LANES * (d + 1)]
        g = _dot(xc.astype(BF16), w) + bg_ref[0, :, 2 * LANES * d:2 * LANES * (d + 1)]
        t_r = jnp.tanh(g[:, :LANES])
        t_i = jnp.tanh(g[:, LANES:])
        log_a = t_r * half_rate[d] + half_rate[d]
        a = jnp.exp(log_a)
        mult = jnp.sqrt(-jnp.tanh(log_a) * (a * a + 1.0))
        return a, mult * (t_i + 1.0) * xc

    def scan_tile(a, b, h_in, seg, reverse):
        order = list(range(seg - 1, -1, -1) if reverse else range(seg))
        local = [None] * seg
        prod = [None] * seg
        h = p = None
        for j in order:
            aj = a[j * SUBLANES:(j + 1) * SUBLANES, :]
            bj = b[j * SUBLANES:(j + 1) * SUBLANES, :]
            h = bj if h is None else aj * h + bj
            p = aj if p is None else aj * p
            local[j], prod[j] = h, p
        end = order[-1]
        carry = jnp.zeros((SUBLANES, LANES), F32)
        c = h_in
        for s in (range(SUBLANES - 1, -1, -1) if reverse else range(SUBLANES)):
            carry = jnp.where(sub == s, c, carry)
            c = prod[end][s:s + 1, :] * c + local[end][s:s + 1, :]
        return [local[j] + prod[j] * carry for j in range(seg)], c

    def run_tile(xc_ref, t0, seg, d, h, emit=None):
        a, b = coeffs(xc_ref[pl.ds(t0, SUBLANES * seg), :], d)
        states, h = scan_tile(a, b, h, seg, d == 1)
        if emit is not None:
            emit(t0, seg, states)
        return h

    def keep(dst_ref):
        def emit(t0, seg, states):
            for j in range(seg):
                dst_ref[pl.ds(t0 + j * SUBLANES, SUBLANES), :] = states[j]
        return emit

    def write_output(other_ref):
        def emit(t0, seg, states):
            for j in range(seg):
                rows = pl.ds(t0 + j, SUBLANES, stride=seg)
                other = other_ref[pl.ds(t0 + j * SUBLANES, SUBLANES), :]
                o_ref[0, rows, :] = (other + states[j]) * gg_ref[0, rows, :]
        return emit

    def main_t0(k):
        return pl.multiple_of(k * main_rows, SUBLANES)

    for t0, seg in static_tiles(c_main, c_tails, True):
        conv_tile(xrc_ref, xcc_s, t0, seg, n_ctx)

    def conv_body(k, carry):
        conv_tile(xr_ref, xc_s, main_t0(k), LRU_MAIN_SEG, n)
        return carry
    lax.fori_loop(0, n_main, conv_body, 0)
    for t0, seg in static_tiles(n_main, tails, False):
        conv_tile(xr_ref, xc_s, t0, seg, n)

    hf = jnp.zeros((1, LANES), F32)
    hb = jnp.zeros((1, LANES), F32)
    for t0, seg in static_tiles(c_main, c_tails, True):
        hf = run_tile(xcc_s, t0, seg, 0, hf)
    for t0, seg in reversed(static_tiles(c_main, c_tails, True)):
        hb = run_tile(xcc_s, t0, seg, 1, hb)
    for t0, seg in reversed(static_tiles(n_main, tails, False)):
        hb = run_tile(xc_s, t0, seg, 1, hb, keep(hb_s))

    def pair_body(first_visit):
        def body(k, carry):
            hf, hb = carry
            fwd_emit = keep(hf_s) if first_visit else write_output(hb_s)
            rev_emit = keep(hb_s) if first_visit else write_output(hf_s)
            hf = run_tile(xc_s, main_t0(k), LRU_MAIN_SEG, 0, hf, fwd_emit)
            hb = run_tile(xc_s, main_t0(n_main - 1 - k), LRU_MAIN_SEG, 1, hb, rev_emit)
            return hf, hb
        return body

    half = n_main // 2
    hf, hb = lax.fori_loop(0, half, pair_body(True), (hf, hb))
    if n_main % 2:
        hf = run_tile(xc_s, half * main_rows, LRU_MAIN_SEG, 0, hf, keep(hf_s))
        hb = run_tile(xc_s, half * main_rows, LRU_MAIN_SEG, 1, hb, write_output(hf_s))
    hf, hb = lax.fori_loop(n_main - half, n_main, pair_body(False), (hf, hb))
    for t0, seg in static_tiles(n_main, tails, False):
        hf = run_tile(xc_s, t0, seg, 0, hf, write_output(hb_s))


def _lru(xr3, xrc3, gg3, conv_w, conv_b, wg, bg, lam):
    bsz, n, _ = xr3.shape
    n_ctx = xrc3.shape[1]
    n_p = LRU_WIDTH // LANES
    return pl.pallas_call(
        functools.partial(_lru_kernel, n=n, n_ctx=n_ctx),
        out_shape=jax.ShapeDtypeStruct((bsz, n, LRU_WIDTH), F32),
        grid=(bsz, n_p),
        in_specs=[
            pl.BlockSpec((1, n, LANES), lambda b, p: (b, 0, p)),
            pl.BlockSpec((1, n_ctx, LANES), lambda b, p: (b, 0, p)),
            pl.BlockSpec((1, n, LANES), lambda b, p: (b, 0, p)),
            pl.BlockSpec((conv_w.shape[0], LANES), lambda b, p: (0, p)),
            pl.BlockSpec((1, LANES), lambda b, p: (0, p)),
            pl.BlockSpec((1, LANES, 4 * LANES), lambda b, p: (p, 0, 0)),
            pl.BlockSpec((1, 1, 4 * LANES), lambda b, p: (p, 0, 0)),
            pl.BlockSpec((2, LANES), lambda b, p: (0, p)),
        ],
        out_specs=pl.BlockSpec((1, n, LANES), lambda b, p: (b, 0, p)),
        scratch_shapes=[
            pltpu.VMEM((n, LANES), F32),
            pltpu.VMEM((n_ctx, LANES), F32),
            pltpu.VMEM((n, LANES), F32),
            pltpu.VMEM((n, LANES), F32),
        ],
        compiler_params=pltpu.CompilerParams(
            dimension_semantics=("arbitrary", "arbitrary"), vmem_limit_bytes=VMEM_LIMIT_BYTES),
        name="lru",
    )(xr3, xrc3, gg3, conv_w, conv_b, wg, bg, lam)


def _outproj_kernel(o_ref, rg_ref, x_ref, mod_ref, watt_ref, wrnn_ref, g2_ref, wr_ref,
                    x1t_ref, h2t_ref, lg_ref):
    tm = x_ref.shape[0]
    n_blocks = tm // BLOCK
    att = jnp.concatenate(
        [jnp.concatenate([o_ref[0, j, g * BLOCK:(g + 1) * BLOCK, :] for g in range(Q_PER_KV)],
                         axis=1) for j in range(n_blocks)], axis=0)
    y = _dot(att, watt_ref[...]) + _dot(rg_ref[...].astype(BF16), wrnn_ref[...])
    x1 = x_ref[...] + mod_ref[0, 2:3, :] * y
    h2 = _rmsnorm_rows(x1, g2_ref[...]) * (1.0 + mod_ref[0, 4:5, :]) + mod_ref[0, 3:4, :]
    for s in range(FEATURE_CHUNKS):
        rows = pl.ds(s, tm, stride=FEATURE_CHUNKS)
        x1t_ref[rows, :] = x1[:, s * LANES:(s + 1) * LANES]
        h2t_ref[rows, :] = h2[:, s * LANES:(s + 1) * LANES]
    lt = _dot_nt(wr_ref[...], h2.astype(BF16))
    for j in range(n_blocks):
        lg_ref[0, j] = lt[:, j * LANES:(j + 1) * LANES]


def _outproj(o_s, rg2, x2, mod3, w_att, w_rnn, g2, wr_t, bsz, n):
    tm = min(OUTPROJ_ROW_TILE, n)
    tps = n // tm
    rows = bsz * n
    const = lambda i: (0, 0)
    return pl.pallas_call(
        _outproj_kernel,
        out_shape=(
            jax.ShapeDtypeStruct((rows * FEATURE_CHUNKS, LANES), F32),
            jax.ShapeDtypeStruct((rows * FEATURE_CHUNKS, LANES), F32),
            jax.ShapeDtypeStruct((bsz, n // LANES, N_EXPERTS, LANES), F32),
        ),
        grid=(rows // tm,),
        in_specs=[
            pl.BlockSpec((1, tm // BLOCK, Q_PER_KV * BLOCK, LANES),
                         lambda i: (i // tps, i % tps, 0, 0)),
            pl.BlockSpec((tm, LRU_WIDTH), lambda i: (i, 0)),
            pl.BlockSpec((tm, D_MODEL), lambda i: (i, 0)),
            pl.BlockSpec((1, N_MOD, D_MODEL), lambda i: (i // tps, 0, 0)),
            pl.BlockSpec((ATTN_WIDTH, D_MODEL), const),
            pl.BlockSpec((LRU_WIDTH, D_MODEL), const),
            pl.BlockSpec((1, D_MODEL), const),
            pl.BlockSpec((N_EXPERTS, D_MODEL), const),
        ],
        out_specs=(
            pl.BlockSpec((tm * FEATURE_CHUNKS, LANES), lambda i: (i, 0)),
            pl.BlockSpec((tm * FEATURE_CHUNKS, LANES), lambda i: (i, 0)),
            pl.BlockSpec((1, tm // LANES, N_EXPERTS, LANES), lambda i: (i // tps, i % tps, 0, 0)),
        ),
        compiler_params=pltpu.CompilerParams(
            dimension_semantics=("arbitrary",), vmem_limit_bytes=VMEM_LIMIT_BYTES),
        name="outproj",
    )(o_s, rg2, x2, mod3, w_att, w_rnn, g2, wr_t)


def _topk_kernel(lg_ref, *refs, cap, n_cast):
    w_refs = refs[:n_cast]
    idx_ref, gate_ref = refs[n_cast:n_cast + 2]
    wb_refs = refs[n_cast + 2:2 * n_cast + 2]
    aff_s, cum_s, before_s = refs[2 * n_cast + 2:]
    for w_ref, wb_ref in zip(w_refs, wb_refs):
        def cast_rows(i, carry):
            rows = pl.ds(pl.multiple_of(i * CAST_ROWS, CAST_ROWS), CAST_ROWS)
            for k in range(w_ref.shape[0]):
                wb_ref[k, rows, :] = w_ref[k, rows, :].astype(BF16)
            return carry
        lax.fori_loop(0, w_ref.shape[1] // CAST_ROWS, cast_rows, 0)

    lg = lg_ref[0]
    n_chunks = lg.shape[0]
    m = jnp.max(lg, axis=1, keepdims=True)
    ex = jnp.exp(lg - m)
    aff = ex / jnp.sum(ex, axis=1, keepdims=True)
    aff_s[...] = aff

    def count(mask):
        c = jnp.sum(jnp.where(mask, 1.0, 0.0), axis=0, keepdims=True)
        return jnp.sum(c, axis=2, keepdims=True)

    def bit_body(it, thr):
        cand = thr | jnp.left_shift(jnp.int32(1), 30 - it)
        return jnp.where(count(aff >= pltpu.bitcast(cand, F32)) >= cap, cand, thr)
    thr = lax.fori_loop(0, 31, bit_body, jnp.zeros((1, N_EXPERTS, 1), jnp.int32))
    thr = pltpu.bitcast(thr, F32)

    tri = (lax.broadcasted_iota(jnp.int32, (LANES, LANES), 0)
           <= lax.broadcasted_iota(jnp.int32, (LANES, LANES), 1))
    tri = jnp.where(tri, 1.0, 0.0).astype(BF16)

    def chunk_cumsum(mask):
        flat = jnp.where(mask, 1.0, 0.0).astype(BF16).reshape(n_chunks * N_EXPERTS, LANES)
        inc = _dot(flat, tri).reshape(n_chunks, N_EXPERTS, LANES)
        run = jnp.zeros((N_EXPERTS, 1), F32)
        before = []
        for c in range(n_chunks):
            before.append(run)
            run = run + inc[c][:, LANES - 1:LANES]
        return inc, jnp.stack(before, axis=0)

    above = aff > thr
    tied = aff == thr
    need = cap - count(above)
    tie_inc, tie_before = chunk_cumsum(tied)
    tie_rank = tie_before + tie_inc - jnp.where(tied, 1.0, 0.0)
    sel = above | (tied & (tie_rank < need))
    sel_inc, sel_before = chunk_cumsum(sel)
    cum_s[...] = sel_inc
    before_s[...] = jnp.broadcast_to(sel_before, sel_inc.shape)

    slot = lax.broadcasted_iota(jnp.int32, (1, cap), 1).astype(F32)
    chunk_id = lax.broadcasted_iota(jnp.int32, (n_chunks, cap), 0).astype(F32)
    lane_id = lax.broadcasted_iota(jnp.int32, (LANES, cap), 0).astype(F32)
    for e in range(N_EXPERTS):
        cin = cum_s[:, e, :]
        cc_exc = before_s[:, e, :][:, 0:1]
        cc_inc = cc_exc + cin[:, LANES - 1:LANES]
        kc = jnp.sum(jnp.where(cc_inc <= slot, 1.0, 0.0), axis=0, keepdims=True)
        onehot = chunk_id == kc
        onehot_b = jnp.where(onehot, 1.0, 0.0).astype(BF16)
        counts_t = _dot_tn(cin.astype(BF16), onehot_b)
        local = slot - jnp.sum(jnp.where(onehot, cc_exc, 0.0), axis=0, keepdims=True)
        il = jnp.sum(jnp.where(counts_t <= local, 1.0, 0.0), axis=0, keepdims=True)
        idx_ref[0, e:e + 1, :] = ((kc * LANES + il) * FEATURE_CHUNKS).astype(jnp.int32)
        aff_t = jnp.zeros((LANES, cap), F32)
        for part in _split_bf16(aff_s[:, e, :], 3):
            aff_t = aff_t + _dot_tn(part, onehot_b)
        gate_ref[0, e:e + 1, :] = jnp.sum(jnp.where(lane_id == il, aff_t, 0.0),
                                          axis=0, keepdims=True)


def _topk(lg4, cap, expert_weights):
    bsz, n_chunks = lg4.shape[0], lg4.shape[1]
    if N_EXPERTS % bsz:
        cast_in, cast_out = (), tuple(w.astype(BF16) for w in expert_weights)
    else:
        cast_in, cast_out = tuple(expert_weights), ()
    per_step = N_EXPERTS // bsz if cast_in else 0
    w_specs = [pl.BlockSpec((per_step,) + w.shape[1:], lambda b: (b, 0, 0)) for w in cast_in]
    outs = pl.pallas_call(
        functools.partial(_topk_kernel, cap=cap, n_cast=len(cast_in)),
        out_shape=(
            jax.ShapeDtypeStruct((bsz, N_EXPERTS, cap), jnp.int32),
            jax.ShapeDtypeStruct((bsz, N_EXPERTS, cap), F32),
        ) + tuple(jax.ShapeDtypeStruct(w.shape, BF16) for w in cast_in),
        grid=(bsz,),
        in_specs=[pl.BlockSpec((1, n_chunks, N_EXPERTS, LANES), lambda b: (b, 0, 0, 0))] + w_specs,
        out_specs=(
            pl.BlockSpec((1, N_EXPERTS, cap), lambda b: (b, 0, 0)),
            pl.BlockSpec((1, N_EXPERTS, cap), lambda b: (b, 0, 0)),
        ) + tuple(w_specs),
        scratch_shapes=[
            pltpu.VMEM((n_chunks, N_EXPERTS, LANES), F32),
            pltpu.VMEM((n_chunks, N_EXPERTS, LANES), F32),
            pltpu.VMEM((n_chunks, N_EXPERTS, LANES), F32),
        ],
        compiler_params=pltpu.CompilerParams(
            dimension_semantics=("arbitrary",), vmem_limit_bytes=VMEM_LIMIT_BYTES),
        name="topk",
    )(lg4, *cast_in)
    return outs[0], outs[1], tuple(outs[2:]) + cast_out


def _moe_kernel(idx_ref, gate_ref, h2t_hbm, x1t_hbm, mod_ref, wg_ref, wu_ref, wd_ref, out_hbm,
                h2_s, acc_s, xs_s, ys_s, stage_s, sems, *, n, cap, sub):
    b = pl.program_id(0)
    e = pl.program_id(1)
    n_b = pl.num_programs(0)
    n_e = pl.num_programs(1)
    tok_rows = n * FEATURE_CHUNKS
    n_parts = cap // sub

    def load_h2(sample):
        return pltpu.make_async_copy(
            h2t_hbm.at[pl.ds(sample * tok_rows, tok_rows), :], h2_s, sems.at[0])

    def load_x1(sample):
        return pltpu.make_async_copy(
            x1t_hbm.at[pl.ds(sample * tok_rows, tok_rows), :], acc_s, sems.at[1])

    def load_x1_chunk(sample, c):
        rows = pl.ds(pl.multiple_of(c * (sub * FEATURE_CHUNKS), sub * FEATURE_CHUNKS),
                     sub * FEATURE_CHUNKS)
        return pltpu.make_async_copy(
            x1t_hbm.at[pl.ds(sample * tok_rows + c * (sub * FEATURE_CHUNKS),
                             sub * FEATURE_CHUNKS), :], acc_s.at[rows, :], sems.at[1])

    gates = gate_ref[0, 0]
    gl = gates.shape[1]
    gates = jnp.pad(gates, ((0, SUBLANES - gates.shape[0]), (0, LANES - gl)))
    gate_cols = gates.T
    gate2 = mod_ref[0, N_MOD - 1:N_MOD, :]

    def token_tile(first_row):
        return pl.ds(pl.multiple_of(first_row, FEATURE_CHUNKS), FEATURE_CHUNKS)

    def gather(part):
        for r in range(sub):
            t = idx_ref[0, 0, 0, part * sub + r]
            xs_s[part, r * FEATURE_CHUNKS:(r + 1) * FEATURE_CHUNKS, :] = h2_s[token_tile(t), :]

    def expert(part):
        x = jnp.concatenate(
            [xs_s[part, pl.ds(s, sub, stride=FEATURE_CHUNKS), :] for s in range(FEATURE_CHUNKS)],
            axis=1).astype(BF16)
        hid = (jax.nn.silu(_dot(x, wg_ref[0])) * _dot(x, wu_ref[0])).astype(BF16)
        y = _dot(hid, wd_ref[0]) * gate2
        first_col = part * sub // gl
        y = jnp.concatenate(
            [y[k * gl:(k + 1) * gl, :] * gate_cols[:gl, first_col + k:first_col + k + 1]
             for k in range(sub // gl)], axis=0)
        for s in range(FEATURE_CHUNKS):
            ys_s[part, pl.ds(s, sub, stride=FEATURE_CHUNKS), :] = y[:, s * LANES:(s + 1) * LANES]

    def scatter(part):
        for r0 in range(0, sub, SCATTER_UNROLL):
            updates = []
            for r in range(r0, r0 + SCATTER_UNROLL):
                dst = token_tile(idx_ref[0, 0, 0, part * sub + r])
                row = ys_s[part, r * FEATURE_CHUNKS:(r + 1) * FEATURE_CHUNKS, :]
                updates.append((dst, acc_s[dst, :] + row))
            for dst, val in updates:
                acc_s[dst, :] = val

    @pl.when((e == 0) & (b == 0))
    def _():
        load_h2(b).start()
        load_x1(b).start()

    @pl.when(e == 0)
    def _():
        load_h2(b).wait()

    for part in range(n_parts):
        gather(part)

    @pl.when((e == n_e - 1) & (b + 1 < n_b))
    def _():
        load_h2(b + 1).start(priority=1)

    expert(0)

    @pl.when(e == 0)
    def _():
        load_x1(b).wait()

    for part in range(1, n_parts):
        expert(part)
    for part in range(n_parts):
        scatter(part)

    @pl.when(e == n_e - 1)
    def _():
        n_out = n // sub

        def store(c, slot):
            return pltpu.make_async_copy(
                stage_s.at[slot], out_hbm.at[pl.ds(b * n + c * sub, sub), :], sems.at[2 + slot])

        def out_body(c, carry):
            slot = c % 2

            @pl.when(c >= 2)
            def _():
                store(c - 2, slot).wait()
            row0 = pl.multiple_of(c * (sub * FEATURE_CHUNKS), sub * FEATURE_CHUNKS)
            for s in range(FEATURE_CHUNKS):
                stage_s[slot, :, s * LANES:(s + 1) * LANES] = (
                    acc_s[pl.ds(row0 + s, sub, stride=FEATURE_CHUNKS), :])
            store(c, slot).start()

            @pl.when(b + 1 < n_b)
            def _():
                load_x1_chunk(b + 1, c).start(priority=1)
            return carry
        lax.fori_loop(0, n_out, out_body, 0)
        for c in range(n_out - 2, n_out):
            store(c, c % 2).wait()


def _moe(idx, gate, h2t, x1t, mod3, wg, wu, wd, bsz, n):
    cap = idx.shape[-1]
    sub = min(256, cap)
    gl = min(LANES, cap)
    assert n // sub >= 2 and cap % sub == 0 and sub % SCATTER_UNROLL == 0 and sub % gl == 0
    assert cap // gl <= SUBLANES
    tok_rows = n * FEATURE_CHUNKS
    weight_spec = pl.BlockSpec((1, D_MODEL, D_MODEL), lambda b, e: (e, 0, 0))
    smem_spec = pl.BlockSpec((1, 1, 1, cap), lambda b, e: (b, e, 0, 0),
                             memory_space=pltpu.SMEM)
    return pl.pallas_call(
        functools.partial(_moe_kernel, n=n, cap=cap, sub=sub),
        out_shape=jax.ShapeDtypeStruct((bsz * n, D_MODEL), F32),
        grid=(bsz, N_EXPERTS),
        in_specs=[
            smem_spec,
            pl.BlockSpec((1, 1, cap // gl, gl), lambda b, e: (b, e, 0, 0)),
            pl.BlockSpec(memory_space=pl.ANY),
            pl.BlockSpec(memory_space=pl.ANY),
            pl.BlockSpec((1, N_MOD, D_MODEL), lambda b, e: (b, 0, 0)),
            weight_spec,
            weight_spec,
            weight_spec,
        ],
        out_specs=pl.BlockSpec(memory_space=pl.ANY),
        scratch_shapes=[
            pltpu.VMEM((tok_rows, LANES), F32),
            pltpu.VMEM((tok_rows, LANES), F32),
            pltpu.VMEM((cap // sub, sub * FEATURE_CHUNKS, LANES), F32),
            pltpu.VMEM((cap // sub, sub * FEATURE_CHUNKS, LANES), F32),
            pltpu.VMEM((2, sub, D_MODEL), F32),
            pltpu.SemaphoreType.DMA((4,)),
        ],
        compiler_params=pltpu.CompilerParams(
            dimension_semantics=("arbitrary", "arbitrary"), vmem_limit_bytes=VMEM_LIMIT_BYTES),
        name="moe",
    )(idx[:, :, None, :], gate.reshape(bsz, N_EXPERTS, cap // gl, gl), h2t, x1t, mod3, wg, wu, wd)


def _rope_tables(n):
    t = jnp.arange(n)
    row = (t // GRID_W).astype(F32)
    col = (t % GRID_W).astype(F32)
    n_freq = HEAD_DIM // 4
    inv_freq = ROPE_BASE ** (-jnp.arange(n_freq, dtype=F32) / n_freq)
    ang_r = row[:, None] * inv_freq
    ang_c = col[:, None] * inv_freq
    cos = jnp.concatenate([jnp.cos(ang_r)] * 2 + [jnp.cos(ang_c)] * 2, axis=1)
    sin = jnp.concatenate([-jnp.sin(ang_r), jnp.sin(ang_r), -jnp.sin(ang_c), jnp.sin(ang_c)],
                          axis=1)
    return jnp.tile(cos, (1, 2)), jnp.tile(sin, (1, 2))


def _pack_gate_weights(w_r, b_r, w_i, b_i):
    n_p = LRU_WIDTH // LANES
    zeros = jnp.zeros((LRU_BLOCK_DIM, LRU_BLOCK_DIM), F32)

    def pair(w, p):
        top = jnp.concatenate([w[2 * p], zeros], axis=1)
        bot = jnp.concatenate([zeros, w[2 * p + 1]], axis=1)
        return jnp.concatenate([top, bot], axis=0)

    ws, bs = [], []
    for p in range(n_p):
        ws.append(jnp.concatenate(
            [pair(w_r[0], p), pair(w_i[0], p), pair(w_r[1], p), pair(w_i[1], p)], axis=1))
        sl = slice(p * LANES, (p + 1) * LANES)
        bs.append(jnp.concatenate([b_r[0, sl], b_i[0, sl], b_r[1, sl], b_i[1, sl]])[None, :])
    return jnp.stack(ws).astype(BF16), 0.5 * jnp.stack(bs)


def _layer(x, ctx, mod3, norm1_g, norm2_g, w_in, q_norm_g, k_norm_g, attn_sink, conv_w, conv_b,
           lru_w_r, lru_b_r, lru_w_i, lru_b_i, lru_lambda, w_out, w_router, w_gate, w_up, w_down):
    bsz, n, _ = x.shape
    n_ctx = ctx.shape[1]
    x2 = x.reshape(bsz * n, D_MODEL)
    c2 = ctx.reshape(bsz * n_ctx, D_MODEL)
    g1 = norm1_g[None, :]
    g2 = norm2_g[None, :]
    qg = jnp.tile(q_norm_g, 2)[None, :]
    kg = jnp.tile(k_norm_g, 2)[None, :]
    cos, sin = _rope_tables(n)
    w_in_b = w_in.astype(BF16)
    w_ctx_b = w_in_b[:, ATTN_WIDTH:ATTN_WIDTH + 2 * KV_WIDTH + LRU_WIDTH]

    q_s, k2, v2, xr2, gg2 = _inproj_latent(x2, mod3, g1, w_in_b, qg, kg, cos, sin, bsz, n)
    kc2, vc2, xrc2 = _inproj_ctx(c2, mod3, g1, w_ctx_b, kg, bsz, n_ctx)

    o_s = _attn(attn_sink, q_s, k2.reshape(bsz, n, KV_WIDTH), v2.reshape(bsz, n, KV_WIDTH),
                kc2.reshape(bsz, n_ctx, KV_WIDTH), vc2.reshape(bsz, n_ctx, KV_WIDTH))

    wg_lru, bg_lru = _pack_gate_weights(lru_w_r, lru_b_r, lru_w_i, lru_b_i)
    rg3 = _lru(xr2.reshape(bsz, n, LRU_WIDTH), xrc2.reshape(bsz, n_ctx, LRU_WIDTH),
               gg2.reshape(bsz, n, LRU_WIDTH), 0.5 * conv_w, 0.5 * conv_b[None, :], wg_lru,
               bg_lru, lru_lambda)

    w_att = (w_out[:ATTN_WIDTH].reshape(N_KV_HEADS, Q_PER_KV, HEAD_DIM, D_MODEL)
             .transpose(1, 0, 2, 3).reshape(ATTN_WIDTH, D_MODEL).astype(BF16))
    w_rnn = w_out[ATTN_WIDTH:].astype(BF16)
    x1t, h2t, lg4 = _outproj(o_s, rg3.reshape(bsz * n, LRU_WIDTH), x2, mod3, w_att, w_rnn, g2,
                             w_router.T.astype(BF16), bsz, n)

    cap = CAPACITY_FACTOR * n // N_EXPERTS
    idx, gate, (wg_b, wu_b, wd_b) = _topk(lg4, cap, (w_gate, w_up, w_down))
    out = _moe(idx, gate, h2t, x1t, mod3, wg_b, wu_b, wd_b, bsz, n)
    return out.reshape(bsz, n, D_MODEL)


def kernel(x, c, ctx, c_ctx, w_ada, b_ada, norm1_g, norm2_g, w_in, q_norm_g, k_norm_g, attn_sink, conv_w, conv_b, lru_w_r, lru_b_r, lru_w_i, lru_b_i, lru_lambda, w_out, w_router, w_gate, w_up, w_down):
    bsz = x.shape[0]
    depth = w_ada.shape[0]
    assert depth == 1, "context-stream update between layers is not implemented"
    pad_rows = -(bsz + 1) % SUBLANES
    cc = jnp.concatenate([c, jnp.zeros((pad_rows, D_MODEL), F32), c_ctx[None, :]], axis=0)
    for layer in range(depth):
        mod = _ada(cc, w_ada[layer], b_ada[layer][None, :])
        mod3 = mod.reshape(cc.shape[0], N_MOD, D_MODEL)
        x = _layer(x, ctx, mod3, norm1_g[layer], norm2_g[layer], w_in[layer], q_norm_g[layer],
                   k_norm_g[layer], attn_sink[layer], conv_w[layer], conv_b[layer],
                   lru_w_r[layer], lru_b_r[layer], lru_w_i[layer], lru_b_i[layer],
                   lru_lambda[layer], w_out[layer], w_router[layer], w_gate[layer],
                   w_up[layer], w_down[layer])
    return x
```
